```python
import jax, jax.numpy as jnp
from jax import lax
import numpy as np

D_MODEL = 1024
BATCH = 8
SEQ = 2048
DEPTH = 1
DEC_BATCH = 128
DEC_SEQ = 1
PAST_LEN = 16384
PAGE_SIZE = 128

N_BRANCH = 3
BRANCH_W = D_MODEL
POOL_WINDOWS = (2, 4, 8, 16)
POOL_GROUPS = len(POOL_WINDOWS)
POOL_GW = BRANCH_W // POOL_GROUPS
POOL_HIST = max(POOL_WINDOWS) - 1
SGU_CHUNK = 128
SGU_GROUPS = 4
SGU_GW = BRANCH_W // SGU_GROUPS
N_MEM = 256
XA_HEADS = 4
XA_HD = BRANCH_W // XA_HEADS
EPS = 1e-6
IN_COLS = 7 * BRANCH_W + N_BRANCH * D_MODEL

kernel_name = "hybrid_pool_sgu_memxattn_step"


def _rmsnorm(x, g):
    xf = x.astype(jnp.float32)
    y = xf * lax.rsqrt(jnp.mean(xf * xf, axis=-1, keepdims=True) + EPS)
    return (y * g.astype(jnp.float32)).astype(x.dtype)


def _pool_mix(p, hist, pos0, pool_w, pool_scale):
    B, S, _ = p.shape
    ext = jnp.concatenate([hist, p], axis=1).astype(jnp.float32)
    cs = jnp.pad(jnp.cumsum(ext, axis=1), ((0, 0), (1, 0), (0, 0)))
    pos = pos0 + jnp.arange(S)
    outs = []
    for g, w in enumerate(POOL_WINDOWS):
        sl = slice(g * POOL_GW, (g + 1) * POOL_GW)
        win = cs[:, POOL_HIST + 1:POOL_HIST + 1 + S, sl] - cs[:, POOL_HIST + 1 - w:POOL_HIST + 1 - w + S, sl]
        cnt = jnp.minimum(w, pos + 1).astype(jnp.float32)[None, :, None]
        outs.append(win / cnt - ext[:, POOL_HIST:, sl])
    d = jnp.stack(outs, axis=2)
    y = jnp.einsum('bsgc,gcd->bsgd', d, pool_w.astype(jnp.float32)).reshape(B, S, BRANCH_W)
    return (y * pool_scale.astype(jnp.float32)).astype(p.dtype)


def _spatial_gate(v, w_s, b_s):
    B, S, _ = v.shape
    n_chunk = -(-S // SGU_CHUNK)
    pad = n_chunk * SGU_CHUNK - S
    vp = jnp.pad(v, ((0, 0), (0, pad), (0, 0))).reshape(B, n_chunk, SGU_CHUNK, SGU_GROUPS, SGU_GW)
    mask = jnp.tril(jnp.ones((SGU_CHUNK, SGU_CHUNK), dtype=bool))
    w = jnp.where(mask[None], w_s, jnp.zeros_like(w_s))
    out = jnp.einsum('gts,bcsgd->bctgd', w, vp) + b_s.T[None, None, :, :, None]
    return out.reshape(B, n_chunk * SGU_CHUNK, BRANCH_W)[:, :S]


def _mem_kv(mem, mem_norm_g, w_kv):
    B = mem.shape[0]
    kv = _rmsnorm(mem, mem_norm_g) @ w_kv
    k, v = jnp.split(kv, 2, axis=-1)
    return (k.reshape(B, N_MEM, XA_HEADS, XA_HD), v.reshape(B, N_MEM, XA_HEADS, XA_HD))


def _cross_attn(q, k, v):
    B, S, _ = q.shape
    q = q.reshape(B, S, XA_HEADS, XA_HD)
    s = jnp.einsum('bqhd,bkhd->bhqk', q, k).astype(jnp.float32) * (XA_HD ** -0.5)
    pr = jax.nn.softmax(s, axis=-1).astype(v.dtype)
    return jnp.einsum('bhqk,bkhd->bqhd', pr, v).reshape(B, S, BRANCH_W)


def _layer(x, hist, pos0, mem_k, mem_v, norm_in_g, w_in, pool_w, pool_scale,
           sgu_norm_g, sgu_w, sgu_b, w_down, w_out):
    B, S, _ = x.shape
    h = _rmsnorm(x, norm_in_g)
    proj = h @ w_in
    cuts = [BRANCH_W * i for i in range(1, 8)]
    p_in, p_z, u, v, s_z, q, a_z, gates = jnp.split(proj, cuts, axis=-1)
    o_pool = _pool_mix(p_in, hist, pos0, pool_w, pool_scale) * jax.nn.silu(p_z)
    vn = _rmsnorm(v, sgu_norm_g)
    o_sgu = u * _spatial_gate(vn, sgu_w, sgu_b) * jax.nn.silu(s_z)
    o_xa = _cross_attn(q, mem_k, mem_v) * jax.nn.silu(a_z)
    br = jnp.stack([o_pool, o_sgu, o_xa], axis=2)
    yb = jnp.einsum('bsnc,ncd->bsnd', br, w_down)
    g = jax.nn.sigmoid(gates.reshape(B, S, N_BRANCH, D_MODEL))
    merged = jnp.sum(g * yb, axis=2)
    x = x + merged @ w_out
    new_hist = jnp.concatenate([hist, p_in], axis=1)[:, -POOL_HIST:]
    return x, new_hist, vn


def setup_inputs(seed: int = 0) -> dict:
    key = jax.random.key(seed)
    ks = jax.random.split(key, 24)
    f32 = jnp.float32
    nrm = lambda k, shp, s: jax.random.normal(k, shp, f32) * s
    return {
        "x_prompt": nrm(ks[0], (BATCH, SEQ, D_MODEL), 1.0),
        "x_sample": nrm(ks[1], (DEC_BATCH, DEC_SEQ, D_MODEL), 1.0),
        "state_pool": nrm(ks[2], (DEPTH, DEC_BATCH, POOL_HIST, BRANCH_W), 1.0),
        "cache_mem_k": nrm(ks[3], (DEPTH, DEC_BATCH, N_MEM, XA_HEADS, XA_HD), 1.0),
        "cache_mem_v": nrm(ks[4], (DEPTH, DEC_BATCH, N_MEM, XA_HEADS, XA_HD), 1.0),
        "mem_prompt": nrm(ks[5], (BATCH, N_MEM, D_MODEL), 1.0),
        "norm_in_g": 1.0 + nrm(ks[6], (DEPTH, D_MODEL), 0.02),
        "w_in": nrm(ks[7], (DEPTH, D_MODEL, IN_COLS), D_MODEL ** -0.5),
        "pool_w": nrm(ks[8], (DEPTH, POOL_GROUPS, POOL_GW, POOL_GW), POOL_GW ** -0.5),
        "pool_scale": 1.0 + nrm(ks[9], (DEPTH, BRANCH_W), 0.1),
        "sgu_norm_g": 1.0 + nrm(ks[10], (DEPTH, BRANCH_W), 0.02),
        "sgu_w": nrm(ks[11], (DEPTH, SGU_GROUPS, SGU_CHUNK, SGU_CHUNK), SGU_CHUNK ** -0.5),
        "sgu_b": 1.0 + nrm(ks[12], (DEPTH, SGU_GROUPS, SGU_CHUNK), 0.01),
        "mem_norm_g": 1.0 + nrm(ks[13], (DEPTH, D_MODEL), 0.02),
        "w_kv": nrm(ks[14], (DEPTH, D_MODEL, 2 * BRANCH_W), D_MODEL ** -0.5),
        "w_down": nrm(ks[15], (DEPTH, N_BRANCH, BRANCH_W, D_MODEL), BRANCH_W ** -0.5),
        "w_out": nrm(ks[16], (DEPTH, D_MODEL, D_MODEL), D_MODEL ** -0.5),
        "norm_f_g": 1.0 + nrm(ks[17], (D_MODEL,), 0.02),
    }


def reference(x_prompt, x_sample, state_pool, cache_mem_k, cache_mem_v, mem_prompt,
              norm_in_g, w_in, pool_w, pool_scale, sgu_norm_g, sgu_w, sgu_b,
              mem_norm_g, w_kv, w_down, w_out, norm_f_g):
    xp, xs = x_prompt, x_sample
    hist_p0 = jnp.zeros((x_prompt.shape[0], POOL_HIST, BRANCH_W), x_prompt.dtype)
    pool_p, pool_s, mk_p, mv_p, sgu_v_s = [], [], [], [], []
    for l in range(DEPTH):
        lw = (norm_in_g[l], w_in[l], pool_w[l], pool_scale[l], sgu_norm_g[l], sgu_w[l],
              sgu_b[l], w_down[l], w_out[l])
        k_p, v_p = _mem_kv(mem_prompt, mem_norm_g[l], w_kv[l])
        xp, hp, _ = _layer(xp, hist_p0, 0, k_p, v_p, *lw)
        xs, hs, vs = _layer(xs, state_pool[l], PAST_LEN, cache_mem_k[l], cache_mem_v[l], *lw)
        pool_p.append(hp)
        pool_s.append(hs)
        mk_p.append(k_p)
        mv_p.append(v_p)
        sgu_v_s.append(vs)
    y_prompt = _rmsnorm(xp, norm_f_g)
    y_sample = _rmsnorm(xs, norm_f_g)
    return (y_prompt, y_sample, jnp.stack(pool_p), jnp.stack(pool_s), jnp.stack(mk_p),
            jnp.stack(mv_p), jnp.stack(sgu_v_s))
```

```python
import functools

import jax
import jax.numpy as jnp
from jax import lax
from jax.experimental import pallas as pl
from jax.experimental.pallas import tpu as pltpu

D_MODEL = 1024
BRANCH_W = 1024
N_BRANCH = 3
POOL_WINDOWS = (2, 4, 8, 16)
POOL_GW = BRANCH_W // len(POOL_WINDOWS)
POOL_HIST = max(POOL_WINDOWS) - 1
HIST_ROWS = POOL_HIST + 1
SGU_CHUNK = 128
SGU_GROUPS = 4
SGU_GW = BRANCH_W // SGU_GROUPS
N_MEM = 256
XA_HEADS = 4
XA_HD = BRANCH_W // XA_HEADS
EPS = 1e-6
PAST_LEN = 16384

SEQ_TILE = 512
ATTN_REQ_BLOCK = 8
V7X_VMEM_LIMIT_BYTES = 60 * 1024 * 1024

F32 = jnp.float32
BF16 = jnp.bfloat16


def _rmsnorm(x, g):
    return x * lax.rsqrt(jnp.mean(x * x, axis=-1, keepdims=True) + EPS) * g


def _silu(z):
    return z * jax.nn.sigmoid(z)


def _dot(a, b):
    return jnp.dot(a, b, preferred_element_type=F32)


def _resident(shape):
    zeros = (0,) * len(shape)
    return pl.BlockSpec(shape, lambda *_: zeros, pipeline_mode=pl.Buffered(1))


def _mem_kv_kernel(mem_ref, g_ref, wkv_ref, k_ref, v_ref, kb_ref, vb_ref):
    m = _rmsnorm(mem_ref[0], g_ref[...]).astype(BF16)
    kv = _dot(m, wkv_ref[...])
    k = kv[:, :BRANCH_W]
    v = kv[:, BRANCH_W:]
    k_ref[0] = k
    v_ref[0] = v
    kb_ref[0] = k.astype(BF16)
    vb_ref[0] = v.astype(BF16)


def _mem_kv(mem, g, wkv_b):
    nb = mem.shape[0]
    blk = lambda: pl.BlockSpec((1, N_MEM, BRANCH_W), lambda b: (b, 0, 0))
    return pl.pallas_call(
        _mem_kv_kernel,
        grid=(nb,),
        in_specs=[pl.BlockSpec((1, N_MEM, D_MODEL), lambda b: (b, 0, 0)),
                  _resident((1, D_MODEL)),
                  _resident((D_MODEL, 2 * BRANCH_W))],
        out_specs=[blk(), blk(), blk(), blk()],
        out_shape=[jax.ShapeDtypeStruct((nb, N_MEM, BRANCH_W), F32),
                   jax.ShapeDtypeStruct((nb, N_MEM, BRANCH_W), F32),
                   jax.ShapeDtypeStruct((nb, N_MEM, BRANCH_W), BF16),
                   jax.ShapeDtypeStruct((nb, N_MEM, BRANCH_W), BF16)],
        compiler_params=pltpu.CompilerParams(dimension_semantics=("arbitrary",),
                                             vmem_limit_bytes=V7X_VMEM_LIMIT_BYTES),
        name="mem_kv",
    )(mem, g, wkv_b)


def _prompt_kernel(x_ref, k_ref, v_ref, gin_ref, win_ref, poolw_ref, pscale_ref, sgug_ref,
                   sguw_ref, sgub_ref, wdown_ref, wout_ref, gf_ref,
                   y_ref, hist_ref, ext_ref):
    i = pl.program_id(1)
    ts = x_ref.shape[1]
    x = x_ref[0]
    hb = _rmsnorm(x, gin_ref[...]).astype(BF16)

    def proj(c):
        return _dot(hb, win_ref[:, c * BRANCH_W:(c + 1) * BRANCH_W])

    p_in = proj(0)

    @pl.when(i == 0)
    def _():
        ext_ref[0:HIST_ROWS, :] = jnp.zeros((HIST_ROWS, BRANCH_W), F32)

    ext_ref[HIST_ROWS:HIST_ROWS + ts, :] = p_in
    pos = i * ts + lax.broadcasted_iota(jnp.int32, (ts, 1), 0)
    mixed = []
    for g, w in enumerate(POOL_WINDOWS):
        sl = slice(g * POOL_GW, (g + 1) * POOL_GW)
        cur = p_in[:, sl]
        win = cur
        for j in range(1, w):
            win = win + ext_ref[HIST_ROWS - j:HIST_ROWS - j + ts, sl]
        cnt = jnp.minimum(w, pos + 1).astype(F32)
        d = win / cnt - cur
        mixed.append(_dot(d.astype(BF16), poolw_ref[g]))
    o_pool = jnp.concatenate(mixed, axis=1) * pscale_ref[...] * _silu(proj(1))
    ext_ref[0:HIST_ROWS, :] = p_in[ts - HIST_ROWS:, :]

    @pl.when(i == pl.num_programs(1) - 1)
    def _():
        hist_ref[0] = p_in[ts - HIST_ROWS:, :]

    u = proj(2)
    vnb = _rmsnorm(proj(3), sgug_ref[...]).astype(BF16)
    tril = (lax.broadcasted_iota(jnp.int32, (SGU_CHUNK, SGU_CHUNK), 0)
            >= lax.broadcasted_iota(jnp.int32, (SGU_CHUNK, SGU_CHUNK), 1))
    ws = [jnp.where(tril, sguw_ref[g], 0.0).astype(BF16) for g in range(SGU_GROUPS)]
    rows = []
    for c in range(ts // SGU_CHUNK):
        rs = slice(c * SGU_CHUNK, (c + 1) * SGU_CHUNK)
        cols = [_dot(ws[g], vnb[rs, g * SGU_GW:(g + 1) * SGU_GW]) + sgub_ref[:, g:g + 1]
                for g in range(SGU_GROUPS)]
        rows.append(jnp.concatenate(cols, axis=1))
    o_sgu = u * jnp.concatenate(rows, axis=0) * _silu(proj(4))

    qb = proj(5).astype(BF16)
    heads = []
    for hd in range(XA_HEADS):
        sl = slice(hd * XA_HD, (hd + 1) * XA_HD)
        s = lax.dot_general(qb[:, sl], k_ref[0, :, sl], (((1,), (1,)), ((), ())),
                            preferred_element_type=F32) * (XA_HD ** -0.5)
        e = jnp.exp(s - jnp.max(s, axis=-1, keepdims=True))
        pr = e / jnp.sum(e, axis=-1, keepdims=True)
        heads.append(_dot(pr.astype(BF16), v_ref[0, :, sl]))
    o_xa = jnp.concatenate(heads, axis=1) * _silu(proj(6))

    merged = None
    for n, o in enumerate((o_pool, o_sgu, o_xa)):
        t = jax.nn.sigmoid(proj(7 + n)) * _dot(o.astype(BF16), wdown_ref[n])
        merged = t if merged is None else merged + t
    xn = x + _dot(merged.astype(BF16), wout_ref[...])
    y_ref[0] = _rmsnorm(xn, gf_ref[...])


def _prompt_layer(x, kb, vb, gin, win_b, poolw_b, pscale, sgug, sguw, sgub_t, wdown_b, wout_b, gf):
    nb, seq, _ = x.shape
    ts = SEQ_TILE
    in_cols = win_b.shape[1]
    return pl.pallas_call(
        _prompt_kernel,
        grid=(nb, seq // ts),
        in_specs=[pl.BlockSpec((1, ts, D_MODEL), lambda b, i: (b, i, 0)),
                  pl.BlockSpec((1, N_MEM, BRANCH_W), lambda b, i: (b, 0, 0)),
                  pl.BlockSpec((1, N_MEM, BRANCH_W), lambda b, i: (b, 0, 0)),
                  _resident((1, D_MODEL)),
                  _resident((D_MODEL, in_cols)),
                  _resident((len(POOL_WINDOWS), POOL_GW, POOL_GW)),
                  _resident((1, BRANCH_W)),
                  _resident((1, BRANCH_W)),
                  _resident((SGU_GROUPS, SGU_CHUNK, SGU_CHUNK)),
                  _resident((SGU_CHUNK, SGU_GROUPS)),
                  _resident((N_BRANCH, BRANCH_W, D_MODEL)),
                  _resident((D_MODEL, D_MODEL)),
                  _resident((1, D_MODEL))],
        out_specs=[pl.BlockSpec((1, ts, D_MODEL), lambda b, i: (b, i, 0)),
                   pl.BlockSpec((1, HIST_ROWS, BRANCH_W), lambda b, i: (b, 0, 0))],
        out_shape=[jax.ShapeDtypeStruct((nb, seq, D_MODEL), F32),
                   jax.ShapeDtypeStruct((nb, HIST_ROWS, BRANCH_W), F32)],
        scratch_shapes=[pltpu.VMEM((HIST_ROWS + ts, BRANCH_W), F32)],
        compiler_params=pltpu.CompilerParams(dimension_semantics=("arbitrary", "arbitrary"),
                                             vmem_limit_bytes=V7X_VMEM_LIMIT_BYTES),
        name="prompt_layer",
    )(x, kb, vb, gin, win_b, poolw_b, pscale, sgug, sguw, sgub_t, wdown_b, wout_b, gf)


def _sample_pre_kernel(x_ref, hist_ref, gin_ref, win_ref, poolw_ref, pscale_ref, sgug_ref,
                       sgw0_ref, sgb0_ref,
                       newhist_ref, vn_ref, opool_ref, osgu_ref, q_ref, az_ref, gates_ref):
    hb = _rmsnorm(x_ref[...], gin_ref[...]).astype(BF16)

    def proj(c, width=BRANCH_W):
        return _dot(hb, win_ref[:, c * BRANCH_W:c * BRANCH_W + width])

    p_in = proj(0)
    newhist_ref[0:POOL_HIST - 1] = hist_ref[1:POOL_HIST]
    newhist_ref[POOL_HIST - 1] = p_in
    mixed = []
    for g, w in enumerate(POOL_WINDOWS):
        sl = slice(g * POOL_GW, (g + 1) * POOL_GW)
        cur = p_in[:, sl]
        win = cur
        for j in range(1, w):
            win = win + hist_ref[POOL_HIST - j, :, sl]
        cnt = float(min(w, PAST_LEN + 1))
        d = win / cnt - cur
        mixed.append(_dot(d.astype(BF16), poolw_ref[g]))
    opool_ref[...] = jnp.concatenate(mixed, axis=1) * pscale_ref[...] * _silu(proj(1))

    vn = _rmsnorm(proj(3), sgug_ref[...])
    vn_ref[...] = vn
    osgu_ref[...] = proj(2) * (vn * sgw0_ref[...] + sgb0_ref[...]) * _silu(proj(4))
    q_ref[...] = proj(5)
    az_ref[...] = proj(6)
    gates_ref[...] = proj(7, N_BRANCH * D_MODEL)


def _sample_pre(xs, hist, gin, win_b, poolw_b, pscale, sgug, sgw0, sgb0):
    n = xs.shape[0]
    row = jax.ShapeDtypeStruct((n, BRANCH_W), F32)
    return pl.pallas_call(
        _sample_pre_kernel,
        out_shape=[jax.ShapeDtypeStruct(hist.shape, F32), row, row, row, row, row,
                   jax.ShapeDtypeStruct((n, N_BRANCH * D_MODEL), F32)],
        compiler_params=pltpu.CompilerParams(vmem_limit_bytes=V7X_VMEM_LIMIT_BYTES),
        name="sample_pre",
    )(xs, hist, gin, win_b, poolw_b, pscale, sgug, sgw0, sgb0)


def _sample_attn_kernel(q_ref, k_ref, v_ref, o_ref):
    for r in range(q_ref.shape[0]):
        s = jnp.sum(k_ref[r] * q_ref[r], axis=-1, keepdims=True) * (XA_HD ** -0.5)
        e = jnp.exp(s - jnp.max(s, axis=0, keepdims=True))
        pr = e / jnp.sum(e, axis=0, keepdims=True)
        o_ref[r] = jnp.sum(pr * v_ref[r], axis=0)


def _sample_attn(q, k, v):
    n = q.shape[0]
    rb = ATTN_REQ_BLOCK
    kv_spec = lambda: pl.BlockSpec((rb, N_MEM, XA_HEADS, XA_HD), lambda i: (i, 0, 0, 0))
    return pl.pallas_call(
        _sample_attn_kernel,
        grid=(n // rb,),
        in_specs=[pl.BlockSpec((rb, XA_HEADS, XA_HD), lambda i: (i, 0, 0)), kv_spec(), kv_spec()],
        out_specs=pl.BlockSpec((rb, XA_HEADS, XA_HD), lambda i: (i, 0, 0)),
        out_shape=jax.ShapeDtypeStruct((n, XA_HEADS, XA_HD), F32),
        compiler_params=pltpu.CompilerParams(dimension_semantics=("arbitrary",),
                                             vmem_limit_bytes=V7X_VMEM_LIMIT_BYTES),
        name="sample_attn",
    )(q, k, v)


def _sample_post_kernel(x_ref, opool_ref, osgu_ref, attn_ref, az_ref, gates_ref, wdown_ref, wout_ref,
                        gf_ref, y_ref):
    o_xa = attn_ref[...] * _silu(az_ref[...])
    merged = None
    for n, o in enumerate((opool_ref[...], osgu_ref[...], o_xa)):
        gate = jax.nn.sigmoid(gates_ref[:, n * D_MODEL:(n + 1) * D_MODEL])
        t = gate * _dot(o.astype(BF16), wdown_ref[n])
        merged = t if merged is None else merged + t
    xn = x_ref[...] + _dot(merged.astype(BF16), wout_ref[...])
    y_ref[...] = _rmsnorm(xn, gf_ref[...])


def _sample_post(xs, o_pool, o_sgu, attn, a_z, gates, wdown_b, wout_b, gf):
    return pl.pallas_call(
        _sample_post_kernel,
        out_shape=jax.ShapeDtypeStruct(xs.shape, F32),
        compiler_params=pltpu.CompilerParams(vmem_limit_bytes=V7X_VMEM_LIMIT_BYTES),
        name="sample_post",
    )(xs, o_pool, o_sgu, attn, a_z, gates, wdown_b, wout_b, gf)


def kernel(x_prompt, x_sample, state_pool, cache_mem_k, cache_mem_v, mem_prompt, norm_in_g, w_in,
           pool_w, pool_scale, sgu_norm_g, sgu_w, sgu_b, mem_norm_g, w_kv, w_down, w_out, norm_f_g):
    depth = w_in.shape[0]
    assert depth == 1, "single-layer step"
    nb, seq, _ = x_prompt.shape
    ns, dec_seq, _ = x_sample.shape
    assert dec_seq == 1 and seq % SEQ_TILE == 0 and seq >= HIST_ROWS and ns % ATTN_REQ_BLOCK == 0

    row = lambda a: a.reshape(1, -1)
    gin, pscale, sgug, gmem, gf = (row(norm_in_g[0]), row(pool_scale[0]), row(sgu_norm_g[0]),
                                   row(mem_norm_g[0]), row(norm_f_g))
    win_b = w_in[0].astype(BF16)
    poolw_b = pool_w[0].astype(BF16)
    wkv_b = w_kv[0].astype(BF16)
    wdown_b = w_down[0].astype(BF16)
    wout_b = w_out[0].astype(BF16)
    sguw = sgu_w[0]
    sgub_t = sgu_b[0].T
    sgw0 = jnp.repeat(sgu_w[0, :, 0, 0], SGU_GW).reshape(1, BRANCH_W)
    sgb0 = jnp.repeat(sgu_b[0, :, 0], SGU_GW).reshape(1, BRANCH_W)

    k_p, v_p, kb, vb = _mem_kv(mem_prompt, gmem, wkv_b)
    y_prompt, hist_p = _prompt_layer(x_prompt, kb, vb, gin, win_b, poolw_b, pscale, sgug, sguw, sgub_t,
                                     wdown_b, wout_b, gf)

    xs = x_sample.reshape(ns, D_MODEL)
    hist_s = jnp.transpose(state_pool[0], (1, 0, 2))
    new_hist_s, vn_s, o_pool_s, o_sgu_s, q_s, az_s, gates_s = _sample_pre(
        xs, hist_s, gin, win_b, poolw_b, pscale, sgug, sgw0, sgb0)
    attn_s = _sample_attn(q_s.reshape(ns, XA_HEADS, XA_HD), cache_mem_k[0], cache_mem_v[0])
    y_sample = _sample_post(xs, o_pool_s, o_sgu_s, attn_s.reshape(ns, BRANCH_W), az_s, gates_s,
                            wdown_b, wout_b, gf)

    new_pool_p = hist_p[None, :, HIST_ROWS - POOL_HIST:, :]
    new_pool_s = jnp.transpose(new_hist_s, (1, 0, 2))[None]
    kv_shape = (1, nb, N_MEM, XA_HEADS, XA_HD)
    return (y_prompt, y_sample.reshape(ns, 1, D_MODEL), new_pool_p, new_pool_s,
            k_p.reshape(kv_shape), v_p.reshape(kv_shape), vn_s.reshape(1, ns, 1, BRANCH_W))
```

```python
import functools

import jax
import jax.numpy as jnp
from jax import lax
from jax.experimental import pallas as pl
from jax.experimental.pallas import tpu as pltpu

D_MODEL = 1024
BRANCH_W = 1024
N_BRANCH = 3
POOL_WINDOWS = (2, 4, 8, 16)
POOL_GW = BRANCH_W // len(POOL_WINDOWS)
POOL_HIST = max(POOL_WINDOWS) - 1
HIST_ROWS = POOL_HIST + 1
SGU_CHUNK = 128
SGU_GROUPS = 4
SGU_GW = BRANCH_W // SGU_GROUPS
N_MEM = 256
XA_HEADS = 4
XA_HD = BRANCH_W // XA_HEADS
EPS = 1e-6
PAST_LEN = 16384

SEQ_TILE = 256
ATTN_REQ_BLOCK = 8
V7X_VMEM_LIMIT_BYTES = 60 * 1024 * 1024

F32 = jnp.float32
BF16 = jnp.bfloat16


def _rmsnorm(x, g):
    return x * lax.rsqrt(jnp.mean(x * x, axis=-1, keepdims=True) + EPS) * g


def _silu(z):
    return z * jax.nn.sigmoid(z)


def _dot(a, b):
    return jnp.dot(a, b, preferred_element_type=F32)


def _resident(shape):
    zeros = (0,) * len(shape)
    return pl.BlockSpec(shape, lambda *_: zeros, pipeline_mode=pl.Buffered(1))


def _mem_kv_kernel(mem_ref, g_ref, wkv_ref, k_ref, v_ref, kb_ref, vb_ref):
    m = _rmsnorm(mem_ref[0], g_ref[...]).astype(BF16)
    kv = _dot(m, wkv_ref[...])
    k = kv[:, :BRANCH_W]
    v = kv[:, BRANCH_W:]
    k_ref[0] = k
    v_ref[0] = v
    kb_ref[0] = k.astype(BF16)
    vb_ref[0] = v.astype(BF16)


def _mem_kv(mem, g, wkv_b):
    nb = mem.shape[0]
    blk = lambda: pl.BlockSpec((1, N_MEM, BRANCH_W), lambda b: (b, 0, 0))
    return pl.pallas_call(
        _mem_kv_kernel,
        grid=(nb,),
        in_specs=[pl.BlockSpec((1, N_MEM, D_MODEL), lambda b: (b, 0, 0)),
                  _resident((1, D_MODEL)),
                  _resident((D_MODEL, 2 * BRANCH_W))],
        out_specs=[blk(), blk(), blk(), blk()],
        out_shape=[jax.ShapeDtypeStruct((nb, N_MEM, BRANCH_W), F32),
                   jax.ShapeDtypeStruct((nb, N_MEM, BRANCH_W), F32),
                   jax.ShapeDtypeStruct((nb, N_MEM, BRANCH_W), BF16),
                   jax.ShapeDtypeStruct((nb, N_MEM, BRANCH_W), BF16)],
        compiler_params=pltpu.CompilerParams(dimension_semantics=("arbitrary",),
                                             vmem_limit_bytes=V7X_VMEM_LIMIT_BYTES),
        name="mem_kv",
    )(mem, g, wkv_b)


def _prompt_kernel(x_ref, k_ref, v_ref, gin_ref, win_ref, poolw_ref, pscale_ref, sgug_ref,
                   sguw_ref, sgub_ref, wdown_ref, wout_ref, gf_ref,
                   y_ref, hist_ref, ext_ref):
    i = pl.program_id(1)
    ts = x_ref.shape[1]
    x = x_ref[0]
    hb = _rmsnorm(x, gin_ref[...]).astype(BF16)

    def proj(c):
        return _dot(hb, win_ref[:, c * BRANCH_W:(c + 1) * BRANCH_W])

    p_in = proj(0)

    @pl.when(i == 0)
    def _():
        ext_ref[0:HIST_ROWS, :] = jnp.zeros((HIST_ROWS, BRANCH_W), F32)

    ext_ref[HIST_ROWS:HIST_ROWS + ts, :] = p_in
    pos = i * ts + lax.broadcasted_iota(jnp.int32, (ts, 1), 0)
    mixed = []
    for g, w in enumerate(POOL_WINDOWS):
        sl = slice(g * POOL_GW, (g + 1) * POOL_GW)
        cur = p_in[:, sl]
        win = cur
        for j in range(1, w):
            win = win + ext_ref[HIST_ROWS - j:HIST_ROWS - j + ts, sl]
        cnt = jnp.minimum(w, pos + 1).astype(F32)
        d = win / cnt - cur
        mixed.append(_dot(d.astype(BF16), poolw_ref[g]))
    o_pool = jnp.concatenate(mixed, axis=1) * pscale_ref[...] * _silu(proj(1))
    ext_ref[0:HIST_ROWS, :] = p_in[ts - HIST_ROWS:, :]

    @pl.when(i == pl.num_programs(1) - 1)
    def _():
        hist_ref[0] = p_in[ts - HIST_ROWS:, :]

    u = proj(2)
    vnb = _rmsnorm(proj(3), sgug_ref[...]).astype(BF16)
    tril = (lax.broadcasted_iota(jnp.int32, (SGU_CHUNK, SGU_CHUNK), 0)
            >= lax.broadcasted_iota(jnp.int32, (SGU_CHUNK, SGU_CHUNK), 1))
    ws = [jnp.where(tril, sguw_ref[g], 0.0).astype(BF16) for g in range(SGU_GROUPS)]
    rows = []
    for c in range(ts // SGU_CHUNK):
        rs = slice(c * SGU_CHUNK, (c + 1) * SGU_CHUNK)
        cols = [_dot(ws[g], vnb[rs, g * SGU_GW:(g + 1) * SGU_GW]) + sgub_ref[:, g:g + 1]
                for g in range(SGU_GROUPS)]
        rows.append(jnp.concatenate(cols, axis=1))
    o_sgu = u * jnp.concatenate(rows, axis=0) * _silu(proj(4))

    qb = proj(5).astype(BF16)
    heads = []
    for hd in range(XA_HEADS):
        sl = slice(hd * XA_HD, (hd + 1) * XA_HD)
        s = lax.dot_general(qb[:, sl], k_ref[0, :, sl], (((1,), (1,)), ((), ())),
                            preferred_element_type=F32) * (XA_HD ** -0.5)
        e = jnp.exp(s - jnp.max(s, axis=-1, keepdims=True))
        pr = e / jnp.sum(e, axis=-1, keepdims=True)
        heads.append(_dot(pr.astype(BF16), v_ref[0, :, sl]))
    o_xa = jnp.concatenate(heads, axis=1) * _silu(proj(6))

    merged = None
    for n, o in enumerate((o_pool, o_sgu, o_xa)):
        t = jax.nn.sigmoid(proj(7 + n)) * _dot(o.astype(BF16), wdown_ref[n])
        merged = t if merged is None else merged + t
    xn = x + _dot(merged.astype(BF16), wout_ref[...])
    y_ref[0] = _rmsnorm(xn, gf_ref[...])


def _prompt_layer(x, kb, vb, gin, win_b, poolw_b, pscale, sgug, sguw, sgub_t, wdown_b, wout_b, gf):
    nb, seq, _ = x.shape
    ts = SEQ_TILE
    in_cols = win_b.shape[1]
    return pl.pallas_call(
        _prompt_kernel,
        grid=(nb, seq // ts),
        in_specs=[pl.BlockSpec((1, ts, D_MODEL), lambda b, i: (b, i, 0)),
                  pl.BlockSpec((1, N_MEM, BRANCH_W), lambda b, i: (b, 0, 0)),
                  pl.BlockSpec((1, N_MEM, BRANCH_W), lambda b, i: (b, 0, 0)),
                  _resident((1, D_MODEL)),
                  _resident((D_MODEL, in_cols)),
                  _resident((len(POOL_WINDOWS), POOL_GW, POOL_GW)),
                  _resident((1, BRANCH_W)),
                  _resident((1, BRANCH_W)),
                  _resident((SGU_GROUPS, SGU_CHUNK, SGU_CHUNK)),
                  _resident((SGU_CHUNK, SGU_GROUPS)),
                  _resident((N_BRANCH, BRANCH_W, D_MODEL)),
                  _resident((D_MODEL, D_MODEL)),
                  _resident((1, D_MODEL))],
        out_specs=[pl.BlockSpec((1, ts, D_MODEL), lambda b, i: (b, i, 0)),
                   pl.BlockSpec((1, HIST_ROWS, BRANCH_W), lambda b, i: (b, 0, 0))],
        out_shape=[jax.ShapeDtypeStruct((nb, seq, D_MODEL), F32),
                   jax.ShapeDtypeStruct((nb, HIST_ROWS, BRANCH_W), F32)],
        scratch_shapes=[pltpu.VMEM((HIST_ROWS + ts, BRANCH_W), F32)],
        compiler_params=pltpu.CompilerParams(dimension_semantics=("arbitrary", "arbitrary"),
                                             vmem_limit_bytes=V7X_VMEM_LIMIT_BYTES),
        name="prompt_layer",
    )(x, kb, vb, gin, win_b, poolw_b, pscale, sgug, sguw, sgub_t, wdown_b, wout_b, gf)


def _sample_pre_kernel(x_ref, hist_ref, gin_ref, win_ref, poolw_ref, pscale_ref, sgug_ref,
                       sgw0_ref, sgb0_ref,
                       newhist_ref, vn_ref, opool_ref, osgu_ref, q_ref, az_ref, gates_ref):
    hb = _rmsnorm(x_ref[...], gin_ref[...]).astype(BF16)

    def proj(c, width=BRANCH_W):
        return _dot(hb, win_ref[:, c * BRANCH_W:c * BRANCH_W + width])

    p_in = proj(0)
    newhist_ref[0:POOL_HIST - 1] = hist_ref[1:POOL_HIST]
    newhist_ref[POOL_HIST - 1] = p_in
    mixed = []
    for g, w in enumerate(POOL_WINDOWS):
        sl = slice(g * POOL_GW, (g + 1) * POOL_GW)
        cur = p_in[:, sl]
        win = cur
        for j in range(1, w):
            win = win + hist_ref[POOL_HIST - j, :, sl]
        cnt = float(min(w, PAST_LEN + 1))
        d = win / cnt - cur
        mixed.append(_dot(d.astype(BF16), poolw_ref[g]))
    opool_ref[...] = jnp.concatenate(mixed, axis=1) * pscale_ref[...] * _silu(proj(1))

    vn = _rmsnorm(proj(3), sgug_ref[...])
    vn_ref[...] = vn
    osgu_ref[...] = proj(2) * (vn * sgw0_ref[...] + sgb0_ref[...]) * _silu(proj(4))
    q_ref[...] = proj(5)
    az_ref[...] = proj(6)
    gates_ref[...] = proj(7, N_BRANCH * D_MODEL)


def _sample_pre(xs, hist, gin, win_b, poolw_b, pscale, sgug, sgw0, sgb0):
    n = xs.shape[0]
    row = jax.ShapeDtypeStruct((n, BRANCH_W), F32)
    return pl.pallas_call(
        _sample_pre_kernel,
        out_shape=[jax.ShapeDtypeStruct(hist.shape, F32), row, row, row, row, row,
                   jax.ShapeDtypeStruct((n, N_BRANCH * D_MODEL), F32)],
        compiler_params=pltpu.CompilerParams(vmem_limit_bytes=V7X_VMEM_LIMIT_BYTES),
        name="sample_pre",
    )(xs, hist, gin, win_b, poolw_b, pscale, sgug, sgw0, sgb0)


def _sample_attn_kernel(q_ref, k_ref, v_ref, o_ref):
    for r in range(q_ref.shape[0]):
        s = jnp.sum(k_ref[r] * q_ref[r], axis=-1, keepdims=True) * (XA_HD ** -0.5)
        e = jnp.exp(s - jnp.max(s, axis=0, keepdims=True))
        pr = e / jnp.sum(e, axis=0, keepdims=True)
        o_ref[r] = jnp.sum(pr * v_ref[r], axis=0)


def _sample_attn(q, k, v):
    n = q.shape[0]
    rb = ATTN_REQ_BLOCK
    kv_spec = lambda: pl.BlockSpec((rb, N_MEM, XA_HEADS, XA_HD), lambda i: (i, 0, 0, 0))
    return pl.pallas_call(
        _sample_attn_kernel,
        grid=(n // rb,),
        in_specs=[pl.BlockSpec((rb, XA_HEADS, XA_HD), lambda i: (i, 0, 0)), kv_spec(), kv_spec()],
        out_specs=pl.BlockSpec((rb, XA_HEADS, XA_HD), lambda i: (i, 0, 0)),
        out_shape=jax.ShapeDtypeStruct((n, XA_HEADS, XA_HD), F32),
        compiler_params=pltpu.CompilerParams(dimension_semantics=("arbitrary",),
                                             vmem_limit_bytes=V7X_VMEM_LIMIT_BYTES),
        name="sample_attn",
    )(q, k, v)


def _sample_post_kernel(x_ref, opool_ref, osgu_ref, attn_ref, az_ref, gates_ref, wdown_ref, wout_ref,
                        gf_ref, y_ref):
    o_xa = attn_ref[...] * _silu(az_ref[...])
    merged = None
    for n, o in enumerate((opool_ref[...], osgu_ref[...], o_xa)):
        gate = jax.nn.sigmoid(gates_ref[:, n * D_MODEL:(n + 1) * D_MODEL])
        t = gate * _dot(o.astype(BF16), wdown_ref[n])
        merged = t if merged is None else merged + t
    xn = x_ref[...] + _dot(merged.astype(BF16), wout_ref[...])
    y_ref[...] = _rmsnorm(xn, gf_ref[...])


def _sample_post(xs, o_pool, o_sgu, attn, a_z, gates, wdown_b, wout_b, gf):
    return pl.pallas_call(
        _sample_post_kernel,
        out_shape=jax.ShapeDtypeStruct(xs.shape, F32),
        compiler_params=pltpu.CompilerParams(vmem_limit_bytes=V7X_VMEM_LIMIT_BYTES),
        name="sample_post",
    )(xs, o_pool, o_sgu, attn, a_z, gates, wdown_b, wout_b, gf)


def kernel(x_prompt, x_sample, state_pool, cache_mem_k, cache_mem_v, mem_prompt, norm_in_g, w_in,
           pool_w, pool_scale, sgu_norm_g, sgu_w, sgu_b, mem_norm_g, w_kv, w_down, w_out, norm_f_g):
    depth = w_in.shape[0]
    assert depth == 1, "single-layer step"
    nb, seq, _ = x_prompt.shape
    ns, dec_seq, _ = x_sample.shape
    assert dec_seq == 1 and seq % SEQ_TILE == 0 and seq >= HIST_ROWS and ns % ATTN_REQ_BLOCK == 0

    row = lambda a: a.reshape(1, -1)
    gin, pscale, sgug, gmem, gf = (row(norm_in_g[0]), row(pool_scale[0]), row(sgu_norm_g[0]),
                                   row(mem_norm_g[0]), row(norm_f_g))
    win_b = w_in[0].astype(BF16)
    poolw_b = pool_w[0].astype(BF16)
    wkv_b = w_kv[0].astype(BF16)
    wdown_b = w_down[0].astype(BF16)
    wout_b = w_out[0].astype(BF16)
    sguw = sgu_w[0]
    sgub_t = sgu_b[0].T
    sgw0 = jnp.repeat(sgu_w[0, :, 0, 0], SGU_GW).reshape(1, BRANCH_W)
    sgb0 = jnp.repeat(sgu_b[0, :, 0], SGU_GW).reshape(1, BRANCH_W)

    k_p, v_p, kb, vb = _mem_kv(mem_prompt, gmem, wkv_b)
    y_prompt, hist_p = _prompt_layer(x_prompt, kb, vb, gin, win_b, poolw_b, pscale, sgug, sguw, sgub_t,
                                     wdown_b, wout_b, gf)

    xs = x_sample.reshape(ns, D_MODEL)
    hist_s = jnp.transpose(state_pool[0], (1, 0, 2))
    new_hist_s, vn_s, o_pool_s, o_sgu_s, q_s, az_s, gates_s = _sample_pre(
        xs, hist_s, gin, win_b, poolw_b, pscale, sgug, sgw0, sgb0)
    attn_s = _sample_attn(q_s.reshape(ns, XA_HEADS, XA_HD), cache_mem_k[0], cache_mem_v[0])
    y_sample = _sample_post(xs, o_pool_s, o_sgu_s, attn_s.reshape(ns, BRANCH_W), az_s, gates_s,
                            wdown_b, wout_b, gf)

    new_pool_p = hist_p[None, :, HIST_ROWS - POOL_HIST:, :]
    new_pool_s = jnp.transpose(new_hist_s, (1, 0, 2))[None]
    kv_shape = (1, nb, N_MEM, XA_HEADS, XA_HD)
    return (y_prompt, y_sample.reshape(ns, 1, D_MODEL), new_pool_p, new_pool_s,
            k_p.reshape(kv_shape), v_p.reshape(kv_shape), vn_s.reshape(1, ns, 1, BRANCH_W))
```

```python
import jax
import jax.numpy as jnp
from jax import lax
from jax.experimental import pallas as pl
from jax.experimental.pallas import tpu as pltpu

D_MODEL = 1024
BRANCH_W = 1024
N_BRANCH = 3
POOL_WINDOWS = (2, 4, 8, 16)
POOL_GW = BRANCH_W // len(POOL_WINDOWS)
POOL_HIST = max(POOL_WINDOWS) - 1
HIST_ROWS = POOL_HIST + 1
SGU_CHUNK = 128
SGU_GROUPS = 4
SGU_GW = BRANCH_W // SGU_GROUPS
N_MEM = 256
XA_HEADS = 4
XA_HD = BRANCH_W // XA_HEADS
EPS = 1e-6
PAST_LEN = 16384

SEQ_TILE = 256
LANES = 128
XA_LANE_TILES = XA_HD // LANES
KV_ROWS = XA_HEADS * XA_LANE_TILES
LOG2E = 1.4426950408889634
V7X_VMEM_LIMIT_BYTES = 60 * 1024 * 1024

F32 = jnp.float32
BF16 = jnp.bfloat16


def _rmsnorm(x, g):
    return x * lax.rsqrt(jnp.mean(x * x, axis=-1, keepdims=True) + EPS) * g


def _silu(z):
    return z * jax.nn.sigmoid(z)


def _dot(a, b):
    return jnp.dot(a, b, preferred_element_type=F32)


def _resident(shape):
    zeros = (0,) * len(shape)
    return pl.BlockSpec(shape, lambda *_: zeros, pipeline_mode=pl.Buffered(1))


def _to_kv_rows(a):
    lead = a.shape[:-1]
    a = a.reshape(*lead, XA_HEADS, XA_LANE_TILES, LANES)
    return jnp.swapaxes(a, -3, -2).reshape(*lead, KV_ROWS, LANES)


def _from_kv_rows(a):
    lead = a.shape[:-2]
    a = a.reshape(*lead, XA_LANE_TILES, XA_HEADS, LANES)
    return jnp.swapaxes(a, -3, -2).reshape(*lead, XA_HEADS, XA_HD)


def _mem_kv_kernel(mem_ref, g_ref, wkv_ref, k_ref, v_ref, kb_ref, vb_ref):
    m = _rmsnorm(mem_ref[0], g_ref[...]).astype(BF16)
    kv = _dot(m, wkv_ref[...])
    k = kv[:, :BRANCH_W]
    v = kv[:, BRANCH_W:]
    kb_ref[0] = k.astype(BF16)
    vb_ref[0] = v.astype(BF16)
    for h in range(XA_HEADS):
        for lt in range(XA_LANE_TILES):
            cols = slice(h * XA_HD + lt * LANES, h * XA_HD + (lt + 1) * LANES)
            rows = pl.ds(lt * XA_HEADS + h, N_MEM, stride=KV_ROWS)
            k_ref[0, rows, :] = k[:, cols]
            v_ref[0, rows, :] = v[:, cols]


def _mem_kv(mem, g, wkv_b):
    nb = mem.shape[0]
    rows_blk = lambda: pl.BlockSpec((1, N_MEM * KV_ROWS, LANES), lambda b: (b, 0, 0))
    flat_blk = lambda: pl.BlockSpec((1, N_MEM, BRANCH_W), lambda b: (b, 0, 0))
    return pl.pallas_call(
        _mem_kv_kernel,
        grid=(nb,),
        in_specs=[pl.BlockSpec((1, N_MEM, D_MODEL), lambda b: (b, 0, 0)),
                  _resident((1, D_MODEL)),
                  _resident((D_MODEL, 2 * BRANCH_W))],
        out_specs=[rows_blk(), rows_blk(), flat_blk(), flat_blk()],
        out_shape=[jax.ShapeDtypeStruct((nb, N_MEM * KV_ROWS, LANES), F32),
                   jax.ShapeDtypeStruct((nb, N_MEM * KV_ROWS, LANES), F32),
                   jax.ShapeDtypeStruct((nb, N_MEM, BRANCH_W), BF16),
                   jax.ShapeDtypeStruct((nb, N_MEM, BRANCH_W), BF16)],
        compiler_params=pltpu.CompilerParams(dimension_semantics=("arbitrary",),
                                             vmem_limit_bytes=V7X_VMEM_LIMIT_BYTES),
        name="mem_kv",
    )(mem, g, wkv_b)


def _sample_attention(q_ref, k_ref, v_ref, o_ref):
    for r in range(q_ref.shape[0]):
        q = q_ref[r] * (XA_HD ** -0.5 * LOG2E)
        part = k_ref[r] * q
        part = part + pltpu.roll(part, XA_HEADS, axis=1)
        s = jnp.sum(part, axis=-1, keepdims=True)
        e = jnp.exp2(s - jnp.max(s, axis=0, keepdims=True))
        o_ref[r] = jnp.sum(e * v_ref[r], axis=0) / jnp.sum(e, axis=0)


def _prompt_kernel(x_ref, k_ref, v_ref, gin_ref, win_ref, poolw_ref, pscale_ref, sgug_ref,
                   sguw_ref, sgub_ref, wdown_ref, wout_ref, gf_ref, sq_ref, sk_ref, sv_ref,
                   y_ref, hist_ref, so_ref, ext_ref):
    i = pl.program_id(1)
    ts = x_ref.shape[1]

    _sample_attention(sq_ref, sk_ref, sv_ref, so_ref)

    x = x_ref[0]
    hb = _rmsnorm(x, gin_ref[...]).astype(BF16)

    def proj(c):
        return _dot(hb, win_ref[:, c * BRANCH_W:(c + 1) * BRANCH_W])

    p_in = proj(0)

    @pl.when(i == 0)
    def _():
        ext_ref[0:HIST_ROWS, :] = jnp.zeros((HIST_ROWS, BRANCH_W), F32)

    ext_ref[HIST_ROWS:HIST_ROWS + ts, :] = p_in
    pos = i * ts + lax.broadcasted_iota(jnp.int32, (ts, 1), 0)
    mixed = []
    for g, w in enumerate(POOL_WINDOWS):
        sl = slice(g * POOL_GW, (g + 1) * POOL_GW)
        cur = p_in[:, sl]
        win = cur
        for j in range(1, w):
            win = win + ext_ref[HIST_ROWS - j:HIST_ROWS - j + ts, sl]
        cnt = jnp.minimum(w, pos + 1).astype(F32)
        d = win / cnt - cur
        mixed.append(_dot(d.astype(BF16), poolw_ref[g]))
    o_pool = jnp.concatenate(mixed, axis=1) * pscale_ref[...] * _silu(proj(1))
    ext_ref[0:HIST_ROWS, :] = p_in[ts - HIST_ROWS:, :]

    @pl.when(i == pl.num_programs(1) - 1)
    def _():
        hist_ref[0] = p_in[ts - HIST_ROWS:, :]

    u = proj(2)
    vnb = _rmsnorm(proj(3), sgug_ref[...]).astype(BF16)
    tril = (lax.broadcasted_iota(jnp.int32, (SGU_CHUNK, SGU_CHUNK), 0)
            >= lax.broadcasted_iota(jnp.int32, (SGU_CHUNK, SGU_CHUNK), 1))
    ws = [jnp.where(tril, sguw_ref[g], 0.0).astype(BF16) for g in range(SGU_GROUPS)]
    rows = []
    for c in range(ts // SGU_CHUNK):
        rs = slice(c * SGU_CHUNK, (c + 1) * SGU_CHUNK)
        cols = [_dot(ws[g], vnb[rs, g * SGU_GW:(g + 1) * SGU_GW]) + sgub_ref[:, g:g + 1]
                for g in range(SGU_GROUPS)]
        rows.append(jnp.concatenate(cols, axis=1))
    o_sgu = u * jnp.concatenate(rows, axis=0) * _silu(proj(4))

    qb = proj(5).astype(BF16)
    heads = []
    for hd in range(XA_HEADS):
        sl = slice(hd * XA_HD, (hd + 1) * XA_HD)
        s = lax.dot_general(qb[:, sl], k_ref[0, :, sl], (((1,), (1,)), ((), ())),
                            preferred_element_type=F32) * (XA_HD ** -0.5)
        e = jnp.exp(s - jnp.max(s, axis=-1, keepdims=True))
        pr = e / jnp.sum(e, axis=-1, keepdims=True)
        heads.append(_dot(pr.astype(BF16), v_ref[0, :, sl]))
    o_xa = jnp.concatenate(heads, axis=1) * _silu(proj(6))

    merged = None
    for n, o in enumerate((o_pool, o_sgu, o_xa)):
        t = jax.nn.sigmoid(proj(7 + n)) * _dot(o.astype(BF16), wdown_ref[n])
        merged = t if merged is None else merged + t
    xn = x + _dot(merged.astype(BF16), wout_ref[...])
    y_ref[0] = _rmsnorm(xn, gf_ref[...])


def _prompt_layer(x, kb, vb, gin, win_b, poolw_b, pscale, sgug, sguw, sgub_t, wdown_b, wout_b, gf,
                  sq, sk, sv):
    nb, seq, _ = x.shape
    ts = SEQ_TILE
    n_tiles = seq // ts
    ns = sq.shape[0]
    rb = ns // (nb * n_tiles)
    assert rb * nb * n_tiles == ns
    in_cols = win_b.shape[1]
    step = lambda b, i: b * n_tiles + i
    return pl.pallas_call(
        _prompt_kernel,
        grid=(nb, n_tiles),
        in_specs=[pl.BlockSpec((1, ts, D_MODEL), lambda b, i: (b, i, 0)),
                  pl.BlockSpec((1, N_MEM, BRANCH_W), lambda b, i: (b, 0, 0)),
                  pl.BlockSpec((1, N_MEM, BRANCH_W), lambda b, i: (b, 0, 0)),
                  _resident((1, D_MODEL)),
                  _resident((D_MODEL, in_cols)),
                  _resident((len(POOL_WINDOWS), POOL_GW, POOL_GW)),
                  _resident((1, BRANCH_W)),
                  _resident((1, BRANCH_W)),
                  _resident((SGU_GROUPS, SGU_CHUNK, SGU_CHUNK)),
                  _resident((SGU_CHUNK, SGU_GROUPS)),
                  _resident((N_BRANCH, BRANCH_W, D_MODEL)),
                  _resident((D_MODEL, D_MODEL)),
                  _resident((1, D_MODEL)),
                  pl.BlockSpec((rb, KV_ROWS, LANES), lambda b, i: (step(b, i), 0, 0)),
                  pl.BlockSpec((rb, N_MEM, KV_ROWS, LANES), lambda b, i: (step(b, i), 0, 0, 0)),
                  pl.BlockSpec((rb, N_MEM, KV_ROWS, LANES), lambda b, i: (step(b, i), 0, 0, 0))],
        out_specs=[pl.BlockSpec((1, ts, D_MODEL), lambda b, i: (b, i, 0)),
                   pl.BlockSpec((1, HIST_ROWS, BRANCH_W), lambda b, i: (b, 0, 0)),
                   pl.BlockSpec((rb, KV_ROWS, LANES), lambda b, i: (step(b, i), 0, 0))],
        out_shape=[jax.ShapeDtypeStruct((nb, seq, D_MODEL), F32),
                   jax.ShapeDtypeStruct((nb, HIST_ROWS, BRANCH_W), F32),
                   jax.ShapeDtypeStruct((ns, KV_ROWS, LANES), F32)],
        scratch_shapes=[pltpu.VMEM((HIST_ROWS + ts, BRANCH_W), F32)],
        compiler_params=pltpu.CompilerParams(dimension_semantics=("arbitrary", "arbitrary"),
                                             vmem_limit_bytes=V7X_VMEM_LIMIT_BYTES),
        name="prompt_layer",
    )(x, kb, vb, gin, win_b, poolw_b, pscale, sgug, sguw, sgub_t, wdown_b, wout_b, gf, sq, sk, sv)


def _sample_pre_kernel(x_ref, hist_ref, gin_ref, win_ref, poolw_ref, pscale_ref, sgug_ref,
                       sgw0_ref, sgb0_ref,
                       newhist_ref, vn_ref, opool_ref, osgu_ref, q_ref, az_ref, gates_ref):
    hb = _rmsnorm(x_ref[...], gin_ref[...]).astype(BF16)

    def proj(c, width=BRANCH_W):
        return _dot(hb, win_ref[:, c * BRANCH_W:c * BRANCH_W + width])

    p_in = proj(0)
    newhist_ref[0:POOL_HIST - 1] = hist_ref[1:POOL_HIST]
    newhist_ref[POOL_HIST - 1] = p_in
    mixed = []
    for g, w in enumerate(POOL_WINDOWS):
        sl = slice(g * POOL_GW, (g + 1) * POOL_GW)
        cur = p_in[:, sl]
        win = cur
        for j in range(1, w):
            win = win + hist_ref[POOL_HIST - j, :, sl]
        cnt = float(min(w, PAST_LEN + 1))
        d = win / cnt - cur
        mixed.append(_dot(d.astype(BF16), poolw_ref[g]))
    opool_ref[...] = jnp.concatenate(mixed, axis=1) * pscale_ref[...] * _silu(proj(1))

    vn = _rmsnorm(proj(3), sgug_ref[...])
    vn_ref[...] = vn
    osgu_ref[...] = proj(2) * (vn * sgw0_ref[...] + sgb0_ref[...]) * _silu(proj(4))
    q_ref[...] = proj(5)
    az_ref[...] = proj(6)
    gates_ref[...] = proj(7, N_BRANCH * D_MODEL)


def _sample_pre(xs, hist, gin, win_b, poolw_b, pscale, sgug, sgw0, sgb0):
    n = xs.shape[0]
    row = jax.ShapeDtypeStruct((n, BRANCH_W), F32)
    return pl.pallas_call(
        _sample_pre_kernel,
        out_shape=[jax.ShapeDtypeStruct(hist.shape, F32), row, row, row, row, row,
                   jax.ShapeDtypeStruct((n, N_BRANCH * D_MODEL), F32)],
        compiler_params=pltpu.CompilerParams(vmem_limit_bytes=V7X_VMEM_LIMIT_BYTES),
        name="sample_pre",
    )(xs, hist, gin, win_b, poolw_b, pscale, sgug, sgw0, sgb0)


def _sample_post_kernel(x_ref, opool_ref, osgu_ref, attn_ref, az_ref, gates_ref, wdown_ref, wout_ref,
                        gf_ref, y_ref):
    o_xa = attn_ref[...] * _silu(az_ref[...])
    merged = None
    for n, o in enumerate((opool_ref[...], osgu_ref[...], o_xa)):
        gate = jax.nn.sigmoid(gates_ref[:, n * D_MODEL:(n + 1) * D_MODEL])
        t = gate * _dot(o.astype(BF16), wdown_ref[n])
        merged = t if merged is None else merged + t
    xn = x_ref[...] + _dot(merged.astype(BF16), wout_ref[...])
    y_ref[...] = _rmsnorm(xn, gf_ref[...])


def _sample_post(xs, o_pool, o_sgu, attn, a_z, gates, wdown_b, wout_b, gf):
    return pl.pallas_call(
        _sample_post_kernel,
        out_shape=jax.ShapeDtypeStruct(xs.shape, F32),
        compiler_params=pltpu.CompilerParams(vmem_limit_bytes=V7X_VMEM_LIMIT_BYTES),
        name="sample_post",
    )(xs, o_pool, o_sgu, attn, a_z, gates, wdown_b, wout_b, gf)


def kernel(x_prompt, x_sample, state_pool, cache_mem_k, cache_mem_v, mem_prompt, norm_in_g, w_in,
           pool_w, pool_scale, sgu_norm_g, sgu_w, sgu_b, mem_norm_g, w_kv, w_down, w_out, norm_f_g):
    depth = w_in.shape[0]
    assert depth == 1, "single-layer step"
    nb, seq, _ = x_prompt.shape
    ns, dec_seq, _ = x_sample.shape
    assert dec_seq == 1 and seq % SEQ_TILE == 0 and seq >= HIST_ROWS

    row = lambda a: a.reshape(1, -1)
    gin, pscale, sgug, gmem, gf = (row(norm_in_g[0]), row(pool_scale[0]), row(sgu_norm_g[0]),
                                   row(mem_norm_g[0]), row(norm_f_g))
    win_b = w_in[0].astype(BF16)
    poolw_b = pool_w[0].astype(BF16)
    wkv_b = w_kv[0].astype(BF16)
    wdown_b = w_down[0].astype(BF16)
    wout_b = w_out[0].astype(BF16)
    sguw = sgu_w[0]
    sgub_t = sgu_b[0].T
    sgw0 = jnp.repeat(sgu_w[0, :, 0, 0], SGU_GW).reshape(1, BRANCH_W)
    sgb0 = jnp.repeat(sgu_b[0, :, 0], SGU_GW).reshape(1, BRANCH_W)

    k_rows, v_rows, kb, vb = _mem_kv(mem_prompt, gmem, wkv_b)

    xs = x_sample.reshape(ns, D_MODEL)
    hist_s = jnp.transpose(state_pool[0], (1, 0, 2))
    new_hist_s, vn_s, o_pool_s, o_sgu_s, q_s, az_s, gates_s = _sample_pre(
        xs, hist_s, gin, win_b, poolw_b, pscale, sgug, sgw0, sgb0)

    y_prompt, hist_p, attn_rows = _prompt_layer(
        x_prompt, kb, vb, gin, win_b, poolw_b, pscale, sgug, sguw, sgub_t, wdown_b, wout_b, gf,
        _to_kv_rows(q_s), _to_kv_rows(cache_mem_k[0].reshape(ns, N_MEM, BRANCH_W)),
        _to_kv_rows(cache_mem_v[0].reshape(ns, N_MEM, BRANCH_W)))

    attn_s = _from_kv_rows(attn_rows).reshape(ns, BRANCH_W)
    y_sample = _sample_post(xs, o_pool_s, o_sgu_s, attn_s, az_s, gates_s, wdown_b, wout_b, gf)

    new_pool_p = hist_p[None, :, HIST_ROWS - POOL_HIST:, :]
    new_pool_s = jnp.transpose(new_hist_s, (1, 0, 2))[None]
    kv_out = lambda a: _from_kv_rows(a.reshape(nb, N_MEM, KV_ROWS, LANES))[None]
    return (y_prompt, y_sample.reshape(ns, 1, D_MODEL), new_pool_p, new_pool_s,
            kv_out(k_rows), kv_out(v_rows), vn_s.reshape(1, ns, 1, BRANCH_W))
```

```python
import jax
import jax.numpy as jnp
from jax import lax
from jax.experimental import pallas as pl
from jax.experimental.pallas import tpu as pltpu

D_MODEL = 1024
BRANCH_W = 1024
N_BRANCH = 3
POOL_WINDOWS = (2, 4, 8, 16)
POOL_GW = BRANCH_W // len(POOL_WINDOWS)
POOL_HIST = max(POOL_WINDOWS) - 1
HIST_ROWS = POOL_HIST + 1
SGU_CHUNK = 128
SGU_GROUPS = 4
SGU_GW = BRANCH_W // SGU_GROUPS
N_MEM = 256
XA_HEADS = 4
XA_HD = BRANCH_W // XA_HEADS
EPS = 1e-6
PAST_LEN = 16384

SEQ_TILE = 256
LANES = 128
XA_LANE_TILES = XA_HD // LANES
KV_ROWS = XA_HEADS * XA_LANE_TILES
LOG2E = 1.4426950408889634
V7X_VMEM_LIMIT_BYTES = 60 * 1024 * 1024

F32 = jnp.float32
BF16 = jnp.bfloat16


def _rmsnorm(x, g):
    return x * lax.rsqrt(jnp.mean(x * x, axis=-1, keepdims=True) + EPS) * g


def _silu(z):
    return z * jax.nn.sigmoid(z)


def _dot(a, b):
    return jnp.dot(a, b, preferred_element_type=F32)


def _resident(shape):
    zeros = (0,) * len(shape)
    return pl.BlockSpec(shape, lambda *_: zeros, pipeline_mode=pl.Buffered(1))


def _to_kv_rows(a):
    lead = a.shape[:-1]
    a = a.reshape(*lead, XA_HEADS, XA_LANE_TILES, LANES)
    return jnp.swapaxes(a, -3, -2).reshape(*lead, KV_ROWS, LANES)


def _from_kv_rows(a):
    lead = a.shape[:-2]
    a = a.reshape(*lead, XA_LANE_TILES, XA_HEADS, LANES)
    return jnp.swapaxes(a, -3, -2).reshape(*lead, XA_HEADS, XA_HD)


def _mem_kv_kernel(mem_ref, g_ref, wkv_ref, k_ref, v_ref, kb_ref, vb_ref):
    m = _rmsnorm(mem_ref[0], g_ref[...]).astype(BF16)
    kv = _dot(m, wkv_ref[...])
    k = kv[:, :BRANCH_W]
    v = kv[:, BRANCH_W:]
    kb_ref[0] = k.astype(BF16)
    vb_ref[0] = v.astype(BF16)
    for h in range(XA_HEADS):
        for lt in range(XA_LANE_TILES):
            cols = slice(h * XA_HD + lt * LANES, h * XA_HD + (lt + 1) * LANES)
            rows = pl.ds(lt * XA_HEADS + h, N_MEM, stride=KV_ROWS)
            k_ref[0, rows, :] = k[:, cols]
            v_ref[0, rows, :] = v[:, cols]


def _mem_kv(mem, g, wkv_b):
    nb = mem.shape[0]
    rows_blk = lambda: pl.BlockSpec((1, N_MEM * KV_ROWS, LANES), lambda b: (b, 0, 0))
    flat_blk = lambda: pl.BlockSpec((1, N_MEM, BRANCH_W), lambda b: (b, 0, 0))
    return pl.pallas_call(
        _mem_kv_kernel,
        grid=(nb,),
        in_specs=[pl.BlockSpec((1, N_MEM, D_MODEL), lambda b: (b, 0, 0)),
                  _resident((1, D_MODEL)),
                  _resident((D_MODEL, 2 * BRANCH_W))],
        out_specs=[rows_blk(), rows_blk(), flat_blk(), flat_blk()],
        out_shape=[jax.ShapeDtypeStruct((nb, N_MEM * KV_ROWS, LANES), F32),
                   jax.ShapeDtypeStruct((nb, N_MEM * KV_ROWS, LANES), F32),
                   jax.ShapeDtypeStruct((nb, N_MEM, BRANCH_W), BF16),
                   jax.ShapeDtypeStruct((nb, N_MEM, BRANCH_W), BF16)],
        compiler_params=pltpu.CompilerParams(dimension_semantics=("arbitrary",),
                                             vmem_limit_bytes=V7X_VMEM_LIMIT_BYTES),
        name="mem_kv",
    )(mem, g, wkv_b)


def _sample_attention(q_ref, k_ref, v_ref, o_ref):
    for r in range(q_ref.shape[0]):
        q = q_ref[r] * (XA_HD ** -0.5 * LOG2E)
        part = k_ref[r] * q
        part = part + pltpu.roll(part, XA_HEADS, axis=1)
        s = jnp.sum(part, axis=-1, keepdims=True)
        e = jnp.exp2(s - jnp.max(s, axis=0, keepdims=True))
        o_ref[r] = jnp.sum(e * v_ref[r], axis=0) / jnp.sum(e, axis=0)


def _prompt_kernel(x_ref, k_ref, v_ref, gin_ref, win_ref, poolw_ref, pscale_ref, sgug_ref,
                   sguw_ref, sgub_ref, wdown_ref, wout_ref, gf_ref, sq_ref, sk_ref, sv_ref,
                   y_ref, hist_ref, so_ref, ext_ref):
    i = pl.program_id(1)
    ts = x_ref.shape[1]

    @pl.when(i == 0)
    def _():
        ext_ref[0:HIST_ROWS, :] = jnp.zeros((HIST_ROWS, BRANCH_W), F32)

    _sample_attention(sq_ref, sk_ref, sv_ref, so_ref)

    x = x_ref[0]
    hb = _rmsnorm(x, gin_ref[...]).astype(BF16)

    def proj(c):
        return _dot(hb, win_ref[:, c * BRANCH_W:(c + 1) * BRANCH_W])


    p_in = proj(0)
    p_z = proj(1)
    ext_ref[HIST_ROWS:HIST_ROWS + ts, :] = p_in
    pos = i * ts + lax.broadcasted_iota(jnp.int32, (ts, 1), 0)
    pooled = []
    for g, w in enumerate(POOL_WINDOWS):
        sl = slice(g * POOL_GW, (g + 1) * POOL_GW)
        win = ext_ref[:, sl]
        span = 1
        while span < w:
            win = win + pltpu.roll(win, span, axis=0)
            span *= 2
        inv_cnt = 1.0 / jnp.minimum(w, pos + 1).astype(F32)
        pooled.append((win[HIST_ROWS:, :] * inv_cnt - p_in[:, sl]).astype(BF16))
    ext_ref[0:HIST_ROWS, :] = p_in[ts - HIST_ROWS:, :]
    hist_ref[0] = p_in[ts - HIST_ROWS:, :]

    v = proj(3)
    u = proj(2)
    s_z = proj(4)
    mixed = [_dot(pooled[g], poolw_ref[g]) for g in range(len(POOL_WINDOWS))]
    o_pool = (jnp.concatenate(mixed, axis=1) * pscale_ref[...] * _silu(p_z)).astype(BF16)

    vnb = _rmsnorm(v, sgug_ref[...]).astype(BF16)
    tril = (lax.broadcasted_iota(jnp.int32, (SGU_CHUNK, SGU_CHUNK), 0)
            >= lax.broadcasted_iota(jnp.int32, (SGU_CHUNK, SGU_CHUNK), 1))
    ws = [jnp.where(tril, sguw_ref[g], 0.0).astype(BF16) for g in range(SGU_GROUPS)]
    q = proj(5)
    a_z = proj(6)
    rows = []
    for c in range(ts // SGU_CHUNK):
        rs = slice(c * SGU_CHUNK, (c + 1) * SGU_CHUNK)
        cols = [_dot(ws[g], vnb[rs, g * SGU_GW:(g + 1) * SGU_GW]) + sgub_ref[:, g:g + 1]
                for g in range(SGU_GROUPS)]
        rows.append(jnp.concatenate(cols, axis=1))
    o_sgu = (u * jnp.concatenate(rows, axis=0) * _silu(s_z)).astype(BF16)

    qb = q.astype(BF16)
    scores = [lax.dot_general(qb[:, hd * XA_HD:(hd + 1) * XA_HD], k_ref[0, :, hd * XA_HD:(hd + 1) * XA_HD],
                              (((1,), (1,)), ((), ())), preferred_element_type=F32) * (XA_HD ** -0.5)
              for hd in range(XA_HEADS)]
    gates = [proj(7 + n) for n in range(N_BRANCH)]
    heads = []
    for hd in range(XA_HEADS):
        s = scores[hd]
        e = jnp.exp(s - jnp.max(s, axis=-1, keepdims=True))
        pr = e / jnp.sum(e, axis=-1, keepdims=True)
        heads.append(_dot(pr.astype(BF16), v_ref[0, :, hd * XA_HD:(hd + 1) * XA_HD]))
    o_xa = (jnp.concatenate(heads, axis=1) * _silu(a_z)).astype(BF16)

    merged = None
    for n, o in enumerate((o_pool, o_sgu, o_xa)):
        t = jax.nn.sigmoid(gates[n]) * _dot(o, wdown_ref[n])
        merged = t if merged is None else merged + t
    xn = x + _dot(merged.astype(BF16), wout_ref[...])
    y_ref[0] = _rmsnorm(xn, gf_ref[...])


def _prompt_layer(x, kb, vb, gin, win_b, poolw_b, pscale, sgug, sguw, sgub_t, wdown_b, wout_b, gf,
                  sq, sk, sv):
    nb, seq, _ = x.shape
    ts = SEQ_TILE
    n_tiles = seq // ts
    ns = sq.shape[0]
    rb = ns // (nb * n_tiles)
    assert rb * nb * n_tiles == ns
    in_cols = win_b.shape[1]
    step = lambda b, i: b * n_tiles + i
    return pl.pallas_call(
        _prompt_kernel,
        grid=(nb, n_tiles),
        in_specs=[pl.BlockSpec((1, ts, D_MODEL), lambda b, i: (b, i, 0)),
                  pl.BlockSpec((1, N_MEM, BRANCH_W), lambda b, i: (b, 0, 0)),
                  pl.BlockSpec((1, N_MEM, BRANCH_W), lambda b, i: (b, 0, 0)),
                  _resident((1, D_MODEL)),
                  _resident((D_MODEL, in_cols)),
                  _resident((len(POOL_WINDOWS), POOL_GW, POOL_GW)),
                  _resident((1, BRANCH_W)),
                  _resident((1, BRANCH_W)),
                  _resident((SGU_GROUPS, SGU_CHUNK, SGU_CHUNK)),
                  _resident((SGU_CHUNK, SGU_GROUPS)),
                  _resident((N_BRANCH, BRANCH_W, D_MODEL)),
                  _resident((D_MODEL, D_MODEL)),
                  _resident((1, D_MODEL)),
                  pl.BlockSpec((rb, KV_ROWS, LANES), lambda b, i: (step(b, i), 0, 0)),
                  pl.BlockSpec((rb, N_MEM, KV_ROWS, LANES), lambda b, i: (step(b, i), 0, 0, 0)),
                  pl.BlockSpec((rb, N_MEM, KV_ROWS, LANES), lambda b, i: (step(b, i), 0, 0, 0))],
        out_specs=[pl.BlockSpec((1, ts, D_MODEL), lambda b, i: (b, i, 0)),
                   pl.BlockSpec((1, HIST_ROWS, BRANCH_W), lambda b, i: (b, 0, 0)),
                   pl.BlockSpec((rb, KV_ROWS, LANES), lambda b, i: (step(b, i), 0, 0))],
        out_shape=[jax.ShapeDtypeStruct((nb, seq, D_MODEL), F32),
                   jax.ShapeDtypeStruct((nb, HIST_ROWS, BRANCH_W), F32),
                   jax.ShapeDtypeStruct((ns, KV_ROWS, LANES), F32)],
        scratch_shapes=[pltpu.VMEM((HIST_ROWS + ts, BRANCH_W), F32)],
        compiler_params=pltpu.CompilerParams(dimension_semantics=("arbitrary", "arbitrary"),
                                             vmem_limit_bytes=V7X_VMEM_LIMIT_BYTES),
        name="prompt_layer",
    )(x, kb, vb, gin, win_b, poolw_b, pscale, sgug, sguw, sgub_t, wdown_b, wout_b, gf, sq, sk, sv)


def _sample_pre_kernel(x_ref, hist_ref, gin_ref, win_ref, poolw_ref, pscale_ref, sgug_ref,
                       sgw0_ref, sgb0_ref,
                       newhist_ref, vn_ref, opool_ref, osgu_ref, q_ref, az_ref, gates_ref):
    hb = _rmsnorm(x_ref[...], gin_ref[...]).astype(BF16)

    def proj(c, width=BRANCH_W):
        return _dot(hb, win_ref[:, c * BRANCH_W:c * BRANCH_W + width])

    p_in = proj(0)
    newhist_ref[0:POOL_HIST - 1] = hist_ref[1:POOL_HIST]
    newhist_ref[POOL_HIST - 1] = p_in
    mixed = []
    for g, w in enumerate(POOL_WINDOWS):
        sl = slice(g * POOL_GW, (g + 1) * POOL_GW)
        cur = p_in[:, sl]
        win = cur
        for j in range(1, w):
            win = win + hist_ref[POOL_HIST - j, :, sl]
        cnt = float(min(w, PAST_LEN + 1))
        d = win / cnt - cur
        mixed.append(_dot(d.astype(BF16), poolw_ref[g]))
    opool_ref[...] = jnp.concatenate(mixed, axis=1) * pscale_ref[...] * _silu(proj(1))

    vn = _rmsnorm(proj(3), sgug_ref[...])
    vn_ref[...] = vn
    osgu_ref[...] = proj(2) * (vn * sgw0_ref[...] + sgb0_ref[...]) * _silu(proj(4))
    q_ref[...] = proj(5)
    az_ref[...] = proj(6)
    gates_ref[...] = proj(7, N_BRANCH * D_MODEL)


def _sample_pre(xs, hist, gin, win_b, poolw_b, pscale, sgug, sgw0, sgb0):
    n = xs.shape[0]
    row = jax.ShapeDtypeStruct((n, BRANCH_W), F32)
    return pl.pallas_call(
        _sample_pre_kernel,
        out_shape=[jax.ShapeDtypeStruct(hist.shape, F32), row, row, row, row, row,
                   jax.ShapeDtypeStruct((n, N_BRANCH * D_MODEL), F32)],
        compiler_params=pltpu.CompilerParams(vmem_limit_bytes=V7X_VMEM_LIMIT_BYTES),
        name="sample_pre",
    )(xs, hist, gin, win_b, poolw_b, pscale, sgug, sgw0, sgb0)


def _sample_post_kernel(x_ref, opool_ref, osgu_ref, attn_ref, az_ref, gates_ref, wdown_ref, wout_ref,
                        gf_ref, y_ref):
    o_xa = attn_ref[...] * _silu(az_ref[...])
    merged = None
    for n, o in enumerate((opool_ref[...], osgu_ref[...], o_xa)):
        gate = jax.nn.sigmoid(gates_ref[:, n * D_MODEL:(n + 1) * D_MODEL])
        t = gate * _dot(o.astype(BF16), wdown_ref[n])
        merged = t if merged is None else merged + t
    xn = x_ref[...] + _dot(merged.astype(BF16), wout_ref[...])
    y_ref[...] = _rmsnorm(xn, gf_ref[...])


def _sample_post(xs, o_pool, o_sgu, attn, a_z, gates, wdown_b, wout_b, gf):
    return pl.pallas_call(
        _sample_post_kernel,
        out_shape=jax.ShapeDtypeStruct(xs.shape, F32),
        compiler_params=pltpu.CompilerParams(vmem_limit_bytes=V7X_VMEM_LIMIT_BYTES),
        name="sample_post",
    )(xs, o_pool, o_sgu, attn, a_z, gates, wdown_b, wout_b, gf)


def kernel(x_prompt, x_sample, state_pool, cache_mem_k, cache_mem_v, mem_prompt, norm_in_g, w_in,
           pool_w, pool_scale, sgu_norm_g, sgu_w, sgu_b, mem_norm_g, w_kv, w_down, w_out, norm_f_g):
    depth = w_in.shape[0]
    assert depth == 1, "single-layer step"
    nb, seq, _ = x_prompt.shape
    ns, dec_seq, _ = x_sample.shape
    assert dec_seq == 1 and seq % SEQ_TILE == 0 and seq >= HIST_ROWS

    row = lambda a: a.reshape(1, -1)
    gin, pscale, sgug, gmem, gf = (row(norm_in_g[0]), row(pool_scale[0]), row(sgu_norm_g[0]),
                                   row(mem_norm_g[0]), row(norm_f_g))
    win_b = w_in[0].astype(BF16)
    poolw_b = pool_w[0].astype(BF16)
    wkv_b = w_kv[0].astype(BF16)
    wdown_b = w_down[0].astype(BF16)
    wout_b = w_out[0].astype(BF16)
    sguw = sgu_w[0]
    sgub_t = sgu_b[0].T
    sgw0 = jnp.repeat(sgu_w[0, :, 0, 0], SGU_GW).reshape(1, BRANCH_W)
    sgb0 = jnp.repeat(sgu_b[0, :, 0], SGU_GW).reshape(1, BRANCH_W)

    k_rows, v_rows, kb, vb = _mem_kv(mem_prompt, gmem, wkv_b)

    xs = x_sample.reshape(ns, D_MODEL)
    hist_s = jnp.transpose(state_pool[0], (1, 0, 2))
    new_hist_s, vn_s, o_pool_s, o_sgu_s, q_s, az_s, gates_s = _sample_pre(
        xs, hist_s, gin, win_b, poolw_b, pscale, sgug, sgw0, sgb0)

    y_prompt, hist_p, attn_rows = _prompt_layer(
        x_prompt, kb, vb, gin, win_b, poolw_b, pscale, sgug, sguw, sgub_t, wdown_b, wout_b, gf,
        _to_kv_rows(q_s), _to_kv_rows(cache_mem_k[0].reshape(ns, N_MEM, BRANCH_W)),
        _to_kv_rows(cache_mem_v[0].reshape(ns, N_MEM, BRANCH_W)))

    attn_s = _from_kv_rows(attn_rows).reshape(ns, BRANCH_W)
    y_sample = _sample_post(xs, o_pool_s, o_sgu_s, attn_s, az_s, gates_s, wdown_b, wout_b, gf)

    new_pool_p = hist_p[None, :, HIST_ROWS - POOL_HIST:, :]
    new_pool_s = jnp.transpose(new_hist_s, (1, 0, 2))[None]
    kv_out = lambda a: _from_kv_rows(a.reshape(nb, N_MEM, KV_ROWS, LANES))[None]
    return (y_prompt, y_sample.reshape(ns, 1, D_MODEL), new_pool_p, new_pool_s,
            kv_out(k_rows), kv_out(v_rows), vn_s.reshape(1, ns, 1, BRANCH_W))
```

```python
import jax
import jax.numpy as jnp
from jax import lax
from jax.experimental import pallas as pl
from jax.experimental.pallas import tpu as pltpu

D_MODEL = 1024
BRANCH_W = 1024
N_BRANCH = 3
N_PROJ = 7 + N_BRANCH
POOL_WINDOWS = (2, 4, 8, 16)
POOL_GROUPS = len(POOL_WINDOWS)
POOL_GW = BRANCH_W // POOL_GROUPS
POOL_HIST = max(POOL_WINDOWS) - 1
HIST_ROWS = POOL_HIST + 1
SGU_CHUNK = 128
SGU_GROUPS = 4
SGU_GW = BRANCH_W // SGU_GROUPS
N_MEM = 256
XA_HEADS = 4
XA_HD = BRANCH_W // XA_HEADS
EPS = 1e-6
PAST_LEN = 16384

SEQ_TILE = 256
LANES = 128
XA_LANE_TILES = XA_HD // LANES
KV_ROWS = XA_HEADS * XA_LANE_TILES
LOG2E = 1.4426950408889634
W_STAGE_COLS = 512
V7X_VMEM_LIMIT_BYTES = 60 * 1024 * 1024

F32 = jnp.float32
BF16 = jnp.bfloat16

_sigmoid = jax.nn.sigmoid


def _rmsnorm(x, g):
    return x * lax.rsqrt(jnp.mean(x * x, axis=-1, keepdims=True) + EPS) * g


def _silu(z):
    return z * _sigmoid(z)


def _dot(a, b):
    return jnp.dot(a, b, preferred_element_type=F32)


def _resident(shape):
    zeros = (0,) * len(shape)
    return pl.BlockSpec(shape, lambda *_: zeros, pipeline_mode=pl.Buffered(1))


def _to_kv_rows(a):
    lead = a.shape[:-1]
    a = a.reshape(*lead, XA_HEADS, XA_LANE_TILES, LANES)
    return jnp.swapaxes(a, -3, -2).reshape(*lead, KV_ROWS, LANES)


def _from_kv_rows(a):
    lead = a.shape[:-2]
    a = a.reshape(*lead, XA_LANE_TILES, XA_HEADS, LANES)
    return jnp.swapaxes(a, -3, -2).reshape(*lead, XA_HEADS, XA_HD)


def _mem_kv_kernel(mem_ref, g_ref, wkv_ref, k_ref, v_ref, kb_ref, vb_ref, wkv_s):
    @pl.when(pl.program_id(0) == 0)
    def _():
        wkv_s[...] = wkv_ref[...].astype(BF16)

    m = _rmsnorm(mem_ref[0], g_ref[...]).astype(BF16)
    kv = _dot(m, wkv_s[...])
    k = kv[:, :BRANCH_W]
    v = kv[:, BRANCH_W:]
    kb_ref[0] = k.astype(BF16)
    vb_ref[0] = v.astype(BF16)
    for h in range(XA_HEADS):
        for lt in range(XA_LANE_TILES):
            cols = slice(h * XA_HD + lt * LANES, h * XA_HD + (lt + 1) * LANES)
            rows = pl.ds(lt * XA_HEADS + h, N_MEM, stride=KV_ROWS)
            k_ref[0, rows, :] = k[:, cols]
            v_ref[0, rows, :] = v[:, cols]


def _mem_kv(mem, g, wkv):
    nb = mem.shape[0]
    rows_blk = lambda: pl.BlockSpec((1, N_MEM * KV_ROWS, LANES), lambda b: (b, 0, 0))
    flat_blk = lambda: pl.BlockSpec((1, N_MEM, BRANCH_W), lambda b: (b, 0, 0))
    return pl.pallas_call(
        _mem_kv_kernel,
        grid=(nb,),
        in_specs=[pl.BlockSpec((1, N_MEM, D_MODEL), lambda b: (b, 0, 0)),
                  _resident((1, D_MODEL)),
                  _resident((D_MODEL, 2 * BRANCH_W))],
        out_specs=[rows_blk(), rows_blk(), flat_blk(), flat_blk()],
        out_shape=[jax.ShapeDtypeStruct((nb, N_MEM * KV_ROWS, LANES), F32),
                   jax.ShapeDtypeStruct((nb, N_MEM * KV_ROWS, LANES), F32),
                   jax.ShapeDtypeStruct((nb, N_MEM, BRANCH_W), BF16),
                   jax.ShapeDtypeStruct((nb, N_MEM, BRANCH_W), BF16)],
        scratch_shapes=[pltpu.VMEM((D_MODEL, 2 * BRANCH_W), BF16)],
        compiler_params=pltpu.CompilerParams(dimension_semantics=("arbitrary",),
                                             vmem_limit_bytes=V7X_VMEM_LIMIT_BYTES),
        name="mem_kv",
    )(mem, g, wkv)


def _stage_weights(copies, stage_ref, sem_ref):
    def chunk_copy(k):
        return pltpu.make_async_copy(copies[k][0], stage_ref.at[k % 2], sem_ref.at[k % 2])

    chunk_copy(0).start()
    for k, (_, dst) in enumerate(copies):
        if k + 1 < len(copies):
            chunk_copy(k + 1).start()
        chunk_copy(k).wait()
        dst[...] = stage_ref[k % 2].astype(BF16)


def _sample_attention(q_ref, k_ref, v_ref, o_ref):
    for r in range(q_ref.shape[0]):
        q = q_ref[r] * (XA_HD ** -0.5 * LOG2E)
        part = k_ref[r] * q
        part = part + pltpu.roll(part, XA_HEADS, axis=1)
        s = jnp.sum(part, axis=-1, keepdims=True)
        e = jnp.exp2(s - jnp.max(s, axis=0, keepdims=True))
        o_ref[r] = jnp.sum(e * v_ref[r], axis=0) / jnp.sum(e, axis=0)


def _prompt_kernel(x_ref, k_ref, v_ref, gin_ref, win_hbm, poolw_ref, pscale_ref, sgug_ref,
                   sguw_ref, sgub_ref, wdown_hbm, wout_hbm, gf_ref, sq_ref, sk_ref, sv_ref,
                   y_ref, hist_ref, so_ref,
                   ext_ref, win_ref, wdown_ref, wout_ref, stage_ref, stage_sem):
    i = pl.program_id(1)
    ts = x_ref.shape[1]

    @pl.when((pl.program_id(0) == 0) & (i == 0))
    def _():
        wc = W_STAGE_COLS
        cols = lambda c: pl.ds(c * wc, wc)
        copies = [(win_hbm.at[:, cols(c)], win_ref.at[:, cols(c)]) for c in range(win_ref.shape[1] // wc)]
        copies += [(wdown_hbm.at[n, :, cols(c)], wdown_ref.at[n, :, cols(c)])
                   for n in range(N_BRANCH) for c in range(D_MODEL // wc)]
        copies += [(wout_hbm.at[:, cols(c)], wout_ref.at[:, cols(c)]) for c in range(D_MODEL // wc)]
        _stage_weights(copies, stage_ref, stage_sem)

    @pl.when(i == 0)
    def _():
        ext_ref[0:HIST_ROWS, :] = jnp.zeros((HIST_ROWS, BRANCH_W), F32)

    _sample_attention(sq_ref, sk_ref, sv_ref, so_ref)

    x = x_ref[0]
    hb = _rmsnorm(x, gin_ref[...]).astype(BF16)

    def proj(c):
        return _dot(hb, win_ref[:, c * BRANCH_W:(c + 1) * BRANCH_W])


    p_in = proj(0)
    p_z = proj(1)
    ext_ref[HIST_ROWS:HIST_ROWS + ts, :] = p_in
    pos = i * ts + lax.broadcasted_iota(jnp.int32, (ts, 1), 0)
    pooled = []
    for g, w in enumerate(POOL_WINDOWS):
        sl = slice(g * POOL_GW, (g + 1) * POOL_GW)
        win = ext_ref[:, sl]
        span = 1
        while span < w:
            win = win + pltpu.roll(win, span, axis=0)
            span *= 2
        inv_cnt = 1.0 / jnp.minimum(w, pos + 1).astype(F32)
        pooled.append((win[HIST_ROWS:, :] * inv_cnt - p_in[:, sl]).astype(BF16))
    ext_ref[0:HIST_ROWS, :] = p_in[ts - HIST_ROWS:, :]
    hist_ref[0] = p_in[ts - HIST_ROWS:, :]

    v = proj(3)
    u = proj(2)
    s_z = proj(4)
    mixed = [_dot(pooled[g], poolw_ref[g].astype(BF16)) for g in range(POOL_GROUPS)]
    o_pool = (jnp.concatenate(mixed, axis=1) * pscale_ref[...] * _silu(p_z)).astype(BF16)

    vnb = _rmsnorm(v, sgug_ref[...]).astype(BF16)
    tril = (lax.broadcasted_iota(jnp.int32, (SGU_CHUNK, SGU_CHUNK), 0)
            >= lax.broadcasted_iota(jnp.int32, (SGU_CHUNK, SGU_CHUNK), 1))
    ws = [jnp.where(tril, sguw_ref[g], 0.0).astype(BF16) for g in range(SGU_GROUPS)]
    q = proj(5)
    a_z = proj(6)
    rows = []
    for c in range(ts // SGU_CHUNK):
        rs = slice(c * SGU_CHUNK, (c + 1) * SGU_CHUNK)
        cols = [_dot(ws[g], vnb[rs, g * SGU_GW:(g + 1) * SGU_GW]) + sgub_ref[:, g:g + 1]
                for g in range(SGU_GROUPS)]
        rows.append(jnp.concatenate(cols, axis=1))
    o_sgu = (u * jnp.concatenate(rows, axis=0) * _silu(s_z)).astype(BF16)

    qb = q.astype(BF16)
    scores = [lax.dot_general(qb[:, hd * XA_HD:(hd + 1) * XA_HD], k_ref[0, :, hd * XA_HD:(hd + 1) * XA_HD],
                              (((1,), (1,)), ((), ())), preferred_element_type=F32) * (XA_HD ** -0.5 * LOG2E)
              for hd in range(XA_HEADS)]
    gates = [proj(7 + n) for n in range(N_BRANCH)]
    heads = []
    for hd in range(XA_HEADS):
        s = scores[hd]
        e = jnp.exp2(s - jnp.max(s, axis=-1, keepdims=True))
        pr = e * (1.0 / jnp.sum(e, axis=-1, keepdims=True))
        heads.append(_dot(pr.astype(BF16), v_ref[0, :, hd * XA_HD:(hd + 1) * XA_HD]))
    o_xa = (jnp.concatenate(heads, axis=1) * _silu(a_z)).astype(BF16)

    merged = None
    for n, o in enumerate((o_pool, o_sgu, o_xa)):
        t = _sigmoid(gates[n]) * _dot(o, wdown_ref[n])
        merged = t if merged is None else merged + t
    xn = x + _dot(merged.astype(BF16), wout_ref[...])
    y_ref[0] = _rmsnorm(xn, gf_ref[...])


def _prompt_layer(x, kb, vb, gin, win, poolw, pscale, sgug, sguw, sgub_t, wdown, wout, gf, sq, sk, sv):
    nb, seq, _ = x.shape
    ts = SEQ_TILE
    n_tiles = seq // ts
    ns = sq.shape[0]
    rb = ns // (nb * n_tiles)
    assert rb * nb * n_tiles == ns
    in_cols = win.shape[1]
    assert in_cols % W_STAGE_COLS == 0 and D_MODEL % W_STAGE_COLS == 0
    step = lambda b, i: b * n_tiles + i
    hbm = lambda: pl.BlockSpec(memory_space=pl.ANY)
    return pl.pallas_call(
        _prompt_kernel,
        grid=(nb, n_tiles),
        in_specs=[pl.BlockSpec((1, ts, D_MODEL), lambda b, i: (b, i, 0)),
                  pl.BlockSpec((1, N_MEM, BRANCH_W), lambda b, i: (b, 0, 0)),
                  pl.BlockSpec((1, N_MEM, BRANCH_W), lambda b, i: (b, 0, 0)),
                  _resident((1, D_MODEL)),
                  hbm(),
                  _resident((POOL_GROUPS, POOL_GW, POOL_GW)),
                  _resident((1, BRANCH_W)),
                  _resident((1, BRANCH_W)),
                  _resident((SGU_GROUPS, SGU_CHUNK, SGU_CHUNK)),
                  _resident((SGU_CHUNK, SGU_GROUPS)),
                  hbm(),
                  hbm(),
                  _resident((1, D_MODEL)),
                  pl.BlockSpec((rb, KV_ROWS, LANES), lambda b, i: (step(b, i), 0, 0)),
                  pl.BlockSpec((rb, N_MEM, KV_ROWS, LANES), lambda b, i: (step(b, i), 0, 0, 0)),
                  pl.BlockSpec((rb, N_MEM, KV_ROWS, LANES), lambda b, i: (step(b, i), 0, 0, 0))],
        out_specs=[pl.BlockSpec((1, ts, D_MODEL), lambda b, i: (b, i, 0)),
                   pl.BlockSpec((1, HIST_ROWS, BRANCH_W), lambda b, i: (b, 0, 0)),
                   pl.BlockSpec((rb, KV_ROWS, LANES), lambda b, i: (step(b, i), 0, 0))],
        out_shape=[jax.ShapeDtypeStruct((nb, seq, D_MODEL), F32),
                   jax.ShapeDtypeStruct((nb, HIST_ROWS, BRANCH_W), F32),
                   jax.ShapeDtypeStruct((ns, KV_ROWS, LANES), F32)],
        scratch_shapes=[pltpu.VMEM((HIST_ROWS + ts, BRANCH_W), F32),
                        pltpu.VMEM((D_MODEL, in_cols), BF16),
                        pltpu.VMEM((N_BRANCH, BRANCH_W, D_MODEL), BF16),
                        pltpu.VMEM((D_MODEL, D_MODEL), BF16),
                        pltpu.VMEM((2, D_MODEL, W_STAGE_COLS), F32),
                        pltpu.SemaphoreType.DMA((2,))],
        compiler_params=pltpu.CompilerParams(dimension_semantics=("arbitrary", "arbitrary"),
                                             vmem_limit_bytes=V7X_VMEM_LIMIT_BYTES),
        name="prompt_layer",
    )(x, kb, vb, gin, win, poolw, pscale, sgug, sguw, sgub_t, wdown, wout, gf, sq, sk, sv)


def _sample_pre_kernel(x_ref, hist_ref, gin_ref, win_ref, poolw_ref, pscale_ref, sgug_ref,
                       sgw0_ref, sgb0_ref,
                       newhist_ref, vn_ref, opool_ref, osgu_ref, q_ref, az_ref, gates_ref,
                       hb_ref, proj_ref):
    c = pl.program_id(0)

    @pl.when(c == 0)
    def _():
        hb_ref[...] = _rmsnorm(x_ref[...], gin_ref[...]).astype(BF16)

    proj_ref[c] = _dot(hb_ref[...], win_ref[...].astype(BF16))

    @pl.when(c == N_PROJ - 1)
    def _():
        p_in = proj_ref[0]
        newhist_ref[0:POOL_HIST - 1] = hist_ref[1:POOL_HIST]
        newhist_ref[POOL_HIST - 1] = p_in
        mixed = []
        for g, w in enumerate(POOL_WINDOWS):
            sl = slice(g * POOL_GW, (g + 1) * POOL_GW)
            cur = p_in[:, sl]
            win = cur
            for j in range(1, w):
                win = win + hist_ref[POOL_HIST - j, :, sl]
            cnt = float(min(w, PAST_LEN + 1))
            d = win / cnt - cur
            mixed.append(_dot(d.astype(BF16), poolw_ref[g].astype(BF16)))
        opool_ref[...] = jnp.concatenate(mixed, axis=1) * pscale_ref[...] * _silu(proj_ref[1])

        vn = _rmsnorm(proj_ref[3], sgug_ref[...])
        vn_ref[...] = vn
        osgu_ref[...] = proj_ref[2] * (vn * sgw0_ref[...] + sgb0_ref[...]) * _silu(proj_ref[4])
        q_ref[...] = proj_ref[5]
        az_ref[...] = proj_ref[6]
        for n in range(N_BRANCH):
            gates_ref[:, n * D_MODEL:(n + 1) * D_MODEL] = proj_ref[7 + n]


def _sample_pre(xs, hist, gin, win, poolw, pscale, sgug, sgw0, sgb0):
    n = xs.shape[0]
    assert win.shape[1] == N_PROJ * BRANCH_W
    row = jax.ShapeDtypeStruct((n, BRANCH_W), F32)
    row_blk = lambda: _resident((n, BRANCH_W))
    return pl.pallas_call(
        _sample_pre_kernel,
        grid=(N_PROJ,),
        in_specs=[_resident((n, D_MODEL)),
                  _resident(hist.shape),
                  _resident((1, D_MODEL)),
                  pl.BlockSpec((D_MODEL, BRANCH_W), lambda c: (0, c)),
                  _resident((POOL_GROUPS, POOL_GW, POOL_GW)),
                  _resident((1, BRANCH_W)),
                  _resident((1, BRANCH_W)),
                  _resident((1, BRANCH_W)),
                  _resident((1, BRANCH_W))],
        out_specs=[_resident(hist.shape), row_blk(), row_blk(), row_blk(), row_blk(), row_blk(),
                   _resident((n, N_BRANCH * D_MODEL))],
        out_shape=[jax.ShapeDtypeStruct(hist.shape, F32), row, row, row, row, row,
                   jax.ShapeDtypeStruct((n, N_BRANCH * D_MODEL), F32)],
        scratch_shapes=[pltpu.VMEM((n, D_MODEL), BF16),
                        pltpu.VMEM((N_PROJ, n, BRANCH_W), F32)],
        compiler_params=pltpu.CompilerParams(dimension_semantics=("arbitrary",),
                                             vmem_limit_bytes=V7X_VMEM_LIMIT_BYTES),
        name="sample_pre",
    )(xs, hist, gin, win, poolw, pscale, sgug, sgw0, sgb0)


def _sample_post_kernel(x_ref, opool_ref, osgu_ref, attn_ref, az_ref, gates_ref, wdown_ref, wout_ref,
                        gf_ref, y_ref):
    o_xa = attn_ref[...] * _silu(az_ref[...])
    merged = None
    for n, o in enumerate((opool_ref[...], osgu_ref[...], o_xa)):
        gate = _sigmoid(gates_ref[:, n * D_MODEL:(n + 1) * D_MODEL])
        t = gate * _dot(o.astype(BF16), wdown_ref[n].astype(BF16))
        merged = t if merged is None else merged + t
    xn = x_ref[...] + _dot(merged.astype(BF16), wout_ref[...].astype(BF16))
    y_ref[...] = _rmsnorm(xn, gf_ref[...])


def _sample_post(xs, o_pool, o_sgu, attn, a_z, gates, wdown, wout, gf):
    return pl.pallas_call(
        _sample_post_kernel,
        out_shape=jax.ShapeDtypeStruct(xs.shape, F32),
        compiler_params=pltpu.CompilerParams(vmem_limit_bytes=V7X_VMEM_LIMIT_BYTES),
        name="sample_post",
    )(xs, o_pool, o_sgu, attn, a_z, gates, wdown, wout, gf)


def kernel(x_prompt, x_sample, state_pool, cache_mem_k, cache_mem_v, mem_prompt, norm_in_g, w_in,
           pool_w, pool_scale, sgu_norm_g, sgu_w, sgu_b, mem_norm_g, w_kv, w_down, w_out, norm_f_g):
    depth = w_in.shape[0]
    assert depth == 1, "single-layer step"
    nb, seq, _ = x_prompt.shape
    ns, dec_seq, _ = x_sample.shape
    assert dec_seq == 1 and seq % SEQ_TILE == 0 and seq >= HIST_ROWS

    row = lambda a: a.reshape(1, -1)
    gin, pscale, sgug, gmem, gf = (row(norm_in_g[0]), row(pool_scale[0]), row(sgu_norm_g[0]),
                                   row(mem_norm_g[0]), row(norm_f_g))
    win, poolw, wkv, wdown, wout, sguw = w_in[0], pool_w[0], w_kv[0], w_down[0], w_out[0], sgu_w[0]
    sgub_t = sgu_b[0].T
    sgw0 = jnp.repeat(sgu_w[0, :, 0, 0], SGU_GW).reshape(1, BRANCH_W)
    sgb0 = jnp.repeat(sgu_b[0, :, 0], SGU_GW).reshape(1, BRANCH_W)

    k_rows, v_rows, kb, vb = _mem_kv(mem_prompt, gmem, wkv)

    xs = x_sample.reshape(ns, D_MODEL)
    hist_s = jnp.transpose(state_pool[0], (1, 0, 2))
    new_hist_s, vn_s, o_pool_s, o_sgu_s, q_s, az_s, gates_s = _sample_pre(
        xs, hist_s, gin, win, poolw, pscale, sgug, sgw0, sgb0)

    y_prompt, hist_p, attn_rows = _prompt_layer(
        x_prompt, kb, vb, gin, win, poolw, pscale, sgug, sguw, sgub_t, wdown, wout, gf,
        _to_kv_rows(q_s), _to_kv_rows(cache_mem_k[0].reshape(ns, N_MEM, BRANCH_W)),
        _to_kv_rows(cache_mem_v[0].reshape(ns, N_MEM, BRANCH_W)))

    attn_s = _from_kv_rows(attn_rows).reshape(ns, BRANCH_W)
    y_sample = _sample_post(xs, o_pool_s, o_sgu_s, attn_s, az_s, gates_s, wdown, wout, gf)

    new_pool_p = hist_p[None, :, HIST_ROWS - POOL_HIST:, :]
    new_pool_s = jnp.transpose(new_hist_s, (1, 0, 2))[None]
    kv_out = lambda a: _from_kv_rows(a.reshape(nb, N_MEM, KV_ROWS, LANES))[None]
    return (y_prompt, y_sample.reshape(ns, 1, D_MODEL), new_pool_p, new_pool_s,
            kv_out(k_rows), kv_out(v_rows), vn_s.reshape(1, ns, 1, BRANCH_W))
```

```python
import jax
import jax.numpy as jnp
from jax import lax
from jax.experimental import pallas as pl
from jax.experimental.pallas import tpu as pltpu

D_MODEL = 1024
BRANCH_W = 1024
N_BRANCH = 3
N_PROJ = 7 + N_BRANCH
POOL_WINDOWS = (2, 4, 8, 16)
POOL_GROUPS = len(POOL_WINDOWS)
POOL_GW = BRANCH_W // POOL_GROUPS
POOL_HIST = max(POOL_WINDOWS) - 1
HIST_ROWS = POOL_HIST + 1
SGU_CHUNK = 128
SGU_GROUPS = 4
SGU_GW = BRANCH_W // SGU_GROUPS
N_MEM = 256
XA_HEADS = 4
XA_HD = BRANCH_W // XA_HEADS
EPS = 1e-6
PAST_LEN = 16384

SEQ_TILE = 256
LANES = 128
XA_LANE_TILES = XA_HD // LANES
KV_ROWS = XA_HEADS * XA_LANE_TILES
LOG2E = 1.4426950408889634
W_STAGE_SLOTS = 4
W_STAGE_ROWS = 256
W_STAGE_COLS = 1024
V7X_VMEM_LIMIT_BYTES = 60 * 1024 * 1024

F32 = jnp.float32
BF16 = jnp.bfloat16

_sigmoid = jax.nn.sigmoid


def _rmsnorm(x, g):
    return x * lax.rsqrt(jnp.mean(x * x, axis=-1, keepdims=True) + EPS) * g


def _silu(z):
    return z * _sigmoid(z)


def _dot(a, b):
    return jnp.dot(a, b, preferred_element_type=F32)


def _resident(shape):
    zeros = (0,) * len(shape)
    return pl.BlockSpec(shape, lambda *_: zeros, pipeline_mode=pl.Buffered(1))


def _to_kv_rows(a):
    lead = a.shape[:-1]
    a = a.reshape(*lead, XA_HEADS, XA_LANE_TILES, LANES)
    return jnp.swapaxes(a, -3, -2).reshape(*lead, KV_ROWS, LANES)


def _from_kv_rows(a):
    lead = a.shape[:-2]
    a = a.reshape(*lead, XA_LANE_TILES, XA_HEADS, LANES)
    return jnp.swapaxes(a, -3, -2).reshape(*lead, XA_HEADS, XA_HD)


def _mem_kv_kernel(mem_ref, g_ref, wkv_ref, k_ref, v_ref, kb_ref, vb_ref, wkv_s):
    @pl.when(pl.program_id(0) == 0)
    def _():
        wkv_s[...] = wkv_ref[...].astype(BF16)

    m = _rmsnorm(mem_ref[0], g_ref[...]).astype(BF16)
    kv = _dot(m, wkv_s[...])
    k = kv[:, :BRANCH_W]
    v = kv[:, BRANCH_W:]
    kb_ref[0] = k.astype(BF16)
    vb_ref[0] = v.astype(BF16)
    for h in range(XA_HEADS):
        for lt in range(XA_LANE_TILES):
            cols = slice(h * XA_HD + lt * LANES, h * XA_HD + (lt + 1) * LANES)
            rows = pl.ds(lt * XA_HEADS + h, N_MEM, stride=KV_ROWS)
            k_ref[0, rows, :] = k[:, cols]
            v_ref[0, rows, :] = v[:, cols]


def _mem_kv(mem, g, wkv):
    nb = mem.shape[0]
    rows_blk = lambda: pl.BlockSpec((1, N_MEM * KV_ROWS, LANES), lambda b: (b, 0, 0))
    flat_blk = lambda: pl.BlockSpec((1, N_MEM, BRANCH_W), lambda b: (b, 0, 0))
    return pl.pallas_call(
        _mem_kv_kernel,
        grid=(nb,),
        in_specs=[pl.BlockSpec((1, N_MEM, D_MODEL), lambda b: (b, 0, 0)),
                  _resident((1, D_MODEL)),
                  _resident((D_MODEL, 2 * BRANCH_W))],
        out_specs=[rows_blk(), rows_blk(), flat_blk(), flat_blk()],
        out_shape=[jax.ShapeDtypeStruct((nb, N_MEM * KV_ROWS, LANES), F32),
                   jax.ShapeDtypeStruct((nb, N_MEM * KV_ROWS, LANES), F32),
                   jax.ShapeDtypeStruct((nb, N_MEM, BRANCH_W), BF16),
                   jax.ShapeDtypeStruct((nb, N_MEM, BRANCH_W), BF16)],
        scratch_shapes=[pltpu.VMEM((D_MODEL, 2 * BRANCH_W), BF16)],
        compiler_params=pltpu.CompilerParams(dimension_semantics=("arbitrary",),
                                             vmem_limit_bytes=V7X_VMEM_LIMIT_BYTES),
        name="mem_kv",
    )(mem, g, wkv)


def _stage_weights(copies, stage_ref, sem_ref):
    slots = stage_ref.shape[0]
    ahead = slots - 1

    def chunk_copy(k):
        return pltpu.make_async_copy(copies[k][0], stage_ref.at[k % slots], sem_ref.at[k % slots])

    for k in range(min(ahead, len(copies))):
        chunk_copy(k).start()
    for k, (_, dst) in enumerate(copies):
        if k + ahead < len(copies):
            chunk_copy(k + ahead).start()
        chunk_copy(k).wait()
        dst[...] = stage_ref[k % slots].astype(BF16)


def _sample_attention(q_ref, k_ref, v_ref, o_ref):
    for r in range(q_ref.shape[0]):
        q = q_ref[r] * (XA_HD ** -0.5 * LOG2E)
        part = k_ref[r] * q
        part = part + pltpu.roll(part, XA_HEADS, axis=1)
        s = jnp.sum(part, axis=-1, keepdims=True)
        e = jnp.exp2(s - jnp.max(s, axis=0, keepdims=True))
        o_ref[r] = jnp.sum(e * v_ref[r], axis=0) / jnp.sum(e, axis=0)


def _prompt_kernel(x_ref, k_ref, v_ref, gin_ref, win_hbm, poolw_ref, pscale_ref, sgug_ref,
                   sguw_ref, sgub_ref, wdown_hbm, wout_hbm, gf_ref, sq_ref, sk_ref, sv_ref,
                   y_ref, hist_ref, so_ref,
                   ext_ref, win_ref, wdown_ref, wout_ref, stage_ref, stage_sem):
    i = pl.program_id(1)
    ts = x_ref.shape[1]

    @pl.when((pl.program_id(0) == 0) & (i == 0))
    def _():
        _, sr, sc = stage_ref.shape
        tiles = lambda ref: [(pl.ds(r * sr, sr), pl.ds(c * sc, sc))
                             for c in range(ref.shape[-1] // sc) for r in range(ref.shape[-2] // sr)]
        copies = [(win_hbm.at[r, c], win_ref.at[r, c]) for r, c in tiles(win_ref)]
        copies += [(wdown_hbm.at[n, r, c], wdown_ref.at[n, r, c])
                   for n in range(N_BRANCH) for r, c in tiles(wdown_ref)]
        copies += [(wout_hbm.at[r, c], wout_ref.at[r, c]) for r, c in tiles(wout_ref)]
        _stage_weights(copies, stage_ref, stage_sem)

    @pl.when(i == 0)
    def _():
        ext_ref[0:HIST_ROWS, :] = jnp.zeros((HIST_ROWS, BRANCH_W), F32)

    _sample_attention(sq_ref, sk_ref, sv_ref, so_ref)

    x = x_ref[0]
    hb = _rmsnorm(x, gin_ref[...]).astype(BF16)

    def proj(c):
        return _dot(hb, win_ref[:, c * BRANCH_W:(c + 1) * BRANCH_W])


    p_in = proj(0)
    p_z = proj(1)
    ext_ref[HIST_ROWS:HIST_ROWS + ts, :] = p_in
    pos = i * ts + lax.broadcasted_iota(jnp.int32, (ts, 1), 0)
    pooled = []
    for g, w in enumerate(POOL_WINDOWS):
        sl = slice(g * POOL_GW, (g + 1) * POOL_GW)
        win = ext_ref[:, sl]
        span = 1
        while span < w:
            win = win + pltpu.roll(win, span, axis=0)
            span *= 2
        inv_cnt = 1.0 / jnp.minimum(w, pos + 1).astype(F32)
        pooled.append((win[HIST_ROWS:, :] * inv_cnt - p_in[:, sl]).astype(BF16))
    ext_ref[0:HIST_ROWS, :] = p_in[ts - HIST_ROWS:, :]
    hist_ref[0] = p_in[ts - HIST_ROWS:, :]

    v = proj(3)
    u = proj(2)
    s_z = proj(4)
    mixed = [_dot(pooled[g], poolw_ref[g].astype(BF16)) for g in range(POOL_GROUPS)]
    o_pool = (jnp.concatenate(mixed, axis=1) * pscale_ref[...] * _silu(p_z)).astype(BF16)

    vnb = _rmsnorm(v, sgug_ref[...]).astype(BF16)
    tril = (lax.broadcasted_iota(jnp.int32, (SGU_CHUNK, SGU_CHUNK), 0)
            >= lax.broadcasted_iota(jnp.int32, (SGU_CHUNK, SGU_CHUNK), 1))
    ws = [jnp.where(tril, sguw_ref[g], 0.0).astype(BF16) for g in range(SGU_GROUPS)]
    q = proj(5)
    a_z = proj(6)
    rows = []
    for c in range(ts // SGU_CHUNK):
        rs = slice(c * SGU_CHUNK, (c + 1) * SGU_CHUNK)
        cols = [_dot(ws[g], vnb[rs, g * SGU_GW:(g + 1) * SGU_GW]) + sgub_ref[:, g:g + 1]
                for g in range(SGU_GROUPS)]
        rows.append(jnp.concatenate(cols, axis=1))
    o_sgu = (u * jnp.concatenate(rows, axis=0) * _silu(s_z)).astype(BF16)

    qb = q.astype(BF16)
    scores = [lax.dot_general(qb[:, hd * XA_HD:(hd + 1) * XA_HD], k_ref[0, :, hd * XA_HD:(hd + 1) * XA_HD],
                              (((1,), (1,)), ((), ())), preferred_element_type=F32) * (XA_HD ** -0.5 * LOG2E)
              for hd in range(XA_HEADS)]
    gates = [proj(7 + n) for n in range(N_BRANCH)]
    heads = []
    for hd in range(XA_HEADS):
        s = scores[hd]
        e = jnp.exp2(s - jnp.max(s, axis=-1, keepdims=True))
        pr = e * (1.0 / jnp.sum(e, axis=-1, keepdims=True))
        heads.append(_dot(pr.astype(BF16), v_ref[0, :, hd * XA_HD:(hd + 1) * XA_HD]))
    o_xa = (jnp.concatenate(heads, axis=1) * _silu(a_z)).astype(BF16)

    merged = None
    for n, o in enumerate((o_pool, o_sgu, o_xa)):
        t = _sigmoid(gates[n]) * _dot(o, wdown_ref[n])
        merged = t if merged is None else merged + t
    xn = x + _dot(merged.astype(BF16), wout_ref[...])
    y_ref[0] = _rmsnorm(xn, gf_ref[...])


def _prompt_layer(x, kb, vb, gin, win, poolw, pscale, sgug, sguw, sgub_t, wdown, wout, gf, sq, sk, sv):
    nb, seq, _ = x.shape
    ts = SEQ_TILE
    n_tiles = seq // ts
    ns = sq.shape[0]
    rb = ns // (nb * n_tiles)
    assert rb * nb * n_tiles == ns
    in_cols = win.shape[1]
    assert in_cols % W_STAGE_COLS == 0 and D_MODEL % W_STAGE_COLS == 0 and D_MODEL % W_STAGE_ROWS == 0
    step = lambda b, i: b * n_tiles + i
    hbm = lambda: pl.BlockSpec(memory_space=pl.ANY)
    return pl.pallas_call(
        _prompt_kernel,
        grid=(nb, n_tiles),
        in_specs=[pl.BlockSpec((1, ts, D_MODEL), lambda b, i: (b, i, 0)),
                  pl.BlockSpec((1, N_MEM, BRANCH_W), lambda b, i: (b, 0, 0)),
                  pl.BlockSpec((1, N_MEM, BRANCH_W), lambda b, i: (b, 0, 0)),
                  _resident((1, D_MODEL)),
                  hbm(),
                  _resident((POOL_GROUPS, POOL_GW, POOL_GW)),
                  _resident((1, BRANCH_W)),
                  _resident((1, BRANCH_W)),
                  _resident((SGU_GROUPS, SGU_CHUNK, SGU_CHUNK)),
                  _resident((SGU_CHUNK, SGU_GROUPS)),
                  hbm(),
                  hbm(),
                  _resident((1, D_MODEL)),
                  pl.BlockSpec((rb, KV_ROWS, LANES), lambda b, i: (step(b, i), 0, 0)),
                  pl.BlockSpec((rb, N_MEM, KV_ROWS, LANES), lambda b, i: (step(b, i), 0, 0, 0)),
                  pl.BlockSpec((rb, N_MEM, KV_ROWS, LANES), lambda b, i: (step(b, i), 0, 0, 0))],
        out_specs=[pl.BlockSpec((1, ts, D_MODEL), lambda b, i: (b, i, 0)),
                   pl.BlockSpec((1, HIST_ROWS, BRANCH_W), lambda b, i: (b, 0, 0)),
                   pl.BlockSpec((rb, KV_ROWS, LANES), lambda b, i: (step(b, i), 0, 0))],
        out_shape=[jax.ShapeDtypeStruct((nb, seq, D_MODEL), F32),
                   jax.ShapeDtypeStruct((nb, HIST_ROWS, BRANCH_W), F32),
                   jax.ShapeDtypeStruct((ns, KV_ROWS, LANES), F32)],
        scratch_shapes=[pltpu.VMEM((HIST_ROWS + ts, BRANCH_W), F32),
                        pltpu.VMEM((D_MODEL, in_cols), BF16),
                        pltpu.VMEM((N_BRANCH, BRANCH_W, D_MODEL), BF16),
                        pltpu.VMEM((D_MODEL, D_MODEL), BF16),
                        pltpu.VMEM((W_STAGE_SLOTS, W_STAGE_ROWS, W_STAGE_COLS), F32),
                        pltpu.SemaphoreType.DMA((W_STAGE_SLOTS,))],
        compiler_params=pltpu.CompilerParams(dimension_semantics=("arbitrary", "arbitrary"),
                                             vmem_limit_bytes=V7X_VMEM_LIMIT_BYTES),
        name="prompt_layer",
    )(x, kb, vb, gin, win, poolw, pscale, sgug, sguw, sgub_t, wdown, wout, gf, sq, sk, sv)


def _sample_pre_kernel(x_ref, hist_ref, gin_ref, win_ref, poolw_ref, pscale_ref, sgug_ref,
                       sgw0_ref, sgb0_ref,
                       newhist_ref, vn_ref, opool_ref, osgu_ref, q_ref, az_ref, gates_ref,
                       hb_ref, proj_ref):
    c = pl.program_id(0)

    @pl.when(c == 0)
    def _():
        hb_ref[...] = _rmsnorm(x_ref[...], gin_ref[...]).astype(BF16)

    proj_ref[c] = _dot(hb_ref[...], win_ref[...].astype(BF16))

    @pl.when(c == N_PROJ - 1)
    def _():
        p_in = proj_ref[0]
        newhist_ref[0:POOL_HIST - 1] = hist_ref[1:POOL_HIST]
        newhist_ref[POOL_HIST - 1] = p_in
        mixed = []
        for g, w in enumerate(POOL_WINDOWS):
            sl = slice(g * POOL_GW, (g + 1) * POOL_GW)
            cur = p_in[:, sl]
            win = cur
            for j in range(1, w):
                win = win + hist_ref[POOL_HIST - j, :, sl]
            cnt = float(min(w, PAST_LEN + 1))
            d = win / cnt - cur
            mixed.append(_dot(d.astype(BF16), poolw_ref[g].astype(BF16)))
        opool_ref[...] = jnp.concatenate(mixed, axis=1) * pscale_ref[...] * _silu(proj_ref[1])

        vn = _rmsnorm(proj_ref[3], sgug_ref[...])
        vn_ref[...] = vn
        osgu_ref[...] = proj_ref[2] * (vn * sgw0_ref[...] + sgb0_ref[...]) * _silu(proj_ref[4])
        q_ref[...] = proj_ref[5]
        az_ref[...] = proj_ref[6]
        for n in range(N_BRANCH):
            gates_ref[:, n * D_MODEL:(n + 1) * D_MODEL] = proj_ref[7 + n]


def _sample_pre(xs, hist, gin, win, poolw, pscale, sgug, sgw0, sgb0):
    n = xs.shape[0]
    assert win.shape[1] == N_PROJ * BRANCH_W
    row = jax.ShapeDtypeStruct((n, BRANCH_W), F32)
    row_blk = lambda: _resident((n, BRANCH_W))
    return pl.pallas_call(
        _sample_pre_kernel,
        grid=(N_PROJ,),
        in_specs=[_resident((n, D_MODEL)),
                  _resident(hist.shape),
                  _resident((1, D_MODEL)),
                  pl.BlockSpec((D_MODEL, BRANCH_W), lambda c: (0, c)),
                  _resident((POOL_GROUPS, POOL_GW, POOL_GW)),
                  _resident((1, BRANCH_W)),
                  _resident((1, BRANCH_W)),
                  _resident((1, BRANCH_W)),
                  _resident((1, BRANCH_W))],
        out_specs=[_resident(hist.shape), row_blk(), row_blk(), row_blk(), row_blk(), row_blk(),
                   _resident((n, N_BRANCH * D_MODEL))],
        out_shape=[jax.ShapeDtypeStruct(hist.shape, F32), row, row, row, row, row,
                   jax.ShapeDtypeStruct((n, N_BRANCH * D_MODEL), F32)],
        scratch_shapes=[pltpu.VMEM((n, D_MODEL), BF16),
                        pltpu.VMEM((N_PROJ, n, BRANCH_W), F32)],
        compiler_params=pltpu.CompilerParams(dimension_semantics=("arbitrary",),
                                             vmem_limit_bytes=V7X_VMEM_LIMIT_BYTES),
        name="sample_pre",
    )(xs, hist, gin, win, poolw, pscale, sgug, sgw0, sgb0)


def _sample_post_kernel(x_ref, opool_ref, osgu_ref, attn_ref, az_ref, gates_ref, wdown_ref, wout_ref,
                        gf_ref, y_ref):
    o_xa = attn_ref[...] * _silu(az_ref[...])
    merged = None
    for n, o in enumerate((opool_ref[...], osgu_ref[...], o_xa)):
        gate = _sigmoid(gates_ref[:, n * D_MODEL:(n + 1) * D_MODEL])
        t = gate * _dot(o.astype(BF16), wdown_ref[n].astype(BF16))
        merged = t if merged is None else merged + t
    xn = x_ref[...] + _dot(merged.astype(BF16), wout_ref[...].astype(BF16))
    y_ref[...] = _rmsnorm(xn, gf_ref[...])


def _sample_post(xs, o_pool, o_sgu, attn, a_z, gates, wdown, wout, gf):
    return pl.pallas_call(
        _sample_post_kernel,
        out_shape=jax.ShapeDtypeStruct(xs.shape, F32),
        compiler_params=pltpu.CompilerParams(vmem_limit_bytes=V7X_VMEM_LIMIT_BYTES),
        name="sample_post",
    )(xs, o_pool, o_sgu, attn, a_z, gates, wdown, wout, gf)


def kernel(x_prompt, x_sample, state_pool, cache_mem_k, cache_mem_v, mem_prompt, norm_in_g, w_in,
           pool_w, pool_scale, sgu_norm_g, sgu_w, sgu_b, mem_norm_g, w_kv, w_down, w_out, norm_f_g):
    depth = w_in.shape[0]
    assert depth == 1, "single-layer step"
    nb, seq, _ = x_prompt.shape
    ns, dec_seq, _ = x_sample.shape
    assert dec_seq == 1 and seq % SEQ_TILE == 0 and seq >= HIST_ROWS

    row = lambda a: a.reshape(1, -1)
    gin, pscale, sgug, gmem, gf = (row(norm_in_g[0]), row(pool_scale[0]), row(sgu_norm_g[0]),
                                   row(mem_norm_g[0]), row(norm_f_g))
    win, poolw, wkv, wdown, wout, sguw = w_in[0], pool_w[0], w_kv[0], w_down[0], w_out[0], sgu_w[0]
    sgub_t = sgu_b[0].T
    sgw0 = jnp.repeat(sgu_w[0, :, 0, 0], SGU_GW).reshape(1, BRANCH_W)
    sgb0 = jnp.repeat(sgu_b[0, :, 0], SGU_GW).reshape(1, BRANCH_W)

    k_rows, v_rows, kb, vb = _mem_kv(mem_prompt, gmem, wkv)

    xs = x_sample.reshape(ns, D_MODEL)
    hist_s = jnp.transpose(state_pool[0], (1, 0, 2))
    new_hist_s, vn_s, o_pool_s, o_sgu_s, q_s, az_s, gates_s = _sample_pre(
        xs, hist_s, gin, win, poolw, pscale, sgug, sgw0, sgb0)

    y_prompt, hist_p, attn_rows = _prompt_layer(
        x_prompt, kb, vb, gin, win, poolw, pscale, sgug, sguw, sgub_t, wdown, wout, gf,
        _to_kv_rows(q_s), _to_kv_rows(cache_mem_k[0].reshape(ns, N_MEM, BRANCH_W)),
        _to_kv_rows(cache_mem_v[0].reshape(ns, N_MEM, BRANCH_W)))

    attn_s = _from_kv_rows(attn_rows).reshape(ns, BRANCH_W)
    y_sample = _sample_post(xs, o_pool_s, o_sgu_s, attn_s, az_s, gates_s, wdown, wout, gf)

    new_pool_p = hist_p[None, :, HIST_ROWS - POOL_HIST:, :]
    new_pool_s = jnp.transpose(new_hist_s, (1, 0, 2))[None]
    kv_out = lambda a: _from_kv_rows(a.reshape(nb, N_MEM, KV_ROWS, LANES))[None]
    return (y_prompt, y_sample.reshape(ns, 1, D_MODEL), new_pool_p, new_pool_s,
            kv_out(k_rows), kv_out(v_rows), vn_s.reshape(1, ns, 1, BRANCH_W))
```

```python
import functools

import jax
import jax.numpy as jnp
from jax import lax
from jax.experimental import pallas as pl
from jax.experimental.pallas import tpu as pltpu

D_MODEL = 1024
BRANCH_W = 1024
N_BRANCH = 3
N_PROJ = 7 + N_BRANCH
POOL_WINDOWS = (2, 4, 8, 16)
POOL_GROUPS = len(POOL_WINDOWS)
POOL_GW = BRANCH_W // POOL_GROUPS
POOL_HIST = max(POOL_WINDOWS) - 1
HIST_ROWS = POOL_HIST + 1
SGU_CHUNK = 128
SGU_GROUPS = 4
SGU_GW = BRANCH_W // SGU_GROUPS
N_MEM = 256
XA_HEADS = 4
XA_HD = BRANCH_W // XA_HEADS
EPS = 1e-6
PAST_LEN = 16384

SEQ_TILE = 256
LANES = 128
XA_LANE_TILES = XA_HD // LANES
KV_ROWS = XA_HEADS * XA_LANE_TILES
LOG2E = 1.4426950408889634
W_STAGE_SLOTS = 4
W_STAGE_ROWS = 256
W_STAGE_COLS = 1024
V7X_VMEM_LIMIT_BYTES = 60 * 1024 * 1024

F32 = jnp.float32
BF16 = jnp.bfloat16

_sigmoid = jax.nn.sigmoid


def _rmsnorm(x, g):
    return x * lax.rsqrt(jnp.mean(x * x, axis=-1, keepdims=True) + EPS) * g


def _silu(z):
    return z * _sigmoid(z)


def _dot(a, b):
    return jnp.dot(a, b, preferred_element_type=F32)


def _resident(shape):
    zeros = (0,) * len(shape)
    return pl.BlockSpec(shape, lambda *_: zeros, pipeline_mode=pl.Buffered(1))


def _to_kv_rows(a):
    lead = a.shape[:-1]
    a = a.reshape(*lead, XA_HEADS, XA_LANE_TILES, LANES)
    return jnp.swapaxes(a, -3, -2).reshape(*lead, KV_ROWS, LANES)


ROW_TILES = D_MODEL // LANES
NATURAL_ORDER = tuple(range(ROW_TILES))
KV_ROW_ORDER = tuple((j % XA_HEADS) * XA_LANE_TILES + j // XA_HEADS for j in range(KV_ROWS))


def _load_row_tiles(ref, order=NATURAL_ORDER):
    n = ref.shape[0] // len(order)
    tiles = [None] * len(order)
    for j, t in enumerate(order):
        tiles[t] = ref[pl.ds(j, n, stride=len(order)), :]
    return jnp.concatenate(tiles, axis=1)


def _store_row_tiles(ref, val, order=NATURAL_ORDER):
    n = val.shape[0]
    for j, t in enumerate(order):
        ref[pl.ds(j, n, stride=len(order)), :] = val[:, t * LANES:(t + 1) * LANES]


def _from_kv_rows(a):
    lead = a.shape[:-2]
    a = a.reshape(*lead, XA_LANE_TILES, XA_HEADS, LANES)
    return jnp.swapaxes(a, -3, -2).reshape(*lead, XA_HEADS, XA_HD)


def _mem_kv_body(mem_ref, g_ref, wkv_s, k_ref, v_ref, kb_ref, vb_ref):
    m = _rmsnorm(mem_ref[0], g_ref[...]).astype(BF16)
    kv = _dot(m, wkv_s[...])
    k = kv[:, :BRANCH_W]
    v = kv[:, BRANCH_W:]
    kb_ref[0] = k.astype(BF16)
    vb_ref[0] = v.astype(BF16)
    for h in range(XA_HEADS):
        for lt in range(XA_LANE_TILES):
            cols = slice(h * XA_HD + lt * LANES, h * XA_HD + (lt + 1) * LANES)
            rows = pl.ds(lt * XA_HEADS + h, N_MEM, stride=KV_ROWS)
            k_ref[0, rows, :] = k[:, cols]
            v_ref[0, rows, :] = v[:, cols]


def _stage_weights(copies, stage_ref, sem_ref):
    slots = stage_ref.shape[0]
    ahead = slots - 1

    def chunk_copy(k):
        return pltpu.make_async_copy(copies[k][0], stage_ref.at[k % slots], sem_ref.at[k % slots])

    for k in range(min(ahead, len(copies))):
        chunk_copy(k).start(priority=k % 2)
    for k, (_, dst) in enumerate(copies):
        if k + ahead < len(copies):
            chunk_copy(k + ahead).start(priority=(k + ahead) % 2)
        chunk_copy(k).wait()
        dst[...] = stage_ref[k % slots].astype(BF16)


def _sample_attention(q_ref, k_ref, v_ref, o_ref):
    for r in range(q_ref.shape[0]):
        q = q_ref[r] * (XA_HD ** -0.5 * LOG2E)
        part = k_ref[r] * q
        part = part + pltpu.roll(part, XA_HEADS, axis=1)
        s = jnp.sum(part, axis=-1, keepdims=True)
        e = jnp.exp2(s - jnp.max(s, axis=0, keepdims=True))
        o_ref[r] = jnp.sum(e * v_ref[r], axis=0) / jnp.sum(e, axis=0)


def _prompt_kernel(x_ref, k_ref, v_ref, gin_ref, win_hbm, poolw_ref, pscale_ref, sgug_ref,
                   sguw_ref, sgub_ref, wdown_hbm, wout_hbm, gf_ref, sq_ref, sk_ref, sv_ref,
                   y_ref, hist_ref, so_ref,
                   ext_ref, win_ref, wdown_ref, wout_ref, stage_ref, stage_sem):
    i = pl.program_id(1)
    ts = x_ref.shape[1]

    @pl.when((pl.program_id(0) == 0) & (i == 0))
    def _():
        _, sr, sc = stage_ref.shape
        tiles = lambda ref: [(pl.ds(r * sr, sr), pl.ds(c * sc, sc))
                             for c in range(ref.shape[-1] // sc) for r in range(ref.shape[-2] // sr)]
        copies = [(win_hbm.at[r, c], win_ref.at[r, c]) for r, c in tiles(win_ref)]
        copies += [(wdown_hbm.at[n, r, c], wdown_ref.at[n, r, c])
                   for n in range(N_BRANCH) for r, c in tiles(wdown_ref)]
        copies += [(wout_hbm.at[r, c], wout_ref.at[r, c]) for r, c in tiles(wout_ref)]
        _stage_weights(copies, stage_ref, stage_sem)

    @pl.when(i == 0)
    def _():
        ext_ref[0:HIST_ROWS, :] = jnp.zeros((HIST_ROWS, BRANCH_W), F32)

    _sample_attention(sq_ref, sk_ref, sv_ref, so_ref)

    x = x_ref[0]
    hb = _rmsnorm(x, gin_ref[...]).astype(BF16)

    def proj(c):
        return _dot(hb, win_ref[:, c * BRANCH_W:(c + 1) * BRANCH_W])


    p_in = proj(0)
    p_z = proj(1)
    ext_ref[HIST_ROWS:HIST_ROWS + ts, :] = p_in
    pos = i * ts + lax.broadcasted_iota(jnp.int32, (ts, 1), 0)
    pooled = []
    for g, w in enumerate(POOL_WINDOWS):
        sl = slice(g * POOL_GW, (g + 1) * POOL_GW)
        win = ext_ref[:, sl]
        span = 1
        while span < w:
            win = win + pltpu.roll(win, span, axis=0)
            span *= 2
        inv_cnt = 1.0 / jnp.minimum(w, pos + 1).astype(F32)
        pooled.append((win[HIST_ROWS:, :] * inv_cnt - p_in[:, sl]).astype(BF16))
    ext_ref[0:HIST_ROWS, :] = p_in[ts - HIST_ROWS:, :]
    hist_ref[0] = p_in[ts - HIST_ROWS:, :]

    v = proj(3)
    u = proj(2)
    s_z = proj(4)
    mixed = [_dot(pooled[g], poolw_ref[g].astype(BF16)) for g in range(POOL_GROUPS)]
    o_pool = (jnp.concatenate(mixed, axis=1) * pscale_ref[...] * _silu(p_z)).astype(BF16)

    vnb = _rmsnorm(v, sgug_ref[...]).astype(BF16)
    tril = (lax.broadcasted_iota(jnp.int32, (SGU_CHUNK, SGU_CHUNK), 0)
            >= lax.broadcasted_iota(jnp.int32, (SGU_CHUNK, SGU_CHUNK), 1))
    ws = [jnp.where(tril, sguw_ref[g], 0.0).astype(BF16) for g in range(SGU_GROUPS)]
    q = proj(5)
    a_z = proj(6)
    rows = []
    for c in range(ts // SGU_CHUNK):
        rs = slice(c * SGU_CHUNK, (c + 1) * SGU_CHUNK)
        cols = [_dot(ws[g], vnb[rs, g * SGU_GW:(g + 1) * SGU_GW]) + sgub_ref[:, g:g + 1]
                for g in range(SGU_GROUPS)]
        rows.append(jnp.concatenate(cols, axis=1))
    o_sgu = (u * jnp.concatenate(rows, axis=0) * _silu(s_z)).astype(BF16)

    qb = q.astype(BF16)
    scores = [lax.dot_general(qb[:, hd * XA_HD:(hd + 1) * XA_HD], k_ref[0, :, hd * XA_HD:(hd + 1) * XA_HD],
                              (((1,), (1,)), ((), ())), preferred_element_type=F32) * (XA_HD ** -0.5 * LOG2E)
              for hd in range(XA_HEADS)]
    gates = [proj(7 + n) for n in range(N_BRANCH)]
    heads = []
    for hd in range(XA_HEADS):
        s = scores[hd]
        e = jnp.exp2(s - jnp.max(s, axis=-1, keepdims=True))
        pr = e * (1.0 / jnp.sum(e, axis=-1, keepdims=True))
        heads.append(_dot(pr.astype(BF16), v_ref[0, :, hd * XA_HD:(hd + 1) * XA_HD]))
    o_xa = (jnp.concatenate(heads, axis=1) * _silu(a_z)).astype(BF16)

    merged = None
    for n, o in enumerate((o_pool, o_sgu, o_xa)):
        t = _sigmoid(gates[n]) * _dot(o, wdown_ref[n])
        merged = t if merged is None else merged + t
    xn = x + _dot(merged.astype(BF16), wout_ref[...])
    y_ref[0] = _rmsnorm(xn, gf_ref[...])


def _prompt_layer(x, kb, vb, gin, win, poolw, pscale, sgug, sguw, sgub_t, wdown, wout, gf, sq, sk, sv):
    nb, seq, _ = x.shape
    ts = SEQ_TILE
    n_tiles = seq // ts
    ns = sq.shape[0]
    rb = ns // (nb * n_tiles)
    assert rb * nb * n_tiles == ns
    in_cols = win.shape[1]
    assert in_cols % W_STAGE_COLS == 0 and D_MODEL % W_STAGE_COLS == 0 and D_MODEL % W_STAGE_ROWS == 0
    step = lambda b, i: b * n_tiles + i
    hbm = lambda: pl.BlockSpec(memory_space=pl.ANY)
    return pl.pallas_call(
        _prompt_kernel,
        grid=(nb, n_tiles),
        in_specs=[pl.BlockSpec((1, ts, D_MODEL), lambda b, i: (b, i, 0)),
                  pl.BlockSpec((1, N_MEM, BRANCH_W), lambda b, i: (b, 0, 0)),
                  pl.BlockSpec((1, N_MEM, BRANCH_W), lambda b, i: (b, 0, 0)),
                  _resident((1, D_MODEL)),
                  hbm(),
                  _resident((POOL_GROUPS, POOL_GW, POOL_GW)),
                  _resident((1, BRANCH_W)),
                  _resident((1, BRANCH_W)),
                  _resident((SGU_GROUPS, SGU_CHUNK, SGU_CHUNK)),
                  _resident((SGU_CHUNK, SGU_GROUPS)),
                  hbm(),
                  hbm(),
                  _resident((1, D_MODEL)),
                  pl.BlockSpec((rb, KV_ROWS, LANES), lambda b, i: (step(b, i), 0, 0)),
                  pl.BlockSpec((rb, N_MEM, KV_ROWS, LANES), lambda b, i: (step(b, i), 0, 0, 0)),
                  pl.BlockSpec((rb, N_MEM, KV_ROWS, LANES), lambda b, i: (step(b, i), 0, 0, 0))],
        out_specs=[pl.BlockSpec((1, ts, D_MODEL), lambda b, i: (b, i, 0)),
                   pl.BlockSpec((1, HIST_ROWS, BRANCH_W), lambda b, i: (b, 0, 0)),
                   pl.BlockSpec((rb, KV_ROWS, LANES), lambda b, i: (step(b, i), 0, 0))],
        out_shape=[jax.ShapeDtypeStruct((nb, seq, D_MODEL), F32),
                   jax.ShapeDtypeStruct((nb, HIST_ROWS, BRANCH_W), F32),
                   jax.ShapeDtypeStruct((ns, KV_ROWS, LANES), F32)],
        scratch_shapes=[pltpu.VMEM((HIST_ROWS + ts, BRANCH_W), F32),
                        pltpu.VMEM((D_MODEL, in_cols), BF16),
                        pltpu.VMEM((N_BRANCH, BRANCH_W, D_MODEL), BF16),
                        pltpu.VMEM((D_MODEL, D_MODEL), BF16),
                        pltpu.VMEM((W_STAGE_SLOTS, W_STAGE_ROWS, W_STAGE_COLS), F32),
                        pltpu.SemaphoreType.DMA((W_STAGE_SLOTS,))],
        compiler_params=pltpu.CompilerParams(dimension_semantics=("arbitrary", "arbitrary"),
                                             vmem_limit_bytes=V7X_VMEM_LIMIT_BYTES),
        name="prompt_layer",
    )(x, kb, vb, gin, win, poolw, pscale, sgug, sguw, sgub_t, wdown, wout, gf, sq, sk, sv)


def _prep_kernel(n_mem_requests,
                 x_ref, hist_ref, gin_ref, win_ref, poolw_ref, pscale_ref, sgug_ref, sguw_ref, sgub_ref,
                 mem_ref, gmem_ref, wkv_ref,
                 newhist_ref, vn_ref, opool_ref, osgu_ref, q_ref, az_ref, gates_ref,
                 k_ref, v_ref, kb_ref, vb_ref,
                 hb_ref, proj_ref, wkv_s):
    c = pl.program_id(0)

    @pl.when(c == 0)
    def _():
        hb_ref[...] = _rmsnorm(_load_row_tiles(x_ref), gin_ref[...]).astype(BF16)
        wkv_s[...] = wkv_ref[...].astype(BF16)

    proj_ref[c] = _dot(hb_ref[...], win_ref[...].astype(BF16))

    @pl.when(c < n_mem_requests)
    def _():
        _mem_kv_body(mem_ref, gmem_ref, wkv_s, k_ref, v_ref, kb_ref, vb_ref)

    @pl.when(c == N_PROJ - 1)
    def _():
        p_in = proj_ref[0]
        newhist_ref[0:POOL_HIST - 1] = hist_ref[1:POOL_HIST]
        newhist_ref[POOL_HIST - 1] = p_in
        mixed = []
        for g, w in enumerate(POOL_WINDOWS):
            sl = slice(g * POOL_GW, (g + 1) * POOL_GW)
            cur = p_in[:, sl]
            win = cur
            for j in range(1, w):
                win = win + hist_ref[POOL_HIST - j, :, sl]
            cnt = float(min(w, PAST_LEN + 1))
            d = win / cnt - cur
            mixed.append(_dot(d.astype(BF16), poolw_ref[g].astype(BF16)))
        opool_ref[...] = jnp.concatenate(mixed, axis=1) * pscale_ref[...] * _silu(proj_ref[1])

        vn = _rmsnorm(proj_ref[3], sgug_ref[...])
        _store_row_tiles(vn_ref, vn)
        gated = [vn[:, g * SGU_GW:(g + 1) * SGU_GW] * sguw_ref[g, 0:1, 0:1] + sgub_ref[g:g + 1, 0:1]
                 for g in range(SGU_GROUPS)]
        osgu_ref[...] = proj_ref[2] * jnp.concatenate(gated, axis=1) * _silu(proj_ref[4])
        _store_row_tiles(q_ref, proj_ref[5], KV_ROW_ORDER)
        az_ref[...] = proj_ref[6]
        for n in range(N_BRANCH):
            gates_ref[:, n * D_MODEL:(n + 1) * D_MODEL] = proj_ref[7 + n]


def _prep(x_tiles, hist, gin, win, poolw, pscale, sgug, sguw, sgub, mem, gmem, wkv):
    n = x_tiles.shape[0] // ROW_TILES
    nb = mem.shape[0]
    assert win.shape[1] == N_PROJ * BRANCH_W and nb <= N_PROJ
    row = jax.ShapeDtypeStruct((n, BRANCH_W), F32)
    row_blk = lambda: _resident((n, BRANCH_W))
    tiles = jax.ShapeDtypeStruct((n * ROW_TILES, LANES), F32)
    tiles_blk = lambda: _resident((n * ROW_TILES, LANES))
    req = lambda c: jnp.minimum(c, nb - 1)
    rows_blk = lambda: pl.BlockSpec((1, N_MEM * KV_ROWS, LANES), lambda c: (req(c), 0, 0))
    flat_blk = lambda: pl.BlockSpec((1, N_MEM, BRANCH_W), lambda c: (req(c), 0, 0))
    return pl.pallas_call(
        functools.partial(_prep_kernel, nb),
        grid=(N_PROJ,),
        in_specs=[tiles_blk(),
                  _resident(hist.shape),
                  _resident((1, D_MODEL)),
                  pl.BlockSpec((D_MODEL, BRANCH_W), lambda c: (0, c)),
                  _resident((POOL_GROUPS, POOL_GW, POOL_GW)),
                  _resident((1, BRANCH_W)),
                  _resident((1, BRANCH_W)),
                  _resident((SGU_GROUPS, SGU_CHUNK, SGU_CHUNK)),
                  _resident((SGU_GROUPS, SGU_CHUNK)),
                  pl.BlockSpec((1, N_MEM, D_MODEL), lambda c: (req(c), 0, 0)),
                  _resident((1, D_MODEL)),
                  _resident((D_MODEL, 2 * BRANCH_W))],
        out_specs=[_resident(hist.shape), tiles_blk(), row_blk(), row_blk(), tiles_blk(), row_blk(),
                   _resident((n, N_BRANCH * D_MODEL)),
                   rows_blk(), rows_blk(), flat_blk(), flat_blk()],
        out_shape=[jax.ShapeDtypeStruct(hist.shape, F32), tiles, row, row, tiles, row,
                   jax.ShapeDtypeStruct((n, N_BRANCH * D_MODEL), F32),
                   jax.ShapeDtypeStruct((nb, N_MEM * KV_ROWS, LANES), F32),
                   jax.ShapeDtypeStruct((nb, N_MEM * KV_ROWS, LANES), F32),
                   jax.ShapeDtypeStruct((nb, N_MEM, BRANCH_W), BF16),
                   jax.ShapeDtypeStruct((nb, N_MEM, BRANCH_W), BF16)],
        scratch_shapes=[pltpu.VMEM((n, D_MODEL), BF16),
                        pltpu.VMEM((N_PROJ, n, BRANCH_W), F32),
                        pltpu.VMEM((D_MODEL, 2 * BRANCH_W), BF16)],
        compiler_params=pltpu.CompilerParams(dimension_semantics=("arbitrary",),
                                             vmem_limit_bytes=V7X_VMEM_LIMIT_BYTES),
        name="prep",
    )(x_tiles, hist, gin, win, poolw, pscale, sgug, sguw, sgub, mem, gmem, wkv)


def _sample_post_kernel(x_ref, opool_ref, osgu_ref, attn_ref, az_ref, gates_ref, wdown_ref, wout_ref,
                        gf_ref, y_ref):
    o_xa = _load_row_tiles(attn_ref, KV_ROW_ORDER) * _silu(az_ref[...])
    merged = None
    for n, o in enumerate((opool_ref[...], osgu_ref[...], o_xa)):
        gate = _sigmoid(gates_ref[:, n * D_MODEL:(n + 1) * D_MODEL])
        t = gate * _dot(o.astype(BF16), wdown_ref[n].astype(BF16))
        merged = t if merged is None else merged + t
    xn = _load_row_tiles(x_ref) + _dot(merged.astype(BF16), wout_ref[...].astype(BF16))
    _store_row_tiles(y_ref, _rmsnorm(xn, gf_ref[...]))


def _sample_post(x_tiles, o_pool, o_sgu, attn_tiles, a_z, gates, wdown, wout, gf):
    return pl.pallas_call(
        _sample_post_kernel,
        out_shape=jax.ShapeDtypeStruct(x_tiles.shape, F32),
        compiler_params=pltpu.CompilerParams(vmem_limit_bytes=V7X_VMEM_LIMIT_BYTES),
        name="sample_post",
    )(x_tiles, o_pool, o_sgu, attn_tiles, a_z, gates, wdown, wout, gf)


def kernel(x_prompt, x_sample, state_pool, cache_mem_k, cache_mem_v, mem_prompt, norm_in_g, w_in,
           pool_w, pool_scale, sgu_norm_g, sgu_w, sgu_b, mem_norm_g, w_kv, w_down, w_out, norm_f_g):
    depth = w_in.shape[0]
    assert depth == 1, "single-layer step"
    nb, seq, _ = x_prompt.shape
    ns, dec_seq, _ = x_sample.shape
    assert dec_seq == 1 and seq % SEQ_TILE == 0 and seq >= HIST_ROWS

    row = lambda a: a.reshape(1, -1)
    gin, pscale, sgug, gmem, gf = (row(norm_in_g[0]), row(pool_scale[0]), row(sgu_norm_g[0]),
                                   row(mem_norm_g[0]), row(norm_f_g))
    win, poolw, wkv, wdown, wout, sguw, sgub = (w_in[0], pool_w[0], w_kv[0], w_down[0], w_out[0],
                                                sgu_w[0], sgu_b[0])
    sgub_t = sgub.T

    x_tiles = x_sample.reshape(ns * ROW_TILES, LANES)
    hist_s = jnp.transpose(state_pool[0], (1, 0, 2))
    new_hist_s, vn_tiles, o_pool_s, o_sgu_s, q_tiles, az_s, gates_s, k_rows, v_rows, kb, vb = _prep(
        x_tiles, hist_s, gin, win, poolw, pscale, sgug, sguw, sgub, mem_prompt, gmem, wkv)

    y_prompt, hist_p, attn_rows = _prompt_layer(
        x_prompt, kb, vb, gin, win, poolw, pscale, sgug, sguw, sgub_t, wdown, wout, gf,
        q_tiles.reshape(ns, KV_ROWS, LANES), _to_kv_rows(cache_mem_k[0].reshape(ns, N_MEM, BRANCH_W)),
        _to_kv_rows(cache_mem_v[0].reshape(ns, N_MEM, BRANCH_W)))

    y_tiles = _sample_post(x_tiles, o_pool_s, o_sgu_s, attn_rows.reshape(ns * KV_ROWS, LANES), az_s, gates_s,
                           wdown, wout, gf)

    new_pool_p = hist_p[None, :, HIST_ROWS - POOL_HIST:, :]
    new_pool_s = jnp.transpose(new_hist_s, (1, 0, 2))[None]
    kv_out = lambda a: _from_kv_rows(a.reshape(nb, N_MEM, KV_ROWS, LANES))[None]
    return (y_prompt, y_tiles.reshape(ns, 1, D_MODEL), new_pool_p, new_pool_s,
            kv_out(k_rows), kv_out(v_rows), vn_tiles.reshape(1, ns, 1, BRANCH_W))
```

```python
import jax
import jax.numpy as jnp
from jax import lax
from jax.experimental import pallas as pl
from jax.experimental.pallas import tpu as pltpu

D_MODEL = 1024
BRANCH_W = 1024
N_BRANCH = 3
N_PROJ = 7 + N_BRANCH
POOL_WINDOWS = (2, 4, 8, 16)
POOL_GROUPS = len(POOL_WINDOWS)
POOL_GW = BRANCH_W // POOL_GROUPS
POOL_HIST = max(POOL_WINDOWS) - 1
HIST_ROWS = POOL_HIST + 1
SGU_CHUNK = 128
SGU_GROUPS = 4
SGU_GW = BRANCH_W // SGU_GROUPS
N_MEM = 256
XA_HEADS = 4
XA_HD = BRANCH_W // XA_HEADS
EPS = 1e-6
PAST_LEN = 16384

SEQ_TILE = 256
LANES = 128
XA_LANE_TILES = XA_HD // LANES
KV_ROWS = XA_HEADS * XA_LANE_TILES
LOG2E = 1.4426950408889634
W_STAGE_SLOTS = 4
W_STAGE_ROWS = 256
W_STAGE_COLS = 1024
V7X_VMEM_LIMIT_BYTES = 60 * 1024 * 1024

F32 = jnp.float32
BF16 = jnp.bfloat16

_sigmoid = jax.nn.sigmoid


def _rmsnorm(x, g):
    return x * lax.rsqrt(jnp.mean(x * x, axis=-1, keepdims=True) + EPS) * g


def _silu(z):
    return z * _sigmoid(z)


def _dot(a, b):
    return jnp.dot(a, b, preferred_element_type=F32)


def _resident(shape):
    zeros = (0,) * len(shape)
    return pl.BlockSpec(shape, lambda *_: zeros, pipeline_mode=pl.Buffered(1))


def _to_kv_rows(a):
    lead = a.shape[:-1]
    a = a.reshape(*lead, XA_HEADS, XA_LANE_TILES, LANES)
    return jnp.swapaxes(a, -3, -2).reshape(*lead, KV_ROWS, LANES)


ROW_TILES = D_MODEL // LANES
NATURAL_ORDER = tuple(range(ROW_TILES))
KV_ROW_ORDER = tuple((j % XA_HEADS) * XA_LANE_TILES + j // XA_HEADS for j in range(KV_ROWS))


def _load_row_tiles(ref, order=NATURAL_ORDER):
    n = ref.shape[0] // len(order)
    tiles = [None] * len(order)
    for j, t in enumerate(order):
        tiles[t] = ref[pl.ds(j, n, stride=len(order)), :]
    return jnp.concatenate(tiles, axis=1)


def _store_row_tiles(ref, val, order=NATURAL_ORDER):
    n = val.shape[0]
    for j, t in enumerate(order):
        ref[pl.ds(j, n, stride=len(order)), :] = val[:, t * LANES:(t + 1) * LANES]


def _from_kv_rows(a):
    lead = a.shape[:-2]
    a = a.reshape(*lead, XA_LANE_TILES, XA_HEADS, LANES)
    return jnp.swapaxes(a, -3, -2).reshape(*lead, XA_HEADS, XA_HD)


def _mem_kv_body(mem_ref, g_ref, wkv_s, k_ref, v_ref, kb_ref, vb_ref):
    m = _rmsnorm(mem_ref[0], g_ref[...]).astype(BF16)
    kv = _dot(m, wkv_s[...])
    k = kv[:, :BRANCH_W]
    v = kv[:, BRANCH_W:]
    kb_ref[0] = k.astype(BF16)
    vb_ref[0] = v.astype(BF16)
    for h in range(XA_HEADS):
        for lt in range(XA_LANE_TILES):
            cols = slice(h * XA_HD + lt * LANES, h * XA_HD + (lt + 1) * LANES)
            rows = pl.ds(lt * XA_HEADS + h, N_MEM, stride=KV_ROWS)
            k_ref[0, rows, :] = k[:, cols]
            v_ref[0, rows, :] = v[:, cols]


def _stage_weights(copies, stage_ref, sem_ref, after_chunk):
    slots = stage_ref.shape[0]
    ahead = slots - 1

    def chunk_copy(k):
        return pltpu.make_async_copy(copies[k][0], stage_ref.at[k % slots], sem_ref.at[k % slots])

    for k in range(min(ahead, len(copies))):
        chunk_copy(k).start()
    for k, (_, dst) in enumerate(copies):
        if k + ahead < len(copies):
            chunk_copy(k + ahead).start()
        chunk_copy(k).wait()
        dst[...] = stage_ref[k % slots].astype(BF16)
        after_chunk(k)


class _HbmWriter:
    def __init__(self, stage_ref, sem_ref):
        self.stage_ref, self.sem_ref, self.pending, self.count = stage_ref, sem_ref, [], 0

    def write(self, value, dst_hbm):
        slots = self.stage_ref.shape[0]
        slot = self.count % slots
        if len(self.pending) == slots:
            self.pending.pop(0).wait()
        self.stage_ref[slot] = value
        copy = pltpu.make_async_copy(self.stage_ref.at[slot], dst_hbm, self.sem_ref.at[slot])
        copy.start()
        self.pending.append(copy)
        self.count += 1

    def finish(self):
        for copy in self.pending:
            copy.wait()
        self.pending = []


def _first_step(xs_ref, hwin_ref, gin_ref, poolw_ref, pscale_ref, sgug_ref, sguw_ref, sgub_ref,
                win_hbm, wdown_hbm, wout_hbm, win_ref, wdown_ref, wout_ref, stage_ref, stage_sem,
                qs_ref, ostage_ref, ostage_sem, pin_hbm, vn_hbm, opool_hbm, osgu_hbm, az_hbm, gate_hbms):
    _, sr, sc = stage_ref.shape
    assert sc == BRANCH_W and D_MODEL % sr == 0
    k_slabs = D_MODEL // sr
    tiles = lambda ref: [(pl.ds(r * sr, sr), pl.ds(c * sc, sc))
                         for c in range(ref.shape[-1] // sc) for r in range(ref.shape[-2] // sr)]
    copies = [(win_hbm.at[r, c], win_ref.at[r, c]) for r, c in tiles(win_ref)]
    n_win = len(copies)
    copies += [(wdown_hbm.at[n, r, c], wdown_ref.at[n, r, c]) for n in range(N_BRANCH) for r, c in tiles(wdown_ref)]
    copies += [(wout_hbm.at[r, c], wout_ref.at[r, c]) for r, c in tiles(wout_ref)]

    hbs = _rmsnorm(_load_row_tiles(xs_ref), gin_ref[...]).astype(BF16)
    writer = _HbmWriter(ostage_ref, ostage_sem)
    kept = {}

    def finish_chunk(c, val):
        if c == 0:
            kept["p_in"] = val
            writer.write(val, pin_hbm)
        elif c == 1:
            p_in = kept["p_in"]
            mixed = []
            for g, w in enumerate(POOL_WINDOWS):
                sl = slice(g * POOL_GW, (g + 1) * POOL_GW)
                d = (p_in[:, sl] + hwin_ref[:, sl]) / float(min(w, PAST_LEN + 1)) - p_in[:, sl]
                mixed.append(_dot(d.astype(BF16), poolw_ref[g].astype(BF16)))
            writer.write(jnp.concatenate(mixed, axis=1) * pscale_ref[...] * _silu(val), opool_hbm)
        elif c == 2:
            kept["u"] = val
        elif c == 3:
            vn = _rmsnorm(val, sgug_ref[...])
            writer.write(vn, vn_hbm)
            gated = [vn[:, g * SGU_GW:(g + 1) * SGU_GW] * sguw_ref[g, 0:1, 0:1] + sgub_ref[0:1, g:g + 1]
                     for g in range(SGU_GROUPS)]
            kept["u_gated"] = kept["u"] * jnp.concatenate(gated, axis=1)
        elif c == 4:
            writer.write(kept["u_gated"] * _silu(val), osgu_hbm)
        elif c == 5:
            _store_row_tiles(qs_ref, val, KV_ROW_ORDER)
        elif c == 6:
            writer.write(val, az_hbm)
        else:
            writer.write(val, gate_hbms[c - 7])

    def after_chunk(k):
        if k >= n_win:
            return
        c, r = divmod(k, k_slabs)
        part = _dot(hbs[:, r * sr:(r + 1) * sr], win_ref[r * sr:(r + 1) * sr, c * sc:(c + 1) * sc])
        kept["acc"] = part if r == 0 else kept["acc"] + part
        if r == k_slabs - 1:
            finish_chunk(c, kept["acc"])

    _stage_weights(copies, stage_ref, stage_sem, after_chunk)
    writer.finish()


def _sample_attention(get_q, k_ref, v_ref, o_ref):
    for r in range(k_ref.shape[0]):
        q = get_q(r) * (XA_HD ** -0.5 * LOG2E)
        part = k_ref[r] * q
        part = part + pltpu.roll(part, XA_HEADS, axis=1)
        s = jnp.sum(part, axis=-1, keepdims=True)
        e = jnp.exp2(s - jnp.max(s, axis=0, keepdims=True))
        o_ref[r] = jnp.sum(e * v_ref[r], axis=0) / jnp.sum(e, axis=0)


def _prompt_kernel(x_ref, k_ref, v_ref, gin_ref, win_hbm, poolw_ref, pscale_ref, sgug_ref,
                   sguw_ref, sgub_ref, wdown_hbm, wout_hbm, gf_ref, xs_ref, hwin_ref, sk_ref, sv_ref,
                   y_ref, hist_ref, so_ref, pin_hbm, vn_hbm, opool_hbm, osgu_hbm, az_hbm, g0_hbm, g1_hbm, g2_hbm,
                   ext_ref, win_ref, wdown_ref, wout_ref, stage_ref, stage_sem, qs_ref, ostage_ref, ostage_sem):
    i = pl.program_id(1)
    ts = x_ref.shape[1]
    step = pl.program_id(0) * pl.num_programs(1) + i

    @pl.when(step == 0)
    def _():
        _first_step(xs_ref, hwin_ref, gin_ref, poolw_ref, pscale_ref, sgug_ref, sguw_ref, sgub_ref,
                    win_hbm, wdown_hbm, wout_hbm, win_ref, wdown_ref, wout_ref, stage_ref, stage_sem,
                    qs_ref, ostage_ref, ostage_sem, pin_hbm, vn_hbm, opool_hbm, osgu_hbm, az_hbm,
                    (g0_hbm, g1_hbm, g2_hbm))

    @pl.when(i == 0)
    def _():
        ext_ref[0:HIST_ROWS, :] = jnp.zeros((HIST_ROWS, BRANCH_W), F32)

    q_rows = sk_ref.shape[0] * KV_ROWS
    q_blk = qs_ref[pl.ds(pl.multiple_of(step * q_rows, q_rows), q_rows), :]
    _sample_attention(lambda r: q_blk[r * KV_ROWS:(r + 1) * KV_ROWS, :], sk_ref, sv_ref, so_ref)

    x = x_ref[0]
    hb = _rmsnorm(x, gin_ref[...]).astype(BF16)

    def proj(c):
        return _dot(hb, win_ref[:, c * BRANCH_W:(c + 1) * BRANCH_W])


    p_in = proj(0)
    p_z = proj(1)
    ext_ref[HIST_ROWS:HIST_ROWS + ts, :] = p_in
    pos = i * ts + lax.broadcasted_iota(jnp.int32, (ts, 1), 0)
    pooled = []
    for g, w in enumerate(POOL_WINDOWS):
        sl = slice(g * POOL_GW, (g + 1) * POOL_GW)
        win = ext_ref[:, sl]
        span = 1
        while span < w:
            win = win + pltpu.roll(win, span, axis=0)
            span *= 2
        inv_cnt = 1.0 / jnp.minimum(w, pos + 1).astype(F32)
        pooled.append((win[HIST_ROWS:, :] * inv_cnt - p_in[:, sl]).astype(BF16))
    ext_ref[0:HIST_ROWS, :] = p_in[ts - HIST_ROWS:, :]
    hist_ref[0] = p_in[ts - HIST_ROWS:, :]

    v = proj(3)
    u = proj(2)
    s_z = proj(4)
    mixed = [_dot(pooled[g], poolw_ref[g].astype(BF16)) for g in range(POOL_GROUPS)]
    o_pool = (jnp.concatenate(mixed, axis=1) * pscale_ref[...] * _silu(p_z)).astype(BF16)

    vnb = _rmsnorm(v, sgug_ref[...]).astype(BF16)
    tril = (lax.broadcasted_iota(jnp.int32, (SGU_CHUNK, SGU_CHUNK), 0)
            >= lax.broadcasted_iota(jnp.int32, (SGU_CHUNK, SGU_CHUNK), 1))
    ws = [jnp.where(tril, sguw_ref[g], 0.0).astype(BF16) for g in range(SGU_GROUPS)]
    q = proj(5)
    a_z = proj(6)
    rows = []
    for c in range(ts // SGU_CHUNK):
        rs = slice(c * SGU_CHUNK, (c + 1) * SGU_CHUNK)
        cols = [_dot(ws[g], vnb[rs, g * SGU_GW:(g + 1) * SGU_GW]) + sgub_ref[:, g:g + 1]
                for g in range(SGU_GROUPS)]
        rows.append(jnp.concatenate(cols, axis=1))
    o_sgu = (u * jnp.concatenate(rows, axis=0) * _silu(s_z)).astype(BF16)

    qb = q.astype(BF16)
    scores = [lax.dot_general(qb[:, hd * XA_HD:(hd + 1) * XA_HD], k_ref[0, :, hd * XA_HD:(hd + 1) * XA_HD],
                              (((1,), (1,)), ((), ())), preferred_element_type=F32) * (XA_HD ** -0.5 * LOG2E)
              for hd in range(XA_HEADS)]
    gates = [proj(7 + n) for n in range(N_BRANCH)]
    heads = []
    for hd in range(XA_HEADS):
        s = scores[hd]
        e = jnp.exp2(s - jnp.max(s, axis=-1, keepdims=True))
        pr = e * (1.0 / jnp.sum(e, axis=-1, keepdims=True))
        heads.append(_dot(pr.astype(BF16), v_ref[0, :, hd * XA_HD:(hd + 1) * XA_HD]))
    o_xa = (jnp.concatenate(heads, axis=1) * _silu(a_z)).astype(BF16)

    merged = None
    for n, o in enumerate((o_pool, o_sgu, o_xa)):
        t = _sigmoid(gates[n]) * _dot(o, wdown_ref[n])
        merged = t if merged is None else merged + t
    xn = x + _dot(merged.astype(BF16), wout_ref[...])
    y_ref[0] = _rmsnorm(xn, gf_ref[...])


def _prompt_layer(x, kb, vb, gin, win, poolw, pscale, sgug, sguw, sgub_t, wdown, wout, gf,
                  xs_tiles, hwin, sk, sv):
    nb, seq, _ = x.shape
    ts = SEQ_TILE
    n_tiles = seq // ts
    ns = sk.shape[0]
    rb = ns // (nb * n_tiles)
    assert rb * nb * n_tiles == ns
    sample_row = jax.ShapeDtypeStruct((ns, BRANCH_W), F32)
    in_cols = win.shape[1]
    assert in_cols % W_STAGE_COLS == 0 and D_MODEL % W_STAGE_COLS == 0 and D_MODEL % W_STAGE_ROWS == 0
    step = lambda b, i: b * n_tiles + i
    hbm = lambda: pl.BlockSpec(memory_space=pl.ANY)
    return pl.pallas_call(
        _prompt_kernel,
        grid=(nb, n_tiles),
        in_specs=[pl.BlockSpec((1, ts, D_MODEL), lambda b, i: (b, i, 0)),
                  pl.BlockSpec((1, N_MEM, BRANCH_W), lambda b, i: (b, 0, 0)),
                  pl.BlockSpec((1, N_MEM, BRANCH_W), lambda b, i: (b, 0, 0)),
                  _resident((1, D_MODEL)),
                  hbm(),
                  _resident((POOL_GROUPS, POOL_GW, POOL_GW)),
                  _resident((1, BRANCH_W)),
                  _resident((1, BRANCH_W)),
                  _resident((SGU_GROUPS, SGU_CHUNK, SGU_CHUNK)),
                  _resident((SGU_CHUNK, SGU_GROUPS)),
                  hbm(),
                  hbm(),
                  _resident((1, D_MODEL)),
                  _resident((ns * ROW_TILES, LANES)),
                  _resident((ns, BRANCH_W)),
                  pl.BlockSpec((rb, N_MEM, KV_ROWS, LANES), lambda b, i: (step(b, i), 0, 0, 0)),
                  pl.BlockSpec((rb, N_MEM, KV_ROWS, LANES), lambda b, i: (step(b, i), 0, 0, 0))],
        out_specs=[pl.BlockSpec((1, ts, D_MODEL), lambda b, i: (b, i, 0)),
                   pl.BlockSpec((1, HIST_ROWS, BRANCH_W), lambda b, i: (b, 0, 0)),
                   pl.BlockSpec((rb, KV_ROWS, LANES), lambda b, i: (step(b, i), 0, 0))] + [hbm()] * 8,
        out_shape=[jax.ShapeDtypeStruct((nb, seq, D_MODEL), F32),
                   jax.ShapeDtypeStruct((nb, HIST_ROWS, BRANCH_W), F32),
                   jax.ShapeDtypeStruct((ns, KV_ROWS, LANES), F32)] + [sample_row] * 8,
        scratch_shapes=[pltpu.VMEM((HIST_ROWS + ts, BRANCH_W), F32),
                        pltpu.VMEM((D_MODEL, in_cols), BF16),
                        pltpu.VMEM((N_BRANCH, BRANCH_W, D_MODEL), BF16),
                        pltpu.VMEM((D_MODEL, D_MODEL), BF16),
                        pltpu.VMEM((W_STAGE_SLOTS, W_STAGE_ROWS, W_STAGE_COLS), F32),
                        pltpu.SemaphoreType.DMA((W_STAGE_SLOTS,)),
                        pltpu.VMEM((ns * KV_ROWS, LANES), F32),
                        pltpu.VMEM((2, ns, BRANCH_W), F32),
                        pltpu.SemaphoreType.DMA((2,))],
        compiler_params=pltpu.CompilerParams(dimension_semantics=("arbitrary", "arbitrary"),
                                             vmem_limit_bytes=V7X_VMEM_LIMIT_BYTES),
        name="prompt_layer",
    )(x, kb, vb, gin, win, poolw, pscale, sgug, sguw, sgub_t, wdown, wout, gf, xs_tiles, hwin, sk, sv)


def _prep_kernel(hist_ref, mem_ref, gmem_ref, wkv_ref,
                 newhist_ref, hwin_ref, k_ref, v_ref, kb_ref, vb_ref, wkv_s):
    @pl.when(pl.program_id(0) == 0)
    def _():
        wkv_s[...] = wkv_ref[...].astype(BF16)
        newhist_ref[0:POOL_HIST - 1] = hist_ref[1:POOL_HIST]
        newhist_ref[POOL_HIST - 1] = jnp.zeros(newhist_ref.shape[1:], F32)
        for g, w in enumerate(POOL_WINDOWS):
            sl = slice(g * POOL_GW, (g + 1) * POOL_GW)
            win = hist_ref[POOL_HIST - 1, :, sl]
            for j in range(2, w):
                win = win + hist_ref[POOL_HIST - j, :, sl]
            hwin_ref[:, sl] = win

    _mem_kv_body(mem_ref, gmem_ref, wkv_s, k_ref, v_ref, kb_ref, vb_ref)


def _prep(hist, mem, gmem, wkv):
    n = hist.shape[1]
    nb = mem.shape[0]
    rows_blk = lambda: pl.BlockSpec((1, N_MEM * KV_ROWS, LANES), lambda b: (b, 0, 0))
    flat_blk = lambda: pl.BlockSpec((1, N_MEM, BRANCH_W), lambda b: (b, 0, 0))
    return pl.pallas_call(
        _prep_kernel,
        grid=(nb,),
        in_specs=[_resident(hist.shape),
                  pl.BlockSpec((1, N_MEM, D_MODEL), lambda b: (b, 0, 0)),
                  _resident((1, D_MODEL)),
                  _resident((D_MODEL, 2 * BRANCH_W))],
        out_specs=[_resident(hist.shape), _resident((n, BRANCH_W)),
                   rows_blk(), rows_blk(), flat_blk(), flat_blk()],
        out_shape=[jax.ShapeDtypeStruct(hist.shape, F32),
                   jax.ShapeDtypeStruct((n, BRANCH_W), F32),
                   jax.ShapeDtypeStruct((nb, N_MEM * KV_ROWS, LANES), F32),
                   jax.ShapeDtypeStruct((nb, N_MEM * KV_ROWS, LANES), F32),
                   jax.ShapeDtypeStruct((nb, N_MEM, BRANCH_W), BF16),
                   jax.ShapeDtypeStruct((nb, N_MEM, BRANCH_W), BF16)],
        scratch_shapes=[pltpu.VMEM((D_MODEL, 2 * BRANCH_W), BF16)],
        compiler_params=pltpu.CompilerParams(dimension_semantics=("arbitrary",),
                                             vmem_limit_bytes=V7X_VMEM_LIMIT_BYTES),
        name="prep",
    )(hist, mem, gmem, wkv)


def _sample_post_kernel(x_ref, opool_ref, osgu_ref, attn_ref, az_ref, g0_ref, g1_ref, g2_ref, wdown_ref,
                        wout_ref, gf_ref, y_ref):
    o_xa = _load_row_tiles(attn_ref, KV_ROW_ORDER) * _silu(az_ref[...])
    merged = None
    for n, (o, g_ref) in enumerate(((opool_ref[...], g0_ref), (osgu_ref[...], g1_ref), (o_xa, g2_ref))):
        gate = _sigmoid(g_ref[...])
        t = gate * _dot(o.astype(BF16), wdown_ref[n].astype(BF16))
        merged = t if merged is None else merged + t
    xn = _load_row_tiles(x_ref) + _dot(merged.astype(BF16), wout_ref[...].astype(BF16))
    _store_row_tiles(y_ref, _rmsnorm(xn, gf_ref[...]))


def _sample_post(x_tiles, o_pool, o_sgu, attn_tiles, a_z, gates, wdown, wout, gf):
    return pl.pallas_call(
        _sample_post_kernel,
        out_shape=jax.ShapeDtypeStruct(x_tiles.shape, F32),
        compiler_params=pltpu.CompilerParams(vmem_limit_bytes=V7X_VMEM_LIMIT_BYTES),
        name="sample_post",
    )(x_tiles, o_pool, o_sgu, attn_tiles, a_z, *gates, wdown, wout, gf)


def kernel(x_prompt, x_sample, state_pool, cache_mem_k, cache_mem_v, mem_prompt, norm_in_g, w_in,
           pool_w, pool_scale, sgu_norm_g, sgu_w, sgu_b, mem_norm_g, w_kv, w_down, w_out, norm_f_g):
    depth = w_in.shape[0]
    assert depth == 1, "single-layer step"
    nb, seq, _ = x_prompt.shape
    ns, dec_seq, _ = x_sample.shape
    assert dec_seq == 1 and seq % SEQ_TILE == 0 and seq >= HIST_ROWS

    row = lambda a: a.reshape(1, -1)
    gin, pscale, sgug, gmem, gf = (row(norm_in_g[0]), row(pool_scale[0]), row(sgu_norm_g[0]),
                                   row(mem_norm_g[0]), row(norm_f_g))
    win, poolw, wkv, wdown, wout, sguw, sgub = (w_in[0], pool_w[0], w_kv[0], w_down[0], w_out[0],
                                                sgu_w[0], sgu_b[0])
    sgub_t = sgub.T

    x_tiles = x_sample.reshape(ns * ROW_TILES, LANES)
    hist_s = jnp.transpose(state_pool[0], (1, 0, 2))
    new_hist_s, hwin_s, k_rows, v_rows, kb, vb = _prep(hist_s, mem_prompt, gmem, wkv)

    (y_prompt, hist_p, attn_rows,
     p_in_s, vn_s, o_pool_s, o_sgu_s, az_s, g0_s, g1_s, g2_s) = _prompt_layer(
        x_prompt, kb, vb, gin, win, poolw, pscale, sgug, sguw, sgub_t, wdown, wout, gf, x_tiles, hwin_s,
        _to_kv_rows(cache_mem_k[0].reshape(ns, N_MEM, BRANCH_W)),
        _to_kv_rows(cache_mem_v[0].reshape(ns, N_MEM, BRANCH_W)))

    y_tiles = _sample_post(x_tiles, o_pool_s, o_sgu_s, attn_rows.reshape(ns * KV_ROWS, LANES), az_s,
                           (g0_s, g1_s, g2_s), wdown, wout, gf)

    new_pool_p = hist_p[None, :, HIST_ROWS - POOL_HIST:, :]
    new_hist_s = lax.dynamic_update_slice(new_hist_s, p_in_s[None], (POOL_HIST - 1, 0, 0))
    new_pool_s = jnp.transpose(new_hist_s, (1, 0, 2))[None]
    kv_out = lambda a: _from_kv_rows(a.reshape(nb, N_MEM, KV_ROWS, LANES))[None]
    return (y_prompt, y_tiles.reshape(ns, 1, D_MODEL), new_pool_p, new_pool_s,
            kv_out(k_rows), kv_out(v_rows), vn_s.reshape(1, ns, 1, BRANCH_W))
```

```python
import jax
import jax.numpy as jnp
from jax import lax
from jax.experimental import pallas as pl
from jax.experimental.pallas import tpu as pltpu

D_MODEL = 1024
BRANCH_W = 1024
N_BRANCH = 3
N_PROJ = 7 + N_BRANCH
POOL_WINDOWS = (2, 4, 8, 16)
POOL_GROUPS = len(POOL_WINDOWS)
POOL_GW = BRANCH_W // POOL_GROUPS
POOL_HIST = max(POOL_WINDOWS) - 1
HIST_ROWS = POOL_HIST + 1
SGU_CHUNK = 128
SGU_GROUPS = 4
SGU_GW = BRANCH_W // SGU_GROUPS
N_MEM = 256
XA_HEADS = 4
XA_HD = BRANCH_W // XA_HEADS
EPS = 1e-6
PAST_LEN = 16384

SEQ_TILE = 256
LANES = 128
XA_LANE_TILES = XA_HD // LANES
KV_ROWS = XA_HEADS * XA_LANE_TILES
LOG2E = 1.4426950408889634
W_STAGE_SLOTS = 4
W_STAGE_ROWS = 256
W_STAGE_COLS = 1024
V7X_VMEM_LIMIT_BYTES = 62 * 1024 * 1024

F32 = jnp.float32
BF16 = jnp.bfloat16

_sigmoid = jax.nn.sigmoid


def _rmsnorm(x, g):
    return x * lax.rsqrt(jnp.mean(x * x, axis=-1, keepdims=True) + EPS) * g


def _silu(z):
    return z * _sigmoid(z)


def _dot(a, b):
    return jnp.dot(a, b, preferred_element_type=F32)


def _resident(shape):
    zeros = (0,) * len(shape)
    return pl.BlockSpec(shape, lambda *_: zeros, pipeline_mode=pl.Buffered(1))


def _to_kv_rows(a):
    lead = a.shape[:-1]
    a = a.reshape(*lead, XA_HEADS, XA_LANE_TILES, LANES)
    return jnp.swapaxes(a, -3, -2).reshape(*lead, KV_ROWS, LANES)


ROW_TILES = D_MODEL // LANES
NATURAL_ORDER = tuple(range(ROW_TILES))
KV_ROW_ORDER = tuple((j % XA_HEADS) * XA_LANE_TILES + j // XA_HEADS for j in range(KV_ROWS))


def _load_row_tiles(ref, order=NATURAL_ORDER):
    n = ref.shape[0] // len(order)
    tiles = [None] * len(order)
    for j, t in enumerate(order):
        tiles[t] = ref[pl.ds(j, n, stride=len(order)), :]
    return jnp.concatenate(tiles, axis=1)


def _store_row_tiles(ref, val, order=NATURAL_ORDER):
    n = val.shape[0]
    for j, t in enumerate(order):
        ref[pl.ds(j, n, stride=len(order)), :] = val[:, t * LANES:(t + 1) * LANES]


def _from_kv_rows(a):
    lead = a.shape[:-2]
    a = a.reshape(*lead, XA_LANE_TILES, XA_HEADS, LANES)
    return jnp.swapaxes(a, -3, -2).reshape(*lead, XA_HEADS, XA_HD)


def _mem_kv_body(mem_ref, g_ref, wkv_s, k_ref, v_ref, kb_ref, vb_ref):
    m = _rmsnorm(mem_ref[0], g_ref[...]).astype(BF16)
    kv = _dot(m, wkv_s[...])
    k = kv[:, :BRANCH_W]
    v = kv[:, BRANCH_W:]
    kb_ref[0] = k.astype(BF16)
    vb_ref[0] = v.astype(BF16)
    for h in range(XA_HEADS):
        for lt in range(XA_LANE_TILES):
            cols = slice(h * XA_HD + lt * LANES, h * XA_HD + (lt + 1) * LANES)
            rows = pl.ds(lt * XA_HEADS + h, N_MEM, stride=KV_ROWS)
            k_ref[0, rows, :] = k[:, cols]
            v_ref[0, rows, :] = v[:, cols]


def _stage_weights(copies, stage_ref, sem_ref, after_chunk):
    slots = stage_ref.shape[0]
    ahead = slots - 1

    def chunk_copy(k):
        return pltpu.make_async_copy(copies[k][0], stage_ref.at[k % slots], sem_ref.at[k % slots])

    for k in range(min(ahead, len(copies))):
        chunk_copy(k).start()
    for k, (_, dst) in enumerate(copies):
        if k + ahead < len(copies):
            chunk_copy(k + ahead).start()
        chunk_copy(k).wait()
        dst[...] = stage_ref[k % slots].astype(BF16)
        after_chunk(k)


class _HbmWriter:
    def __init__(self, stage_ref, sem_ref):
        self.stage_ref, self.sem_ref, self.pending, self.count = stage_ref, sem_ref, [], 0

    def write(self, value, dst_hbm):
        slots = self.stage_ref.shape[0]
        slot = self.count % slots
        if len(self.pending) == slots:
            self.pending.pop(0).wait()
        self.stage_ref[slot] = value
        copy = pltpu.make_async_copy(self.stage_ref.at[slot], dst_hbm, self.sem_ref.at[slot])
        copy.start()
        self.pending.append(copy)
        self.count += 1

    def finish(self):
        for copy in self.pending:
            copy.wait()
        self.pending = []


def _first_step(xs_ref, hwin_ref, gin_ref, poolw_ref, pscale_ref, sgug_ref, sguw_ref, sgub_ref,
                win_hbm, wdown_hbm, wout_hbm, win_ref, wdown_ref, wout_ref, stage_ref, stage_sem,
                qs_ref, ostage_ref, ostage_sem, pin_hbm, vn_hbm, opool_hbm, osgu_hbm, az_hbm, gate_hbms):
    _, sr, sc = stage_ref.shape
    assert sc == BRANCH_W and D_MODEL % sr == 0
    k_slabs = D_MODEL // sr
    tiles = lambda ref: [(pl.ds(r * sr, sr), pl.ds(c * sc, sc))
                         for c in range(ref.shape[-1] // sc) for r in range(ref.shape[-2] // sr)]
    copies = [(win_hbm.at[r, c], win_ref.at[r, c]) for r, c in tiles(win_ref)]
    n_win = len(copies)
    copies += [(wdown_hbm.at[n, r, c], wdown_ref.at[n, r, c]) for n in range(N_BRANCH) for r, c in tiles(wdown_ref)]
    copies += [(wout_hbm.at[r, c], wout_ref.at[r, c]) for r, c in tiles(wout_ref)]

    hbs = _rmsnorm(_load_row_tiles(xs_ref), gin_ref[...]).astype(BF16)
    writer = _HbmWriter(ostage_ref, ostage_sem)
    kept = {}

    def finish_chunk(c, val):
        if c == 0:
            kept["p_in"] = val
            writer.write(val, pin_hbm)
        elif c == 1:
            p_in = kept["p_in"]
            mixed = []
            for g, w in enumerate(POOL_WINDOWS):
                sl = slice(g * POOL_GW, (g + 1) * POOL_GW)
                d = (p_in[:, sl] + hwin_ref[:, sl]) / float(min(w, PAST_LEN + 1)) - p_in[:, sl]
                mixed.append(_dot(d.astype(BF16), poolw_ref[g].astype(BF16)))
            writer.write(jnp.concatenate(mixed, axis=1) * pscale_ref[...] * _silu(val), opool_hbm)
        elif c == 2:
            kept["u"] = val
        elif c == 3:
            vn = _rmsnorm(val, sgug_ref[...])
            writer.write(vn, vn_hbm)
            gated = [vn[:, g * SGU_GW:(g + 1) * SGU_GW] * sguw_ref[g, 0:1, 0:1] + sgub_ref[0:1, g:g + 1]
                     for g in range(SGU_GROUPS)]
            kept["u_gated"] = kept["u"] * jnp.concatenate(gated, axis=1)
        elif c == 4:
            writer.write(kept["u_gated"] * _silu(val), osgu_hbm)
        elif c == 5:
            _store_row_tiles(qs_ref, val, KV_ROW_ORDER)
        elif c == 6:
            writer.write(val, az_hbm)
        else:
            writer.write(val, gate_hbms[c - 7])

    def after_chunk(k):
        if k >= n_win:
            return
        c, r = divmod(k, k_slabs)
        part = _dot(hbs[:, r * sr:(r + 1) * sr], win_ref[r * sr:(r + 1) * sr, c * sc:(c + 1) * sc])
        kept["acc"] = part if r == 0 else kept["acc"] + part
        if r == k_slabs - 1:
            finish_chunk(c, kept["acc"])

    _stage_weights(copies, stage_ref, stage_sem, after_chunk)
    writer.finish()


def _last_step(xs_ref, attn_ref, gf_ref, wdown_ref, wout_ref, opool_hbm, osgu_hbm, az_hbm, gate_hbms,
               stage_ref, last_sem, tiles_ref, ys_hbm):
    ns = xs_ref.shape[0] // ROW_TILES
    srcs = (opool_hbm, osgu_hbm, az_hbm) + tuple(gate_hbms)
    per_slot = stage_ref.shape[1] // ns
    bufs = [stage_ref.at[k // per_slot, pl.ds((k % per_slot) * ns, ns), :] for k in range(len(srcs))]
    copies = [pltpu.make_async_copy(src, buf, last_sem.at[k]) for k, (src, buf) in enumerate(zip(srcs, bufs))]
    for copy in copies:
        copy.start()
    for copy in copies:
        copy.wait()
    o_pool, o_sgu, a_z, *gates = [buf[...] for buf in bufs]
    o_xa = _load_row_tiles(attn_ref, KV_ROW_ORDER) * _silu(a_z)
    merged = None
    for n, o in enumerate((o_pool, o_sgu, o_xa)):
        t = _sigmoid(gates[n]) * _dot(o.astype(BF16), wdown_ref[n])
        merged = t if merged is None else merged + t
    xn = _load_row_tiles(xs_ref) + _dot(merged.astype(BF16), wout_ref[...])
    _store_row_tiles(tiles_ref, _rmsnorm(xn, gf_ref[...]))
    out = pltpu.make_async_copy(tiles_ref, ys_hbm, last_sem.at[0])
    out.start()
    out.wait()


def _sample_attention(get_q, k_ref, v_ref, put_o):
    for r in range(k_ref.shape[0]):
        q = get_q(r) * (XA_HD ** -0.5 * LOG2E)
        part = k_ref[r] * q
        part = part + pltpu.roll(part, XA_HEADS, axis=1)
        s = jnp.sum(part, axis=-1, keepdims=True)
        e = jnp.exp2(s - jnp.max(s, axis=0, keepdims=True))
        put_o(r, jnp.sum(e * v_ref[r], axis=0) / jnp.sum(e, axis=0))


def _prompt_kernel(x_ref, k_ref, v_ref, gin_ref, win_hbm, poolw_ref, pscale_ref, sgug_ref,
                   sguw_ref, sgub_ref, wdown_hbm, wout_hbm, gf_ref, xs_ref, hwin_ref, sk_ref, sv_ref,
                   y_ref, hist_ref, pin_hbm, vn_hbm, opool_hbm, osgu_hbm, az_hbm, g0_hbm, g1_hbm, g2_hbm, ys_hbm,
                   ext_ref, win_ref, wdown_ref, wout_ref, stage_ref, stage_sem, qs_ref, attn_ref,
                   ostage_ref, ostage_sem, last_sem):
    i = pl.program_id(1)
    ts = x_ref.shape[1]
    step = pl.program_id(0) * pl.num_programs(1) + i

    @pl.when(step == 0)
    def _():
        _first_step(xs_ref, hwin_ref, gin_ref, poolw_ref, pscale_ref, sgug_ref, sguw_ref, sgub_ref,
                    win_hbm, wdown_hbm, wout_hbm, win_ref, wdown_ref, wout_ref, stage_ref, stage_sem,
                    qs_ref, ostage_ref, ostage_sem, pin_hbm, vn_hbm, opool_hbm, osgu_hbm, az_hbm,
                    (g0_hbm, g1_hbm, g2_hbm))

    @pl.when(i == 0)
    def _():
        ext_ref[0:HIST_ROWS, :] = jnp.zeros((HIST_ROWS, BRANCH_W), F32)

    q_rows = sk_ref.shape[0] * KV_ROWS
    q_blk = qs_ref[pl.ds(pl.multiple_of(step * q_rows, q_rows), q_rows), :]

    def put_attention(r, out):
        attn_ref[pl.ds(pl.multiple_of(step * q_rows + r * KV_ROWS, KV_ROWS), KV_ROWS), :] = out

    _sample_attention(lambda r: q_blk[r * KV_ROWS:(r + 1) * KV_ROWS, :], sk_ref, sv_ref, put_attention)

    x = x_ref[0]
    hb = _rmsnorm(x, gin_ref[...]).astype(BF16)

    def proj(c):
        return _dot(hb, win_ref[:, c * BRANCH_W:(c + 1) * BRANCH_W])


    p_in = proj(0)
    p_z = proj(1)
    ext_ref[HIST_ROWS:HIST_ROWS + ts, :] = p_in
    pos = i * ts + lax.broadcasted_iota(jnp.int32, (ts, 1), 0)
    pooled = []
    for g, w in enumerate(POOL_WINDOWS):
        sl = slice(g * POOL_GW, (g + 1) * POOL_GW)
        win = ext_ref[:, sl]
        span = 1
        while span < w:
            win = win + pltpu.roll(win, span, axis=0)
            span *= 2
        inv_cnt = 1.0 / jnp.minimum(w, pos + 1).astype(F32)
        pooled.append((win[HIST_ROWS:, :] * inv_cnt - p_in[:, sl]).astype(BF16))
    ext_ref[0:HIST_ROWS, :] = p_in[ts - HIST_ROWS:, :]
    hist_ref[0] = p_in[ts - HIST_ROWS:, :]

    v = proj(3)
    u = proj(2)
    s_z = proj(4)
    mixed = [_dot(pooled[g], poolw_ref[g].astype(BF16)) for g in range(POOL_GROUPS)]
    o_pool = (jnp.concatenate(mixed, axis=1) * pscale_ref[...] * _silu(p_z)).astype(BF16)

    vnb = _rmsnorm(v, sgug_ref[...]).astype(BF16)
    tril = (lax.broadcasted_iota(jnp.int32, (SGU_CHUNK, SGU_CHUNK), 0)
            >= lax.broadcasted_iota(jnp.int32, (SGU_CHUNK, SGU_CHUNK), 1))
    ws = [jnp.where(tril, sguw_ref[g], 0.0).astype(BF16) for g in range(SGU_GROUPS)]
    q = proj(5)
    a_z = proj(6)
    rows = []
    for c in range(ts // SGU_CHUNK):
        rs = slice(c * SGU_CHUNK, (c + 1) * SGU_CHUNK)
        cols = [_dot(ws[g], vnb[rs, g * SGU_GW:(g + 1) * SGU_GW]) + sgub_ref[:, g:g + 1]
                for g in range(SGU_GROUPS)]
        rows.append(jnp.concatenate(cols, axis=1))
    o_sgu = (u * jnp.concatenate(rows, axis=0) * _silu(s_z)).astype(BF16)

    qb = q.astype(BF16)
    scores = [lax.dot_general(qb[:, hd * XA_HD:(hd + 1) * XA_HD], k_ref[0, :, hd * XA_HD:(hd + 1) * XA_HD],
                              (((1,), (1,)), ((), ())), preferred_element_type=F32) * (XA_HD ** -0.5 * LOG2E)
              for hd in range(XA_HEADS)]
    gates = [proj(7 + n) for n in range(N_BRANCH)]
    heads = []
    for hd in range(XA_HEADS):
        s = scores[hd]
        e = jnp.exp2(s - jnp.max(s, axis=-1, keepdims=True))
        pr = e * (1.0 / jnp.sum(e, axis=-1, keepdims=True))
        heads.append(_dot(pr.astype(BF16), v_ref[0, :, hd * XA_HD:(hd + 1) * XA_HD]))
    o_xa = (jnp.concatenate(heads, axis=1) * _silu(a_z)).astype(BF16)

    merged = None
    for n, o in enumerate((o_pool, o_sgu, o_xa)):
        t = _sigmoid(gates[n]) * _dot(o, wdown_ref[n])
        merged = t if merged is None else merged + t
    xn = x + _dot(merged.astype(BF16), wout_ref[...])
    y_ref[0] = _rmsnorm(xn, gf_ref[...])

    @pl.when(step == pl.num_programs(0) * pl.num_programs(1) - 1)
    def _():
        _last_step(xs_ref, attn_ref, gf_ref, wdown_ref, wout_ref, opool_hbm, osgu_hbm, az_hbm,
                   (g0_hbm, g1_hbm, g2_hbm), stage_ref, last_sem, qs_ref, ys_hbm)


def _prompt_layer(x, kb, vb, gin, win, poolw, pscale, sgug, sguw, sgub_t, wdown, wout, gf,
                  xs_tiles, hwin, sk, sv):
    nb, seq, _ = x.shape
    ts = SEQ_TILE
    n_tiles = seq // ts
    ns = sk.shape[0]
    rb = ns // (nb * n_tiles)
    assert rb * nb * n_tiles == ns
    sample_row = jax.ShapeDtypeStruct((ns, BRANCH_W), F32)
    in_cols = win.shape[1]
    assert in_cols % W_STAGE_COLS == 0 and D_MODEL % W_STAGE_COLS == 0 and D_MODEL % W_STAGE_ROWS == 0
    step = lambda b, i: b * n_tiles + i
    hbm = lambda: pl.BlockSpec(memory_space=pl.ANY)
    return pl.pallas_call(
        _prompt_kernel,
        grid=(nb, n_tiles),
        in_specs=[pl.BlockSpec((1, ts, D_MODEL), lambda b, i: (b, i, 0)),
                  pl.BlockSpec((1, N_MEM, BRANCH_W), lambda b, i: (b, 0, 0)),
                  pl.BlockSpec((1, N_MEM, BRANCH_W), lambda b, i: (b, 0, 0)),
                  _resident((1, D_MODEL)),
                  hbm(),
                  _resident((POOL_GROUPS, POOL_GW, POOL_GW)),
                  _resident((1, BRANCH_W)),
                  _resident((1, BRANCH_W)),
                  _resident((SGU_GROUPS, SGU_CHUNK, SGU_CHUNK)),
                  _resident((SGU_CHUNK, SGU_GROUPS)),
                  hbm(),
                  hbm(),
                  _resident((1, D_MODEL)),
                  _resident((ns * ROW_TILES, LANES)),
                  _resident((ns, BRANCH_W)),
                  pl.BlockSpec((rb, N_MEM, KV_ROWS, LANES), lambda b, i: (step(b, i), 0, 0, 0)),
                  pl.BlockSpec((rb, N_MEM, KV_ROWS, LANES), lambda b, i: (step(b, i), 0, 0, 0))],
        out_specs=[pl.BlockSpec((1, ts, D_MODEL), lambda b, i: (b, i, 0)),
                   pl.BlockSpec((1, HIST_ROWS, BRANCH_W), lambda b, i: (b, 0, 0))] + [hbm()] * 9,
        out_shape=[jax.ShapeDtypeStruct((nb, seq, D_MODEL), F32),
                   jax.ShapeDtypeStruct((nb, HIST_ROWS, BRANCH_W), F32)] + [sample_row] * 8
                  + [jax.ShapeDtypeStruct((ns * ROW_TILES, LANES), F32)],
        scratch_shapes=[pltpu.VMEM((HIST_ROWS + ts, BRANCH_W), F32),
                        pltpu.VMEM((D_MODEL, in_cols), BF16),
                        pltpu.VMEM((N_BRANCH, BRANCH_W, D_MODEL), BF16),
                        pltpu.VMEM((D_MODEL, D_MODEL), BF16),
                        pltpu.VMEM((W_STAGE_SLOTS, W_STAGE_ROWS, W_STAGE_COLS), F32),
                        pltpu.SemaphoreType.DMA((W_STAGE_SLOTS,)),
                        pltpu.VMEM((ns * KV_ROWS, LANES), F32),
                        pltpu.VMEM((ns * KV_ROWS, LANES), F32),
                        pltpu.VMEM((2, ns, BRANCH_W), F32),
                        pltpu.SemaphoreType.DMA((2,)),
                        pltpu.SemaphoreType.DMA((2 * N_BRANCH,))],
        compiler_params=pltpu.CompilerParams(dimension_semantics=("arbitrary", "arbitrary"),
                                             vmem_limit_bytes=V7X_VMEM_LIMIT_BYTES),
        name="prompt_layer",
    )(x, kb, vb, gin, win, poolw, pscale, sgug, sguw, sgub_t, wdown, wout, gf, xs_tiles, hwin, sk, sv)


def _prep_kernel(hist_ref, mem_ref, gmem_ref, wkv_ref,
                 newhist_ref, hwin_ref, k_ref, v_ref, kb_ref, vb_ref, wkv_s):
    @pl.when(pl.program_id(0) == 0)
    def _():
        wkv_s[...] = wkv_ref[...].astype(BF16)
        newhist_ref[0:POOL_HIST - 1] = hist_ref[1:POOL_HIST]
        newhist_ref[POOL_HIST - 1] = jnp.zeros(newhist_ref.shape[1:], F32)
        for g, w in enumerate(POOL_WINDOWS):
            sl = slice(g * POOL_GW, (g + 1) * POOL_GW)
            win = hist_ref[POOL_HIST - 1, :, sl]
            for j in range(2, w):
                win = win + hist_ref[POOL_HIST - j, :, sl]
            hwin_ref[:, sl] = win

    _mem_kv_body(mem_ref, gmem_ref, wkv_s, k_ref, v_ref, kb_ref, vb_ref)


def _prep(hist, mem, gmem, wkv):
    n = hist.shape[1]
    nb = mem.shape[0]
    rows_blk = lambda: pl.BlockSpec((1, N_MEM * KV_ROWS, LANES), lambda b: (b, 0, 0))
    flat_blk = lambda: pl.BlockSpec((1, N_MEM, BRANCH_W), lambda b: (b, 0, 0))
    return pl.pallas_call(
        _prep_kernel,
        grid=(nb,),
        in_specs=[_resident(hist.shape),
                  pl.BlockSpec((1, N_MEM, D_MODEL), lambda b: (b, 0, 0)),
                  _resident((1, D_MODEL)),
                  _resident((D_MODEL, 2 * BRANCH_W))],
        out_specs=[_resident(hist.shape), _resident((n, BRANCH_W)),
                   rows_blk(), rows_blk(), flat_blk(), flat_blk()],
        out_shape=[jax.ShapeDtypeStruct(hist.shape, F32),
                   jax.ShapeDtypeStruct((n, BRANCH_W), F32),
                   jax.ShapeDtypeStruct((nb, N_MEM * KV_ROWS, LANES), F32),
                   jax.ShapeDtypeStruct((nb, N_MEM * KV_ROWS, LANES), F32),
                   jax.ShapeDtypeStruct((nb, N_MEM, BRANCH_W), BF16),
                   jax.ShapeDtypeStruct((nb, N_MEM, BRANCH_W), BF16)],
        scratch_shapes=[pltpu.VMEM((D_MODEL, 2 * BRANCH_W), BF16)],
        compiler_params=pltpu.CompilerParams(dimension_semantics=("arbitrary",),
                                             vmem_limit_bytes=V7X_VMEM_LIMIT_BYTES),
        name="prep",
    )(hist, mem, gmem, wkv)


def kernel(x_prompt, x_sample, state_pool, cache_mem_k, cache_mem_v, mem_prompt, norm_in_g, w_in,
           pool_w, pool_scale, sgu_norm_g, sgu_w, sgu_b, mem_norm_g, w_kv, w_down, w_out, norm_f_g):
    depth = w_in.shape[0]
    assert depth == 1, "single-layer step"
    nb, seq, _ = x_prompt.shape
    ns, dec_seq, _ = x_sample.shape
    assert dec_seq == 1 and seq % SEQ_TILE == 0 and seq >= HIST_ROWS

    row = lambda a: a.reshape(1, -1)
    gin, pscale, sgug, gmem, gf = (row(norm_in_g[0]), row(pool_scale[0]), row(sgu_norm_g[0]),
                                   row(mem_norm_g[0]), row(norm_f_g))
    win, poolw, wkv, wdown, wout, sguw, sgub = (w_in[0], pool_w[0], w_kv[0], w_down[0], w_out[0],
                                                sgu_w[0], sgu_b[0])
    sgub_t = sgub.T

    x_tiles = x_sample.reshape(ns * ROW_TILES, LANES)
    hist_s = jnp.transpose(state_pool[0], (1, 0, 2))
    new_hist_s, hwin_s, k_rows, v_rows, kb, vb = _prep(hist_s, mem_prompt, gmem, wkv)

    y_prompt, hist_p, p_in_s, vn_s, _, _, _, _, _, _, y_tiles = _prompt_layer(
        x_prompt, kb, vb, gin, win, poolw, pscale, sgug, sguw, sgub_t, wdown, wout, gf, x_tiles, hwin_s,
        _to_kv_rows(cache_mem_k[0].reshape(ns, N_MEM, BRANCH_W)),
        _to_kv_rows(cache_mem_v[0].reshape(ns, N_MEM, BRANCH_W)))

    new_pool_p = hist_p[None, :, HIST_ROWS - POOL_HIST:, :]
    new_hist_s = lax.dynamic_update_slice(new_hist_s, p_in_s[None], (POOL_HIST - 1, 0, 0))
    new_pool_s = jnp.transpose(new_hist_s, (1, 0, 2))[None]
    kv_out = lambda a: _from_kv_rows(a.reshape(nb, N_MEM, KV_ROWS, LANES))[None]
    return (y_prompt, y_tiles.reshape(ns, 1, D_MODEL), new_pool_p, new_pool_s,
            kv_out(k_rows), kv_out(v_rows), vn_s.reshape(1, ns, 1, BRANCH_W))
```

```python
import jax
import jax.numpy as jnp
from jax import lax
from jax.experimental import pallas as pl
from jax.experimental.pallas import tpu as pltpu

D_MODEL = 1024
BRANCH_W = 1024
N_BRANCH = 3
N_PROJ = 7 + N_BRANCH
POOL_WINDOWS = (2, 4, 8, 16)
POOL_GROUPS = len(POOL_WINDOWS)
POOL_GW = BRANCH_W // POOL_GROUPS
POOL_HIST = max(POOL_WINDOWS) - 1
HIST_ROWS = POOL_HIST + 1
SGU_CHUNK = 128
SGU_GROUPS = 4
SGU_GW = BRANCH_W // SGU_GROUPS
N_MEM = 256
XA_HEADS = 4
XA_HD = BRANCH_W // XA_HEADS
EPS = 1e-6
PAST_LEN = 16384

SEQ_TILE = 256
LANES = 128
XA_LANE_TILES = XA_HD // LANES
KV_ROWS = XA_HEADS * XA_LANE_TILES
LOG2E = 1.4426950408889634
W_STAGE_SLOTS = 4
W_STAGE_ROWS = 256
W_STAGE_COLS = 1024
V7X_VMEM_LIMIT_BYTES = 62 * 1024 * 1024

F32 = jnp.float32
BF16 = jnp.bfloat16

_sigmoid = jax.nn.sigmoid


def _rmsnorm(x, g):
    return x * lax.rsqrt(jnp.mean(x * x, axis=-1, keepdims=True) + EPS) * g


def _silu(z):
    return z * _sigmoid(z)


def _dot(a, b):
    return jnp.dot(a, b, preferred_element_type=F32)


def _resident(shape):
    zeros = (0,) * len(shape)
    return pl.BlockSpec(shape, lambda *_: zeros, pipeline_mode=pl.Buffered(1))


def _to_kv_rows(a):
    lead = a.shape[:-1]
    a = a.reshape(*lead, XA_HEADS, XA_LANE_TILES, LANES)
    return jnp.swapaxes(a, -3, -2).reshape(*lead, KV_ROWS, LANES)


ROW_TILES = D_MODEL // LANES
NATURAL_ORDER = tuple(range(ROW_TILES))
KV_ROW_ORDER = tuple((j % XA_HEADS) * XA_LANE_TILES + j // XA_HEADS for j in range(KV_ROWS))


def _load_row_tiles(ref, order=NATURAL_ORDER):
    n = ref.shape[0] // len(order)
    tiles = [None] * len(order)
    for j, t in enumerate(order):
        tiles[t] = ref[pl.ds(j, n, stride=len(order)), :]
    return jnp.concatenate(tiles, axis=1)


def _store_row_tiles(ref, val, order=NATURAL_ORDER):
    n = val.shape[0]
    for j, t in enumerate(order):
        ref[pl.ds(j, n, stride=len(order)), :] = val[:, t * LANES:(t + 1) * LANES]


def _from_kv_rows(a):
    lead = a.shape[:-2]
    a = a.reshape(*lead, XA_LANE_TILES, XA_HEADS, LANES)
    return jnp.swapaxes(a, -3, -2).reshape(*lead, XA_HEADS, XA_HD)


def _mem_kv_body(mem_ref, g_ref, wkv_s, k_ref, v_ref, kb_ref, vb_ref):
    m = _rmsnorm(mem_ref[0], g_ref[...]).astype(BF16)
    kv = _dot(m, wkv_s[...])
    k = kv[:, :BRANCH_W]
    v = kv[:, BRANCH_W:]
    kb_ref[0] = k.astype(BF16)
    vb_ref[0] = v.astype(BF16)
    for h in range(XA_HEADS):
        for lt in range(XA_LANE_TILES):
            cols = slice(h * XA_HD + lt * LANES, h * XA_HD + (lt + 1) * LANES)
            rows = pl.ds(lt * XA_HEADS + h, N_MEM, stride=KV_ROWS)
            k_ref[0, rows, :] = k[:, cols]
            v_ref[0, rows, :] = v[:, cols]


def _stage_weights(copies, stage_ref, sem_ref, after_chunk):
    slots = stage_ref.shape[0]
    ahead = slots - 1

    def chunk_copy(k):
        return pltpu.make_async_copy(copies[k][0], stage_ref.at[k % slots], sem_ref.at[k % slots])

    for k in range(min(ahead, len(copies))):
        chunk_copy(k).start()
    for k, (_, dst) in enumerate(copies):
        if k + ahead < len(copies):
            chunk_copy(k + ahead).start()
        chunk_copy(k).wait()
        dst[...] = stage_ref[k % slots].astype(BF16)
        after_chunk(k)


class _HbmWriter:
    def __init__(self, stage_ref, sem_ref):
        self.stage_ref, self.sem_ref, self.pending, self.count = stage_ref, sem_ref, [], 0

    def write(self, value, dst_hbm):
        slots = self.stage_ref.shape[0]
        slot = self.count % slots
        if len(self.pending) == slots:
            self.pending.pop(0).wait()
        self.stage_ref[slot] = value
        copy = pltpu.make_async_copy(self.stage_ref.at[slot], dst_hbm, self.sem_ref.at[slot])
        copy.start()
        self.pending.append(copy)
        self.count += 1

    def finish(self):
        for copy in self.pending:
            copy.wait()
        self.pending = []


def _first_step(xs_ref, hist_hbm, gin_ref, poolw_ref, pscale_ref, sgug_ref, sguw_ref, sgub_ref,
                win_hbm, wdown_hbm, wout_hbm, win_ref, wdown_ref, wout_ref, stage_ref, stage_sem,
                qs_ref, ostage_ref, ostage_sem, roll_sem, slab_sem,
                newhist_hbm, vn_hbm, opool_hbm, osgu_hbm, az_hbm, gate_hbms):
    _, sr, sc = stage_ref.shape
    assert sc == BRANCH_W and D_MODEL % sr == 0
    k_slabs = D_MODEL // sr
    tiles = lambda ref: [(pl.ds(r * sr, sr), pl.ds(c * sc, sc))
                         for c in range(ref.shape[-1] // sc) for r in range(ref.shape[-2] // sr)]
    copies = [(win_hbm.at[r, c], win_ref.at[r, c]) for r, c in tiles(win_ref)]
    n_win = len(copies)
    copies += [(wdown_hbm.at[n, r, c], wdown_ref.at[n, r, c]) for n in range(N_BRANCH) for r, c in tiles(wdown_ref)]
    copies += [(wout_hbm.at[r, c], wout_ref.at[r, c]) for r, c in tiles(wout_ref)]

    ns = xs_ref.shape[0] // ROW_TILES
    hbs = _rmsnorm(_load_row_tiles(xs_ref), gin_ref[...]).astype(BF16)
    writer = _HbmWriter(ostage_ref, ostage_sem)
    kept = {}

    roll = pltpu.make_async_copy(hist_hbm.at[pl.ds(1, POOL_HIST - 1)], newhist_hbm.at[pl.ds(0, POOL_HIST - 1)],
                                 roll_sem.at[0])
    roll.start()
    slabs = []
    for g, w in enumerate(POOL_WINDOWS):
        for j in range(1, w):
            t = len(slabs)
            assert sr % ns == 0 and sc % POOL_GW == 0
            per_slot = (sr // ns) * (sc // POOL_GW)
            window = stage_ref.at[t // per_slot, pl.ds((t % per_slot) // (sc // POOL_GW) * ns, ns),
                                  pl.ds((t % (sc // POOL_GW)) * POOL_GW, POOL_GW)]
            copy = pltpu.make_async_copy(hist_hbm.at[POOL_HIST - j, :, pl.ds(g * POOL_GW, POOL_GW)], window,
                                         slab_sem.at[t])
            copy.start()
            slabs.append((g, copy, window))
    hist_sum = [None] * POOL_GROUPS
    for g, copy, window in slabs:
        copy.wait()
        hist_sum[g] = window[...] if hist_sum[g] is None else hist_sum[g] + window[...]

    def finish_chunk(c, val):
        if c == 0:
            kept["p_in"] = val
            writer.write(val, newhist_hbm.at[POOL_HIST - 1])
        elif c == 1:
            p_in = kept["p_in"]
            mixed = []
            for g, w in enumerate(POOL_WINDOWS):
                sl = slice(g * POOL_GW, (g + 1) * POOL_GW)
                d = (p_in[:, sl] + hist_sum[g]) / float(min(w, PAST_LEN + 1)) - p_in[:, sl]
                mixed.append(_dot(d.astype(BF16), poolw_ref[g].astype(BF16)))
            writer.write(jnp.concatenate(mixed, axis=1) * pscale_ref[...] * _silu(val), opool_hbm)
        elif c == 2:
            kept["u"] = val
        elif c == 3:
            vn = _rmsnorm(val, sgug_ref[...])
            writer.write(vn, vn_hbm)
            gated = [vn[:, g * SGU_GW:(g + 1) * SGU_GW] * sguw_ref[g, 0:1, 0:1] + sgub_ref[0:1, g:g + 1]
                     for g in range(SGU_GROUPS)]
            kept["u_gated"] = kept["u"] * jnp.concatenate(gated, axis=1)
        elif c == 4:
            writer.write(kept["u_gated"] * _silu(val), osgu_hbm)
        elif c == 5:
            _store_row_tiles(qs_ref, val, KV_ROW_ORDER)
        elif c == 6:
            writer.write(val, az_hbm)
        else:
            writer.write(val, gate_hbms[c - 7])

    def after_chunk(k):
        if k >= n_win:
            return
        c, r = divmod(k, k_slabs)
        part = _dot(hbs[:, r * sr:(r + 1) * sr], win_ref[r * sr:(r + 1) * sr, c * sc:(c + 1) * sc])
        kept["acc"] = part if r == 0 else kept["acc"] + part
        if r == k_slabs - 1:
            finish_chunk(c, kept["acc"])

    _stage_weights(copies, stage_ref, stage_sem, after_chunk)
    writer.finish()
    roll.wait()


def _last_step(xs_ref, attn_ref, gf_ref, wdown_ref, wout_ref, opool_hbm, osgu_hbm, az_hbm, gate_hbms,
               stage_ref, last_sem, tiles_ref, ys_hbm):
    ns = xs_ref.shape[0] // ROW_TILES
    srcs = (opool_hbm, osgu_hbm, az_hbm) + tuple(gate_hbms)
    per_slot = stage_ref.shape[1] // ns
    bufs = [stage_ref.at[k // per_slot, pl.ds((k % per_slot) * ns, ns), :] for k in range(len(srcs))]
    copies = [pltpu.make_async_copy(src, buf, last_sem.at[k]) for k, (src, buf) in enumerate(zip(srcs, bufs))]
    for copy in copies:
        copy.start()
    for copy in copies:
        copy.wait()
    o_pool, o_sgu, a_z, *gates = [buf[...] for buf in bufs]
    o_xa = _load_row_tiles(attn_ref, KV_ROW_ORDER) * _silu(a_z)
    merged = None
    for n, o in enumerate((o_pool, o_sgu, o_xa)):
        t = _sigmoid(gates[n]) * _dot(o.astype(BF16), wdown_ref[n])
        merged = t if merged is None else merged + t
    xn = _load_row_tiles(xs_ref) + _dot(merged.astype(BF16), wout_ref[...])
    _store_row_tiles(tiles_ref, _rmsnorm(xn, gf_ref[...]))
    out = pltpu.make_async_copy(tiles_ref, ys_hbm, last_sem.at[0])
    out.start()
    out.wait()


def _sample_attention(get_q, k_ref, v_ref, put_o):
    for r in range(k_ref.shape[0]):
        q = get_q(r) * (XA_HD ** -0.5 * LOG2E)
        part = k_ref[r] * q
        part = part + pltpu.roll(part, XA_HEADS, axis=1)
        s = jnp.sum(part, axis=-1, keepdims=True)
        e = jnp.exp2(s - jnp.max(s, axis=0, keepdims=True))
        put_o(r, jnp.sum(e * v_ref[r], axis=0) / jnp.sum(e, axis=0))


def _prompt_kernel(x_ref, k_ref, v_ref, gin_ref, win_hbm, poolw_ref, pscale_ref, sgug_ref,
                   sguw_ref, sgub_ref, wdown_hbm, wout_hbm, gf_ref, xs_ref, shist_hbm, sk_ref, sv_ref,
                   y_ref, hist_ref, snewhist_hbm, vn_hbm, opool_hbm, osgu_hbm, az_hbm, g0_hbm, g1_hbm, g2_hbm,
                   ys_hbm,
                   ext_ref, win_ref, wdown_ref, wout_ref, stage_ref, stage_sem, qs_ref, attn_ref,
                   ostage_ref, ostage_sem, last_sem, roll_sem, slab_sem):
    i = pl.program_id(1)
    ts = x_ref.shape[1]
    step = pl.program_id(0) * pl.num_programs(1) + i

    @pl.when(step == 0)
    def _():
        _first_step(xs_ref, shist_hbm, gin_ref, poolw_ref, pscale_ref, sgug_ref, sguw_ref, sgub_ref,
                    win_hbm, wdown_hbm, wout_hbm, win_ref, wdown_ref, wout_ref, stage_ref, stage_sem,
                    qs_ref, ostage_ref, ostage_sem, roll_sem, slab_sem,
                    snewhist_hbm, vn_hbm, opool_hbm, osgu_hbm, az_hbm, (g0_hbm, g1_hbm, g2_hbm))

    @pl.when(i == 0)
    def _():
        ext_ref[0:HIST_ROWS, :] = jnp.zeros((HIST_ROWS, BRANCH_W), F32)

    q_rows = sk_ref.shape[0] * KV_ROWS
    q_blk = qs_ref[pl.ds(pl.multiple_of(step * q_rows, q_rows), q_rows), :]

    def put_attention(r, out):
        attn_ref[pl.ds(pl.multiple_of(step * q_rows + r * KV_ROWS, KV_ROWS), KV_ROWS), :] = out

    _sample_attention(lambda r: q_blk[r * KV_ROWS:(r + 1) * KV_ROWS, :], sk_ref, sv_ref, put_attention)

    x = x_ref[0]
    hb = _rmsnorm(x, gin_ref[...]).astype(BF16)

    def proj(c):
        return _dot(hb, win_ref[:, c * BRANCH_W:(c + 1) * BRANCH_W])


    p_in = proj(0)
    p_z = proj(1)
    ext_ref[HIST_ROWS:HIST_ROWS + ts, :] = p_in
    pos = i * ts + lax.broadcasted_iota(jnp.int32, (ts, 1), 0)
    pooled = []
    for g, w in enumerate(POOL_WINDOWS):
        sl = slice(g * POOL_GW, (g + 1) * POOL_GW)
        win = ext_ref[:, sl]
        span = 1
        while span < w:
            win = win + pltpu.roll(win, span, axis=0)
            span *= 2
        inv_cnt = 1.0 / jnp.minimum(w, pos + 1).astype(F32)
        pooled.append((win[HIST_ROWS:, :] * inv_cnt - p_in[:, sl]).astype(BF16))
    ext_ref[0:HIST_ROWS, :] = p_in[ts - HIST_ROWS:, :]
    hist_ref[0] = p_in[ts - HIST_ROWS:, :]

    v = proj(3)
    u = proj(2)
    s_z = proj(4)
    mixed = [_dot(pooled[g], poolw_ref[g].astype(BF16)) for g in range(POOL_GROUPS)]
    o_pool = (jnp.concatenate(mixed, axis=1) * pscale_ref[...] * _silu(p_z)).astype(BF16)

    vnb = _rmsnorm(v, sgug_ref[...]).astype(BF16)
    tril = (lax.broadcasted_iota(jnp.int32, (SGU_CHUNK, SGU_CHUNK), 0)
            >= lax.broadcasted_iota(jnp.int32, (SGU_CHUNK, SGU_CHUNK), 1))
    ws = [jnp.where(tril, sguw_ref[g], 0.0).astype(BF16) for g in range(SGU_GROUPS)]
    q = proj(5)
    a_z = proj(6)
    rows = []
    for c in range(ts // SGU_CHUNK):
        rs = slice(c * SGU_CHUNK, (c + 1) * SGU_CHUNK)
        cols = [_dot(ws[g], vnb[rs, g * SGU_GW:(g + 1) * SGU_GW]) + sgub_ref[:, g:g + 1]
                for g in range(SGU_GROUPS)]
        rows.append(jnp.concatenate(cols, axis=1))
    o_sgu = (u * jnp.concatenate(rows, axis=0) * _silu(s_z)).astype(BF16)

    qb = q.astype(BF16)
    scores = [lax.dot_general(qb[:, hd * XA_HD:(hd + 1) * XA_HD], k_ref[0, :, hd * XA_HD:(hd + 1) * XA_HD],
                              (((1,), (1,)), ((), ())), preferred_element_type=F32) * (XA_HD ** -0.5 * LOG2E)
              for hd in range(XA_HEADS)]
    gates = [proj(7 + n) for n in range(N_BRANCH)]
    heads = []
    for hd in range(XA_HEADS):
        s = scores[hd]
        e = jnp.exp2(s - jnp.max(s, axis=-1, keepdims=True))
        pr = e * (1.0 / jnp.sum(e, axis=-1, keepdims=True))
        heads.append(_dot(pr.astype(BF16), v_ref[0, :, hd * XA_HD:(hd + 1) * XA_HD]))
    o_xa = (jnp.concatenate(heads, axis=1) * _silu(a_z)).astype(BF16)

    merged = None
    for n, o in enumerate((o_pool, o_sgu, o_xa)):
        t = _sigmoid(gates[n]) * _dot(o, wdown_ref[n])
        merged = t if merged is None else merged + t
    xn = x + _dot(merged.astype(BF16), wout_ref[...])
    y_ref[0] = _rmsnorm(xn, gf_ref[...])

    @pl.when(step == pl.num_programs(0) * pl.num_programs(1) - 1)
    def _():
        _last_step(xs_ref, attn_ref, gf_ref, wdown_ref, wout_ref, opool_hbm, osgu_hbm, az_hbm,
                   (g0_hbm, g1_hbm, g2_hbm), stage_ref, last_sem, qs_ref, ys_hbm)


def _prompt_layer(x, kb, vb, gin, win, poolw, pscale, sgug, sguw, sgub_t, wdown, wout, gf,
                  xs_tiles, shist, sk, sv):
    nb, seq, _ = x.shape
    ts = SEQ_TILE
    n_tiles = seq // ts
    ns = sk.shape[0]
    rb = ns // (nb * n_tiles)
    assert rb * nb * n_tiles == ns
    sample_row = jax.ShapeDtypeStruct((ns, BRANCH_W), F32)
    n_hist_slabs = sum(w - 1 for w in POOL_WINDOWS)
    assert n_hist_slabs * ns * POOL_GW <= W_STAGE_SLOTS * W_STAGE_ROWS * W_STAGE_COLS
    in_cols = win.shape[1]
    assert in_cols % W_STAGE_COLS == 0 and D_MODEL % W_STAGE_COLS == 0 and D_MODEL % W_STAGE_ROWS == 0
    step = lambda b, i: b * n_tiles + i
    hbm = lambda: pl.BlockSpec(memory_space=pl.ANY)
    return pl.pallas_call(
        _prompt_kernel,
        grid=(nb, n_tiles),
        in_specs=[pl.BlockSpec((1, ts, D_MODEL), lambda b, i: (b, i, 0)),
                  pl.BlockSpec((1, N_MEM, BRANCH_W), lambda b, i: (b, 0, 0)),
                  pl.BlockSpec((1, N_MEM, BRANCH_W), lambda b, i: (b, 0, 0)),
                  _resident((1, D_MODEL)),
                  hbm(),
                  _resident((POOL_GROUPS, POOL_GW, POOL_GW)),
                  _resident((1, BRANCH_W)),
                  _resident((1, BRANCH_W)),
                  _resident((SGU_GROUPS, SGU_CHUNK, SGU_CHUNK)),
                  _resident((SGU_CHUNK, SGU_GROUPS)),
                  hbm(),
                  hbm(),
                  _resident((1, D_MODEL)),
                  _resident((ns * ROW_TILES, LANES)),
                  hbm(),
                  pl.BlockSpec((rb, N_MEM, KV_ROWS, LANES), lambda b, i: (step(b, i), 0, 0, 0)),
                  pl.BlockSpec((rb, N_MEM, KV_ROWS, LANES), lambda b, i: (step(b, i), 0, 0, 0))],
        out_specs=[pl.BlockSpec((1, ts, D_MODEL), lambda b, i: (b, i, 0)),
                   pl.BlockSpec((1, HIST_ROWS, BRANCH_W), lambda b, i: (b, 0, 0))] + [hbm()] * 9,
        out_shape=[jax.ShapeDtypeStruct((nb, seq, D_MODEL), F32),
                   jax.ShapeDtypeStruct((nb, HIST_ROWS, BRANCH_W), F32),
                   jax.ShapeDtypeStruct(shist.shape, F32)] + [sample_row] * 7
                  + [jax.ShapeDtypeStruct((ns * ROW_TILES, LANES), F32)],
        scratch_shapes=[pltpu.VMEM((HIST_ROWS + ts, BRANCH_W), F32),
                        pltpu.VMEM((D_MODEL, in_cols), BF16),
                        pltpu.VMEM((N_BRANCH, BRANCH_W, D_MODEL), BF16),
                        pltpu.VMEM((D_MODEL, D_MODEL), BF16),
                        pltpu.VMEM((W_STAGE_SLOTS, W_STAGE_ROWS, W_STAGE_COLS), F32),
                        pltpu.SemaphoreType.DMA((W_STAGE_SLOTS,)),
                        pltpu.VMEM((ns * KV_ROWS, LANES), F32),
                        pltpu.VMEM((ns * KV_ROWS, LANES), F32),
                        pltpu.VMEM((2, ns, BRANCH_W), F32),
                        pltpu.SemaphoreType.DMA((2,)),
                        pltpu.SemaphoreType.DMA((2 * N_BRANCH,)),
                        pltpu.SemaphoreType.DMA((1,)),
                        pltpu.SemaphoreType.DMA((n_hist_slabs,))],
        compiler_params=pltpu.CompilerParams(dimension_semantics=("arbitrary", "arbitrary"),
                                             vmem_limit_bytes=V7X_VMEM_LIMIT_BYTES),
        name="prompt_layer",
    )(x, kb, vb, gin, win, poolw, pscale, sgug, sguw, sgub_t, wdown, wout, gf, xs_tiles, shist, sk, sv)


def _mem_kv_kernel(mem_ref, gmem_ref, wkv_ref, k_ref, v_ref, kb_ref, vb_ref, wkv_s):
    @pl.when(pl.program_id(0) == 0)
    def _():
        wkv_s[...] = wkv_ref[...].astype(BF16)

    _mem_kv_body(mem_ref, gmem_ref, wkv_s, k_ref, v_ref, kb_ref, vb_ref)


def _mem_kv(mem, gmem, wkv):
    nb = mem.shape[0]
    rows_blk = lambda: pl.BlockSpec((1, N_MEM * KV_ROWS, LANES), lambda b: (b, 0, 0))
    flat_blk = lambda: pl.BlockSpec((1, N_MEM, BRANCH_W), lambda b: (b, 0, 0))
    return pl.pallas_call(
        _mem_kv_kernel,
        grid=(nb,),
        in_specs=[pl.BlockSpec((1, N_MEM, D_MODEL), lambda b: (b, 0, 0)),
                  _resident((1, D_MODEL)),
                  _resident((D_MODEL, 2 * BRANCH_W))],
        out_specs=[rows_blk(), rows_blk(), flat_blk(), flat_blk()],
        out_shape=[jax.ShapeDtypeStruct((nb, N_MEM * KV_ROWS, LANES), F32),
                   jax.ShapeDtypeStruct((nb, N_MEM * KV_ROWS, LANES), F32),
                   jax.ShapeDtypeStruct((nb, N_MEM, BRANCH_W), BF16),
                   jax.ShapeDtypeStruct((nb, N_MEM, BRANCH_W), BF16)],
        scratch_shapes=[pltpu.VMEM((D_MODEL, 2 * BRANCH_W), BF16)],
        compiler_params=pltpu.CompilerParams(dimension_semantics=("arbitrary",),
                                             vmem_limit_bytes=V7X_VMEM_LIMIT_BYTES),
        name="mem_kv",
    )(mem, gmem, wkv)


def kernel(x_prompt, x_sample, state_pool, cache_mem_k, cache_mem_v, mem_prompt, norm_in_g, w_in,
           pool_w, pool_scale, sgu_norm_g, sgu_w, sgu_b, mem_norm_g, w_kv, w_down, w_out, norm_f_g):
    depth = w_in.shape[0]
    assert depth == 1, "single-layer step"
    nb, seq, _ = x_prompt.shape
    ns, dec_seq, _ = x_sample.shape
    assert dec_seq == 1 and seq % SEQ_TILE == 0 and seq >= HIST_ROWS

    row = lambda a: a.reshape(1, -1)
    gin, pscale, sgug, gmem, gf = (row(norm_in_g[0]), row(pool_scale[0]), row(sgu_norm_g[0]),
                                   row(mem_norm_g[0]), row(norm_f_g))
    win, poolw, wkv, wdown, wout, sguw, sgub = (w_in[0], pool_w[0], w_kv[0], w_down[0], w_out[0],
                                                sgu_w[0], sgu_b[0])
    sgub_t = sgub.T

    x_tiles = x_sample.reshape(ns * ROW_TILES, LANES)
    hist_s = jnp.transpose(state_pool[0], (1, 0, 2))
    k_rows, v_rows, kb, vb = _mem_kv(mem_prompt, gmem, wkv)

    y_prompt, hist_p, new_hist_s, vn_s, _, _, _, _, _, _, y_tiles = _prompt_layer(
        x_prompt, kb, vb, gin, win, poolw, pscale, sgug, sguw, sgub_t, wdown, wout, gf, x_tiles, hist_s,
        _to_kv_rows(cache_mem_k[0].reshape(ns, N_MEM, BRANCH_W)),
        _to_kv_rows(cache_mem_v[0].reshape(ns, N_MEM, BRANCH_W)))

    new_pool_p = hist_p[None, :, HIST_ROWS - POOL_HIST:, :]
    new_pool_s = jnp.transpose(new_hist_s, (1, 0, 2))[None]
    kv_out = lambda a: _from_kv_rows(a.reshape(nb, N_MEM, KV_ROWS, LANES))[None]
    return (y_prompt, y_tiles.reshape(ns, 1, D_MODEL), new_pool_p, new_pool_s,
            kv_out(k_rows), kv_out(v_rows), vn_s.reshape(1, ns, 1, BRANCH_W))
```

```python
import jax
import jax.numpy as jnp
from jax import lax
from jax.experimental import pallas as pl
from jax.experimental.pallas import tpu as pltpu

D_MODEL = 1024
BRANCH_W = 1024
N_BRANCH = 3
N_PROJ = 7 + N_BRANCH
POOL_WINDOWS = (2, 4, 8, 16)
POOL_GROUPS = len(POOL_WINDOWS)
POOL_GW = BRANCH_W // POOL_GROUPS
POOL_HIST = max(POOL_WINDOWS) - 1
HIST_ROWS = POOL_HIST + 1
SGU_CHUNK = 128
SGU_GROUPS = 4
SGU_GW = BRANCH_W // SGU_GROUPS
N_MEM = 256
XA_HEADS = 4
XA_HD = BRANCH_W // XA_HEADS
EPS = 1e-6
PAST_LEN = 16384

SEQ_TILE = 256
LANES = 128
XA_LANE_TILES = XA_HD // LANES
KV_ROWS = XA_HEADS * XA_LANE_TILES
LOG2E = 1.4426950408889634
W_STAGE_SLOTS = 8
W_STAGE_ROWS = 128
W_STAGE_COLS = 1024
V7X_VMEM_LIMIT_BYTES = 62 * 1024 * 1024

F32 = jnp.float32
BF16 = jnp.bfloat16

_sigmoid = jax.nn.sigmoid


def _rmsnorm(x, g):
    return x * lax.rsqrt(jnp.mean(x * x, axis=-1, keepdims=True) + EPS) * g


def _silu(z):
    return z * _sigmoid(z)


def _dot(a, b):
    return jnp.dot(a, b, preferred_element_type=F32)


def _resident(shape):
    zeros = (0,) * len(shape)
    return pl.BlockSpec(shape, lambda *_: zeros, pipeline_mode=pl.Buffered(1))


def _to_kv_rows(a):
    lead = a.shape[:-1]
    a = a.reshape(*lead, XA_HEADS, XA_LANE_TILES, LANES)
    return jnp.swapaxes(a, -3, -2).reshape(*lead, KV_ROWS, LANES)


ROW_TILES = D_MODEL // LANES
NATURAL_ORDER = tuple(range(ROW_TILES))
KV_ROW_ORDER = tuple((j % XA_HEADS) * XA_LANE_TILES + j // XA_HEADS for j in range(KV_ROWS))


def _load_row_tiles(ref, order=NATURAL_ORDER):
    n = ref.shape[0] // len(order)
    tiles = [None] * len(order)
    for j, t in enumerate(order):
        tiles[t] = ref[pl.ds(j, n, stride=len(order)), :]
    return jnp.concatenate(tiles, axis=1)


def _store_row_tiles(ref, val, order=NATURAL_ORDER):
    n = val.shape[0]
    for j, t in enumerate(order):
        ref[pl.ds(j, n, stride=len(order)), :] = val[:, t * LANES:(t + 1) * LANES]


def _from_kv_rows(a):
    lead = a.shape[:-2]
    a = a.reshape(*lead, XA_LANE_TILES, XA_HEADS, LANES)
    return jnp.swapaxes(a, -3, -2).reshape(*lead, XA_HEADS, XA_HD)


def _mem_kv_body(mem_ref, g_ref, wkv_s, k_ref, v_ref, kb_ref, vb_ref):
    m = _rmsnorm(mem_ref[0], g_ref[...]).astype(BF16)
    kv = _dot(m, wkv_s[...])
    k = kv[:, :BRANCH_W]
    v = kv[:, BRANCH_W:]
    kb_ref[0] = k.astype(BF16)
    vb_ref[0] = v.astype(BF16)
    for h in range(XA_HEADS):
        for lt in range(XA_LANE_TILES):
            cols = slice(h * XA_HD + lt * LANES, h * XA_HD + (lt + 1) * LANES)
            rows = pl.ds(lt * XA_HEADS + h, N_MEM, stride=KV_ROWS)
            k_ref[0, rows, :] = k[:, cols]
            v_ref[0, rows, :] = v[:, cols]


def _stage_weights(copies, stage_ref, sem_ref, after_chunk):
    slots = stage_ref.shape[0]
    ahead = slots - 1

    def chunk_copy(k):
        return pltpu.make_async_copy(copies[k][0], stage_ref.at[k % slots], sem_ref.at[k % slots])

    for k in range(min(ahead, len(copies))):
        chunk_copy(k).start()
    for k, (_, dst) in enumerate(copies):
        if k + ahead < len(copies):
            chunk_copy(k + ahead).start()
        chunk_copy(k).wait()
        dst[...] = stage_ref[k % slots].astype(BF16)
        after_chunk(k)


class _HbmWriter:
    def __init__(self, stage_ref, sem_ref):
        self.stage_ref, self.sem_ref, self.pending, self.count = stage_ref, sem_ref, [], 0

    def write(self, value, dst_hbm):
        slots = self.stage_ref.shape[0]
        slot = self.count % slots
        if len(self.pending) == slots:
            self.pending.pop(0).wait()
        self.stage_ref[slot] = value
        copy = pltpu.make_async_copy(self.stage_ref.at[slot], dst_hbm, self.sem_ref.at[slot])
        copy.start()
        self.pending.append(copy)
        self.count += 1

    def finish(self):
        for copy in self.pending:
            copy.wait()
        self.pending = []


def _first_step(xs_ref, hwin_ref, gin_ref, poolw_ref, pscale_ref, sgug_ref, sguw_ref, sgub_ref,
                win_hbm, wdown_hbm, wout_hbm, win_ref, wdown_ref, wout_ref, stage_ref, stage_sem,
                qs_ref, tiles_ref, tiles_sem, ostage_ref, ostage_sem,
                pin_hbm, vn_hbm, opool_hbm, osgu_hbm, az_hbm, gate_hbms):
    _, sr, sc = stage_ref.shape
    assert sc == BRANCH_W and D_MODEL % sr == 0
    k_slabs = D_MODEL // sr
    tiles = lambda ref: [(pl.ds(r * sr, sr), pl.ds(c * sc, sc))
                         for c in range(ref.shape[-1] // sc) for r in range(ref.shape[-2] // sr)]
    copies = [(win_hbm.at[r, c], win_ref.at[r, c]) for r, c in tiles(win_ref)]
    n_win = len(copies)
    copies += [(wdown_hbm.at[n, r, c], wdown_ref.at[n, r, c]) for n in range(N_BRANCH) for r, c in tiles(wdown_ref)]
    copies += [(wout_hbm.at[r, c], wout_ref.at[r, c]) for r, c in tiles(wout_ref)]

    hbs = _rmsnorm(_load_row_tiles(xs_ref), gin_ref[...]).astype(BF16)
    writer = _HbmWriter(ostage_ref, ostage_sem)
    kept = {}

    def finish_chunk(c, val):
        if c == 0:
            kept["p_in"] = val
            writer.write(val, pin_hbm)
        elif c == 1:
            p_in = kept["p_in"]
            mixed = []
            for g, w in enumerate(POOL_WINDOWS):
                sl = slice(g * POOL_GW, (g + 1) * POOL_GW)
                d = (p_in[:, sl] + hwin_ref[:, sl]) / float(min(w, PAST_LEN + 1)) - p_in[:, sl]
                mixed.append(_dot(d.astype(BF16), poolw_ref[g].astype(BF16)))
            writer.write(jnp.concatenate(mixed, axis=1) * pscale_ref[...] * _silu(val), opool_hbm)
        elif c == 2:
            kept["u"] = val
        elif c == 3:
            vn = _rmsnorm(val, sgug_ref[...])
            _store_row_tiles(tiles_ref, vn)
            kept["vn_copy"] = pltpu.make_async_copy(tiles_ref, vn_hbm, tiles_sem.at[0])
            kept["vn_copy"].start()
            gated = [vn[:, g * SGU_GW:(g + 1) * SGU_GW] * sguw_ref[g, 0:1, 0:1] + sgub_ref[0:1, g:g + 1]
                     for g in range(SGU_GROUPS)]
            kept["u_gated"] = kept["u"] * jnp.concatenate(gated, axis=1)
        elif c == 4:
            writer.write(kept["u_gated"] * _silu(val), osgu_hbm)
        elif c == 5:
            _store_row_tiles(qs_ref, val, KV_ROW_ORDER)
        elif c == 6:
            writer.write(val, az_hbm)
        else:
            writer.write(val, gate_hbms[c - 7])

    def after_chunk(k):
        if k >= n_win:
            return
        c, r = divmod(k, k_slabs)
        part = _dot(hbs[:, r * sr:(r + 1) * sr], win_ref[r * sr:(r + 1) * sr, c * sc:(c + 1) * sc])
        kept["acc"] = part if r == 0 else kept["acc"] + part
        if r == k_slabs - 1:
            finish_chunk(c, kept["acc"])

    _stage_weights(copies, stage_ref, stage_sem, after_chunk)
    writer.finish()
    kept["vn_copy"].wait()


def _last_step(xs_ref, attn_ref, gf_ref, wdown_ref, wout_ref, opool_hbm, osgu_hbm, az_hbm, gate_hbms,
               stage_ref, last_sem, tiles_ref, ys_hbm):
    ns = xs_ref.shape[0] // ROW_TILES
    srcs = (opool_hbm, osgu_hbm, az_hbm) + tuple(gate_hbms)
    per_slot = stage_ref.shape[1] // ns
    bufs = [stage_ref.at[k // per_slot, pl.ds((k % per_slot) * ns, ns), :] for k in range(len(srcs))]
    copies = [pltpu.make_async_copy(src, buf, last_sem.at[k]) for k, (src, buf) in enumerate(zip(srcs, bufs))]
    for copy in copies:
        copy.start()
    for copy in copies:
        copy.wait()
    o_pool, o_sgu, a_z, *gates = [buf[...] for buf in bufs]
    o_xa = _load_row_tiles(attn_ref, KV_ROW_ORDER) * _silu(a_z)
    merged = None
    for n, o in enumerate((o_pool, o_sgu, o_xa)):
        t = _sigmoid(gates[n]) * _dot(o.astype(BF16), wdown_ref[n])
        merged = t if merged is None else merged + t
    xn = _load_row_tiles(xs_ref) + _dot(merged.astype(BF16), wout_ref[...])
    _store_row_tiles(tiles_ref, _rmsnorm(xn, gf_ref[...]))
    out = pltpu.make_async_copy(tiles_ref, ys_hbm, last_sem.at[0])
    out.start()
    out.wait()


def _sample_attention(get_q, k_ref, v_ref, put_o):
    for r in range(k_ref.shape[0]):
        q = get_q(r) * (XA_HD ** -0.5 * LOG2E)
        part = k_ref[r] * q
        part = part + pltpu.roll(part, XA_HEADS, axis=1)
        s = jnp.sum(part, axis=-1, keepdims=True)
        e = jnp.exp2(s - jnp.max(s, axis=0, keepdims=True))
        put_o(r, jnp.sum(e * v_ref[r], axis=0) / jnp.sum(e, axis=0))


def _prompt_kernel(x_ref, k_ref, v_ref, gin_ref, win_hbm, poolw_ref, pscale_ref, sgug_ref,
                   sguw_ref, sgub_ref, wdown_hbm, wout_hbm, gf_ref, xs_ref, hwin_ref, sk_ref, sv_ref,
                   y_ref, hist_ref, pin_hbm, vn_hbm, opool_hbm, osgu_hbm, az_hbm, g0_hbm, g1_hbm, g2_hbm, ys_hbm,
                   ext_ref, win_ref, wdown_ref, wout_ref, stage_ref, stage_sem, qs_ref, attn_ref,
                   ostage_ref, ostage_sem, last_sem):
    i = pl.program_id(1)
    ts = x_ref.shape[1]
    step = pl.program_id(0) * pl.num_programs(1) + i

    @pl.when(step == 0)
    def _():
        _first_step(xs_ref, hwin_ref, gin_ref, poolw_ref, pscale_ref, sgug_ref, sguw_ref, sgub_ref,
                    win_hbm, wdown_hbm, wout_hbm, win_ref, wdown_ref, wout_ref, stage_ref, stage_sem,
                    qs_ref, attn_ref, last_sem, ostage_ref, ostage_sem,
                    pin_hbm, vn_hbm, opool_hbm, osgu_hbm, az_hbm, (g0_hbm, g1_hbm, g2_hbm))

    @pl.when(i == 0)
    def _():
        ext_ref[0:HIST_ROWS, :] = jnp.zeros((HIST_ROWS, BRANCH_W), F32)

    q_rows = sk_ref.shape[0] * KV_ROWS
    q_blk = qs_ref[pl.ds(pl.multiple_of(step * q_rows, q_rows), q_rows), :]

    def put_attention(r, out):
        attn_ref[pl.ds(pl.multiple_of(step * q_rows + r * KV_ROWS, KV_ROWS), KV_ROWS), :] = out

    _sample_attention(lambda r: q_blk[r * KV_ROWS:(r + 1) * KV_ROWS, :], sk_ref, sv_ref, put_attention)

    x = x_ref[0]
    hb = _rmsnorm(x, gin_ref[...]).astype(BF16)

    def proj(c):
        return _dot(hb, win_ref[:, c * BRANCH_W:(c + 1) * BRANCH_W])


    p_in = proj(0)
    p_z = proj(1)
    ext_ref[HIST_ROWS:HIST_ROWS + ts, :] = p_in
    pos = i * ts + lax.broadcasted_iota(jnp.int32, (ts, 1), 0)
    pooled = []
    for g, w in enumerate(POOL_WINDOWS):
        sl = slice(g * POOL_GW, (g + 1) * POOL_GW)
        win = ext_ref[:, sl]
        span = 1
        while span < w:
            win = win + pltpu.roll(win, span, axis=0)
            span *= 2
        inv_cnt = 1.0 / jnp.minimum(w, pos + 1).astype(F32)
        pooled.append((win[HIST_ROWS:, :] * inv_cnt - p_in[:, sl]).astype(BF16))
    ext_ref[0:HIST_ROWS, :] = p_in[ts - HIST_ROWS:, :]
    hist_ref[0] = p_in[ts - HIST_ROWS:, :]

    v = proj(3)
    u = proj(2)
    s_z = proj(4)
    mixed = [_dot(pooled[g], poolw_ref[g].astype(BF16)) for g in range(POOL_GROUPS)]
    o_pool = (jnp.concatenate(mixed, axis=1) * pscale_ref[...] * _silu(p_z)).astype(BF16)

    vnb = _rmsnorm(v, sgug_ref[...]).astype(BF16)
    tril = (lax.broadcasted_iota(jnp.int32, (SGU_CHUNK, SGU_CHUNK), 0)
            >= lax.broadcasted_iota(jnp.int32, (SGU_CHUNK, SGU_CHUNK), 1))
    ws = [jnp.where(tril, sguw_ref[g], 0.0).astype(BF16) for g in range(SGU_GROUPS)]
    q = proj(5)
    a_z = proj(6)
    rows = []
    for c in range(ts // SGU_CHUNK):
        rs = slice(c * SGU_CHUNK, (c + 1) * SGU_CHUNK)
        cols = [_dot(ws[g], vnb[rs, g * SGU_GW:(g + 1) * SGU_GW]) + sgub_ref[:, g:g + 1]
                for g in range(SGU_GROUPS)]
        rows.append(jnp.concatenate(cols, axis=1))
    o_sgu = (u * jnp.concatenate(rows, axis=0) * _silu(s_z)).astype(BF16)

    qb = q.astype(BF16)
    scores = [lax.dot_general(qb[:, hd * XA_HD:(hd + 1) * XA_HD], k_ref[0, :, hd * XA_HD:(hd + 1) * XA_HD],
                              (((1,), (1,)), ((), ())), preferred_element_type=F32) * (XA_HD ** -0.5 * LOG2E)
              for hd in range(XA_HEADS)]
    gates = [proj(7 + n) for n in range(N_BRANCH)]
    heads = []
    for hd in range(XA_HEADS):
        s = scores[hd]
        e = jnp.exp2(s - jnp.max(s, axis=-1, keepdims=True))
        pr = e * (1.0 / jnp.sum(e, axis=-1, keepdims=True))
        heads.append(_dot(pr.astype(BF16), v_ref[0, :, hd * XA_HD:(hd + 1) * XA_HD]))
    o_xa = (jnp.concatenate(heads, axis=1) * _silu(a_z)).astype(BF16)

    merged = None
    for n, o in enumerate((o_pool, o_sgu, o_xa)):
        t = _sigmoid(gates[n]) * _dot(o, wdown_ref[n])
        merged = t if merged is None else merged + t
    xn = x + _dot(merged.astype(BF16), wout_ref[...])
    y_ref[0] = _rmsnorm(xn, gf_ref[...])

    @pl.when(step == pl.num_programs(0) * pl.num_programs(1) - 1)
    def _():
        _last_step(xs_ref, attn_ref, gf_ref, wdown_ref, wout_ref, opool_hbm, osgu_hbm, az_hbm,
                   (g0_hbm, g1_hbm, g2_hbm), stage_ref, last_sem, qs_ref, ys_hbm)


def _prompt_layer(x, kb, vb, gin, win, poolw, pscale, sgug, sguw, sgub_t, wdown, wout, gf,
                  xs_tiles, hwin, sk, sv):
    nb, seq, _ = x.shape
    ts = SEQ_TILE
    n_tiles = seq // ts
    ns = sk.shape[0]
    rb = ns // (nb * n_tiles)
    assert rb * nb * n_tiles == ns
    sample_row = jax.ShapeDtypeStruct((ns, BRANCH_W), F32)
    sample_tiles = jax.ShapeDtypeStruct((ns * ROW_TILES, LANES), F32)
    in_cols = win.shape[1]
    assert in_cols % W_STAGE_COLS == 0 and D_MODEL % W_STAGE_COLS == 0 and D_MODEL % W_STAGE_ROWS == 0
    step = lambda b, i: b * n_tiles + i
    hbm = lambda: pl.BlockSpec(memory_space=pl.ANY)
    return pl.pallas_call(
        _prompt_kernel,
        grid=(nb, n_tiles),
        in_specs=[pl.BlockSpec((1, ts, D_MODEL), lambda b, i: (b, i, 0)),
                  pl.BlockSpec((1, N_MEM, BRANCH_W), lambda b, i: (b, 0, 0)),
                  pl.BlockSpec((1, N_MEM, BRANCH_W), lambda b, i: (b, 0, 0)),
                  _resident((1, D_MODEL)),
                  hbm(),
                  _resident((POOL_GROUPS, POOL_GW, POOL_GW)),
                  _resident((1, BRANCH_W)),
                  _resident((1, BRANCH_W)),
                  _resident((SGU_GROUPS, SGU_CHUNK, SGU_CHUNK)),
                  _resident((SGU_CHUNK, SGU_GROUPS)),
                  hbm(),
                  hbm(),
                  _resident((1, D_MODEL)),
                  _resident((ns * ROW_TILES, LANES)),
                  _resident((ns, BRANCH_W)),
                  pl.BlockSpec((rb, N_MEM, KV_ROWS, LANES), lambda b, i: (step(b, i), 0, 0, 0)),
                  pl.BlockSpec((rb, N_MEM, KV_ROWS, LANES), lambda b, i: (step(b, i), 0, 0, 0))],
        out_specs=[pl.BlockSpec((1, ts, D_MODEL), lambda b, i: (b, i, 0)),
                   pl.BlockSpec((1, HIST_ROWS, BRANCH_W), lambda b, i: (b, 0, 0))] + [hbm()] * 9,
        out_shape=[jax.ShapeDtypeStruct((nb, seq, D_MODEL), F32),
                   jax.ShapeDtypeStruct((nb, HIST_ROWS, BRANCH_W), F32),
                   sample_row, sample_tiles] + [sample_row] * 6 + [sample_tiles],
        scratch_shapes=[pltpu.VMEM((HIST_ROWS + ts, BRANCH_W), F32),
                        pltpu.VMEM((D_MODEL, in_cols), BF16),
                        pltpu.VMEM((N_BRANCH, BRANCH_W, D_MODEL), BF16),
                        pltpu.VMEM((D_MODEL, D_MODEL), BF16),
                        pltpu.VMEM((W_STAGE_SLOTS, W_STAGE_ROWS, W_STAGE_COLS), F32),
                        pltpu.SemaphoreType.DMA((W_STAGE_SLOTS,)),
                        pltpu.VMEM((ns * KV_ROWS, LANES), F32),
                        pltpu.VMEM((ns * KV_ROWS, LANES), F32),
                        pltpu.VMEM((2, ns, BRANCH_W), F32),
                        pltpu.SemaphoreType.DMA((2,)),
                        pltpu.SemaphoreType.DMA((2 * N_BRANCH,))],
        compiler_params=pltpu.CompilerParams(dimension_semantics=("arbitrary", "arbitrary"),
                                             vmem_limit_bytes=V7X_VMEM_LIMIT_BYTES),
        name="prompt_layer",
    )(x, kb, vb, gin, win, poolw, pscale, sgug, sguw, sgub_t, wdown, wout, gf, xs_tiles, hwin, sk, sv)


def _prep_kernel(hist_ref, mem_ref, gmem_ref, wkv_ref,
                 newhist_ref, hwin_ref, k_ref, v_ref, kb_ref, vb_ref, wkv_s):
    @pl.when(pl.program_id(0) == 0)
    def _():
        wkv_s[...] = wkv_ref[...].astype(BF16)
        newhist_ref[0:POOL_HIST - 1] = hist_ref[1:POOL_HIST]
        newhist_ref[POOL_HIST - 1] = jnp.zeros(newhist_ref.shape[1:], F32)
        for g, w in enumerate(POOL_WINDOWS):
            sl = slice(g * POOL_GW, (g + 1) * POOL_GW)
            win = hist_ref[POOL_HIST - 1, :, sl]
            for j in range(2, w):
                win = win + hist_ref[POOL_HIST - j, :, sl]
            hwin_ref[:, sl] = win

    _mem_kv_body(mem_ref, gmem_ref, wkv_s, k_ref, v_ref, kb_ref, vb_ref)


def _prep(hist, mem, gmem, wkv):
    n = hist.shape[1]
    nb = mem.shape[0]
    rows_blk = lambda: pl.BlockSpec((1, N_MEM * KV_ROWS, LANES), lambda b: (b, 0, 0))
    flat_blk = lambda: pl.BlockSpec((1, N_MEM, BRANCH_W), lambda b: (b, 0, 0))
    return pl.pallas_call(
        _prep_kernel,
        grid=(nb,),
        in_specs=[_resident(hist.shape),
                  pl.BlockSpec((1, N_MEM, D_MODEL), lambda b: (b, 0, 0)),
                  _resident((1, D_MODEL)),
                  _resident((D_MODEL, 2 * BRANCH_W))],
        out_specs=[_resident(hist.shape), _resident((n, BRANCH_W)),
                   rows_blk(), rows_blk(), flat_blk(), flat_blk()],
        out_shape=[jax.ShapeDtypeStruct(hist.shape, F32),
                   jax.ShapeDtypeStruct((n, BRANCH_W), F32),
                   jax.ShapeDtypeStruct((nb, N_MEM * KV_ROWS, LANES), F32),
                   jax.ShapeDtypeStruct((nb, N_MEM * KV_ROWS, LANES), F32),
                   jax.ShapeDtypeStruct((nb, N_MEM, BRANCH_W), BF16),
                   jax.ShapeDtypeStruct((nb, N_MEM, BRANCH_W), BF16)],
        scratch_shapes=[pltpu.VMEM((D_MODEL, 2 * BRANCH_W), BF16)],
        compiler_params=pltpu.CompilerParams(dimension_semantics=("arbitrary",),
                                             vmem_limit_bytes=V7X_VMEM_LIMIT_BYTES),
        name="prep",
    )(hist, mem, gmem, wkv)


def kernel(x_prompt, x_sample, state_pool, cache_mem_k, cache_mem_v, mem_prompt, norm_in_g, w_in,
           pool_w, pool_scale, sgu_norm_g, sgu_w, sgu_b, mem_norm_g, w_kv, w_down, w_out, norm_f_g):
    depth = w_in.shape[0]
    assert depth == 1, "single-layer step"
    nb, seq, _ = x_prompt.shape
    ns, dec_seq, _ = x_sample.shape
    assert dec_seq == 1 and seq % SEQ_TILE == 0 and seq >= HIST_ROWS

    row = lambda a: a.reshape(1, -1)
    gin, pscale, sgug, gmem, gf = (row(norm_in_g[0]), row(pool_scale[0]), row(sgu_norm_g[0]),
                                   row(mem_norm_g[0]), row(norm_f_g))
    win, poolw, wkv, wdown, wout, sguw, sgub = (w_in[0], pool_w[0], w_kv[0], w_down[0], w_out[0],
                                                sgu_w[0], sgu_b[0])
    sgub_t = sgub.T

    x_tiles = x_sample.reshape(ns * ROW_TILES, LANES)
    hist_s = jnp.transpose(state_pool[0], (1, 0, 2))
    new_hist_s, hwin_s, k_rows, v_rows, kb, vb = _prep(hist_s, mem_prompt, gmem, wkv)

    y_prompt, hist_p, p_in_s, vn_s, _, _, _, _, _, _, y_tiles = _prompt_layer(
        x_prompt, kb, vb, gin, win, poolw, pscale, sgug, sguw, sgub_t, wdown, wout, gf, x_tiles, hwin_s,
        _to_kv_rows(cache_mem_k[0].reshape(ns, N_MEM, BRANCH_W)),
        _to_kv_rows(cache_mem_v[0].reshape(ns, N_MEM, BRANCH_W)))

    new_pool_p = hist_p[None, :, HIST_ROWS - POOL_HIST:, :]
    new_hist_s = lax.dynamic_update_slice(new_hist_s, p_in_s[None], (POOL_HIST - 1, 0, 0))
    new_pool_s = jnp.transpose(new_hist_s, (1, 0, 2))[None]
    kv_out = lambda a: _from_kv_rows(a.reshape(nb, N_MEM, KV_ROWS, LANES))[None]
    return (y_prompt, y_tiles.reshape(ns, 1, D_MODEL), new_pool_p, new_pool_s,
            kv_out(k_rows), kv_out(v_rows), vn_s.reshape(1, ns, 1, BRANCH_W))
```

```python
import jax
import jax.numpy as jnp
from jax import lax
from jax.experimental import pallas as pl
from jax.experimental.pallas import tpu as pltpu

D_MODEL = 1024
BRANCH_W = 1024
N_BRANCH = 3
N_PROJ = 7 + N_BRANCH
POOL_WINDOWS = (2, 4, 8, 16)
POOL_GROUPS = len(POOL_WINDOWS)
POOL_GW = BRANCH_W // POOL_GROUPS
POOL_HIST = max(POOL_WINDOWS) - 1
HIST_ROWS = POOL_HIST + 1
SGU_CHUNK = 128
SGU_GROUPS = 4
SGU_GW = BRANCH_W // SGU_GROUPS
N_MEM = 256
XA_HEADS = 4
XA_HD = BRANCH_W // XA_HEADS
EPS = 1e-6
PAST_LEN = 16384

SEQ_TILE = 256
LANES = 128
XA_LANE_TILES = XA_HD // LANES
KV_ROWS = XA_HEADS * XA_LANE_TILES
LOG2E = 1.4426950408889634
W_STAGE_SLOTS = 8
W_STAGE_ROWS = 128
W_STAGE_COLS = 1024
V7X_VMEM_LIMIT_BYTES = 62 * 1024 * 1024

F32 = jnp.float32
BF16 = jnp.bfloat16

_sigmoid = jax.nn.sigmoid


def _rmsnorm(x, g):
    return x * lax.rsqrt(jnp.mean(x * x, axis=-1, keepdims=True) + EPS) * g


def _silu(z):
    return z * _sigmoid(z)


def _dot(a, b):
    return jnp.dot(a, b, preferred_element_type=F32)


def _resident(shape):
    zeros = (0,) * len(shape)
    return pl.BlockSpec(shape, lambda *_: zeros, pipeline_mode=pl.Buffered(1))


def _to_kv_rows(a):
    lead = a.shape[:-1]
    a = a.reshape(*lead, XA_HEADS, XA_LANE_TILES, LANES)
    return jnp.swapaxes(a, -3, -2).reshape(*lead, KV_ROWS, LANES)


ROW_TILES = D_MODEL // LANES
NATURAL_ORDER = tuple(range(ROW_TILES))
KV_ROW_ORDER = tuple((j % XA_HEADS) * XA_LANE_TILES + j // XA_HEADS for j in range(KV_ROWS))


def _load_row_tiles(ref, order=NATURAL_ORDER):
    n = ref.shape[0] // len(order)
    tiles = [None] * len(order)
    for j, t in enumerate(order):
        tiles[t] = ref[pl.ds(j, n, stride=len(order)), :]
    return jnp.concatenate(tiles, axis=1)


def _store_row_tiles(ref, val, order=NATURAL_ORDER):
    n = val.shape[0]
    for j, t in enumerate(order):
        ref[pl.ds(j, n, stride=len(order)), :] = val[:, t * LANES:(t + 1) * LANES]


def _from_kv_rows(a):
    lead = a.shape[:-2]
    a = a.reshape(*lead, XA_LANE_TILES, XA_HEADS, LANES)
    return jnp.swapaxes(a, -3, -2).reshape(*lead, XA_HEADS, XA_HD)


def _mem_kv_body(mem_ref, g_ref, wkv_s, k_ref, v_ref, kb_ref, vb_ref):
    m = _rmsnorm(mem_ref[0], g_ref[...]).astype(BF16)
    kv = _dot(m, wkv_s[...])
    k = kv[:, :BRANCH_W]
    v = kv[:, BRANCH_W:]
    kb_ref[0] = k.astype(BF16)
    vb_ref[0] = v.astype(BF16)
    for h in range(XA_HEADS):
        for lt in range(XA_LANE_TILES):
            cols = slice(h * XA_HD + lt * LANES, h * XA_HD + (lt + 1) * LANES)
            rows = pl.ds(lt * XA_HEADS + h, N_MEM, stride=KV_ROWS)
            k_ref[0, rows, :] = k[:, cols]
            v_ref[0, rows, :] = v[:, cols]


def _stage_weights(copies, stage_ref, sem_ref, after_chunk):
    slots = stage_ref.shape[0]
    ahead = slots - 1

    def chunk_copy(k):
        return pltpu.make_async_copy(copies[k][0], stage_ref.at[k % slots], sem_ref.at[k % slots])

    for k in range(min(ahead, len(copies))):
        chunk_copy(k).start()
    for k, (_, dst) in enumerate(copies):
        if k + ahead < len(copies):
            chunk_copy(k + ahead).start()
        chunk_copy(k).wait()
        dst[...] = stage_ref[k % slots].astype(BF16)
        after_chunk(k)


class _HbmWriter:
    def __init__(self, stage_ref, sem_ref):
        self.stage_ref, self.sem_ref, self.pending, self.count = stage_ref, sem_ref, [], 0

    def write(self, value, dst_hbm):
        slots = self.stage_ref.shape[0]
        slot = self.count % slots
        if len(self.pending) == slots:
            self.pending.pop(0).wait()
        self.stage_ref[slot] = value
        copy = pltpu.make_async_copy(self.stage_ref.at[slot], dst_hbm, self.sem_ref.at[slot])
        copy.start()
        self.pending.append(copy)
        self.count += 1

    def finish(self):
        for copy in self.pending:
            copy.wait()
        self.pending = []


def _first_step(xs_ref, hwin_ref, gin_ref, poolw_ref, pscale_ref, sgug_ref, sguw_ref, sgub_ref,
                win_hbm, wdown_hbm, wout_hbm, win_ref, wdown_ref, wout_ref, stage_ref, stage_sem,
                qs_ref, tiles_ref, tiles_sem, ostage_ref, ostage_sem,
                pin_hbm, vn_hbm, opool_hbm, osgu_hbm, az_hbm, gate_hbms):
    _, sr, sc = stage_ref.shape
    assert sc == BRANCH_W and D_MODEL % sr == 0
    k_slabs = D_MODEL // sr
    tiles = lambda ref: [(pl.ds(r * sr, sr), pl.ds(c * sc, sc))
                         for c in range(ref.shape[-1] // sc) for r in range(ref.shape[-2] // sr)]
    copies = [(win_hbm.at[r, c], win_ref.at[r, c]) for r, c in tiles(win_ref)]
    n_win = len(copies)
    copies += [(wdown_hbm.at[n, r, c], wdown_ref.at[n, r, c]) for n in range(N_BRANCH) for r, c in tiles(wdown_ref)]
    copies += [(wout_hbm.at[r, c], wout_ref.at[r, c]) for r, c in tiles(wout_ref)]

    hbs = _rmsnorm(_load_row_tiles(xs_ref), gin_ref[...]).astype(BF16)
    writer = _HbmWriter(ostage_ref, ostage_sem)
    kept = {}

    def finish_chunk(c, val):
        if c == 0:
            kept["p_in"] = val
            writer.write(val, pin_hbm)
        elif c == 1:
            p_in = kept["p_in"]
            mixed = []
            for g, w in enumerate(POOL_WINDOWS):
                sl = slice(g * POOL_GW, (g + 1) * POOL_GW)
                d = (p_in[:, sl] + hwin_ref[:, sl]) / float(min(w, PAST_LEN + 1)) - p_in[:, sl]
                mixed.append(_dot(d.astype(BF16), poolw_ref[g].astype(BF16)))
            writer.write(jnp.concatenate(mixed, axis=1) * pscale_ref[...] * _silu(val), opool_hbm)
        elif c == 2:
            kept["u"] = val
        elif c == 3:
            vn = _rmsnorm(val, sgug_ref[...])
            _store_row_tiles(tiles_ref, vn)
            kept["vn_copy"] = pltpu.make_async_copy(tiles_ref, vn_hbm, tiles_sem.at[0])
            kept["vn_copy"].start()
            gated = [vn[:, g * SGU_GW:(g + 1) * SGU_GW] * sguw_ref[g, 0:1, 0:1] + sgub_ref[0:1, g:g + 1]
                     for g in range(SGU_GROUPS)]
            kept["u_gated"] = kept["u"] * jnp.concatenate(gated, axis=1)
        elif c == 4:
            writer.write(kept["u_gated"] * _silu(val), osgu_hbm)
        elif c == 5:
            _store_row_tiles(qs_ref, val, KV_ROW_ORDER)
        elif c == 6:
            writer.write(val, az_hbm)
        else:
            writer.write(val, gate_hbms[c - 7])

    def after_chunk(k):
        if k >= n_win:
            return
        c, r = divmod(k, k_slabs)
        part = _dot(hbs[:, r * sr:(r + 1) * sr], win_ref[r * sr:(r + 1) * sr, c * sc:(c + 1) * sc])
        kept["acc"] = part if r == 0 else kept["acc"] + part
        if r == k_slabs - 1:
            finish_chunk(c, kept["acc"])

    _stage_weights(copies, stage_ref, stage_sem, after_chunk)
    writer.finish()
    kept["vn_copy"].wait()


def _last_step(xs_ref, attn_ref, gf_ref, wdown_ref, wout_ref, opool_hbm, osgu_hbm, az_hbm, gate_hbms,
               stage_ref, last_sem, tiles_ref, ys_hbm):
    ns = xs_ref.shape[0] // ROW_TILES
    srcs = (opool_hbm, osgu_hbm, az_hbm) + tuple(gate_hbms)
    per_slot = stage_ref.shape[1] // ns
    bufs = [stage_ref.at[k // per_slot, pl.ds((k % per_slot) * ns, ns), :] for k in range(len(srcs))]
    copies = [pltpu.make_async_copy(src, buf, last_sem.at[k]) for k, (src, buf) in enumerate(zip(srcs, bufs))]
    for copy in copies:
        copy.start()
    for copy in copies:
        copy.wait()
    o_pool, o_sgu, a_z, *gates = [buf[...] for buf in bufs]
    o_xa = _load_row_tiles(attn_ref, KV_ROW_ORDER) * _silu(a_z)
    merged = None
    for n, o in enumerate((o_pool, o_sgu, o_xa)):
        t = _sigmoid(gates[n]) * _dot(o.astype(BF16), wdown_ref[n])
        merged = t if merged is None else merged + t
    xn = _load_row_tiles(xs_ref) + _dot(merged.astype(BF16), wout_ref[...])
    _store_row_tiles(tiles_ref, _rmsnorm(xn, gf_ref[...]))
    out = pltpu.make_async_copy(tiles_ref, ys_hbm, last_sem.at[0])
    out.start()
    out.wait()


def _sample_attention(get_q, k_ref, v_ref, put_o):
    for r in range(k_ref.shape[0]):
        q = get_q(r) * (XA_HD ** -0.5 * LOG2E)
        part = k_ref[r] * q
        part = part + pltpu.roll(part, XA_HEADS, axis=1)
        s = jnp.sum(part, axis=-1, keepdims=True)
        e = jnp.exp2(s - jnp.max(s, axis=0, keepdims=True))
        put_o(r, jnp.sum(e * v_ref[r], axis=0) / jnp.sum(e, axis=0))


def _prompt_kernel(x_ref, k_ref, v_ref, gin_ref, win_hbm, poolw_ref, pscale_ref, sgug_ref,
                   sguw_ref, sgub_ref, wdown_hbm, wout_hbm, gf_ref, xs_ref, hwin_ref, sk_ref, sv_ref,
                   y_ref, hist_ref, pin_hbm, vn_hbm, opool_hbm, osgu_hbm, az_hbm, g0_hbm, g1_hbm, g2_hbm, ys_hbm,
                   ext_ref, win_ref, wdown_ref, wout_ref, stage_ref, stage_sem, qs_ref, attn_ref,
                   ostage_ref, ostage_sem, last_sem):
    i = pl.program_id(1)
    ts = x_ref.shape[1]
    step = pl.program_id(0) * pl.num_programs(1) + i

    @pl.when(step == 0)
    def _():
        _first_step(xs_ref, hwin_ref, gin_ref, poolw_ref, pscale_ref, sgug_ref, sguw_ref, sgub_ref,
                    win_hbm, wdown_hbm, wout_hbm, win_ref, wdown_ref, wout_ref, stage_ref, stage_sem,
                    qs_ref, attn_ref, last_sem, ostage_ref, ostage_sem,
                    pin_hbm, vn_hbm, opool_hbm, osgu_hbm, az_hbm, (g0_hbm, g1_hbm, g2_hbm))

    @pl.when(i == 0)
    def _():
        ext_ref[0:HIST_ROWS, :] = jnp.zeros((HIST_ROWS, BRANCH_W), F32)

    q_rows = sk_ref.shape[0] * KV_ROWS
    q_blk = qs_ref[pl.ds(pl.multiple_of(step * q_rows, q_rows), q_rows), :]

    def put_attention(r, out):
        attn_ref[pl.ds(pl.multiple_of(step * q_rows + r * KV_ROWS, KV_ROWS), KV_ROWS), :] = out

    _sample_attention(lambda r: q_blk[r * KV_ROWS:(r + 1) * KV_ROWS, :], sk_ref, sv_ref, put_attention)

    x = x_ref[0]
    hb = _rmsnorm(x, gin_ref[...]).astype(BF16)

    def proj(c):
        return _dot(hb, win_ref[:, c * BRANCH_W:(c + 1) * BRANCH_W])


    p_in = proj(0)
    p_z = proj(1)
    ext_ref[HIST_ROWS:HIST_ROWS + ts, :] = p_in
    pos = i * ts + lax.broadcasted_iota(jnp.int32, (ts, 1), 0)
    pooled = []
    for g, w in enumerate(POOL_WINDOWS):
        sl = slice(g * POOL_GW, (g + 1) * POOL_GW)
        win = ext_ref[:, sl]
        span = 1
        while span < w:
            win = win + pltpu.roll(win, span, axis=0)
            span *= 2
        inv_cnt = 1.0 / jnp.minimum(w, pos + 1).astype(F32)
        pooled.append((win[HIST_ROWS:, :] * inv_cnt - p_in[:, sl]).astype(BF16))
    ext_ref[0:HIST_ROWS, :] = p_in[ts - HIST_ROWS:, :]
    hist_ref[0] = p_in[ts - HIST_ROWS:, :]

    v = proj(3)
    u = proj(2)
    s_z = proj(4)
    mixed = [_dot(pooled[g], poolw_ref[g].astype(BF16)) for g in range(POOL_GROUPS)]
    o_pool = (jnp.concatenate(mixed, axis=1) * pscale_ref[...] * _silu(p_z)).astype(BF16)

    vnb = _rmsnorm(v, sgug_ref[...]).astype(BF16)
    tril = (lax.broadcasted_iota(jnp.int32, (SGU_CHUNK, SGU_CHUNK), 0)
            >= lax.broadcasted_iota(jnp.int32, (SGU_CHUNK, SGU_CHUNK), 1))
    ws = [jnp.where(tril, sguw_ref[g], 0.0).astype(BF16) for g in range(SGU_GROUPS)]
    q = proj(5)
    a_z = proj(6)
    rows = []
    for c in range(ts // SGU_CHUNK):
        rs = slice(c * SGU_CHUNK, (c + 1) * SGU_CHUNK)
        cols = [_dot(ws[g], vnb[rs, g * SGU_GW:(g + 1) * SGU_GW]) + sgub_ref[:, g:g + 1]
                for g in range(SGU_GROUPS)]
        rows.append(jnp.concatenate(cols, axis=1))
    o_sgu = (u * jnp.concatenate(rows, axis=0) * _silu(s_z)).astype(BF16)

    qb = q.astype(BF16)
    scores = [lax.dot_general(qb[:, hd * XA_HD:(hd + 1) * XA_HD], k_ref[0, :, hd * XA_HD:(hd + 1) * XA_HD],
                              (((1,), (1,)), ((), ())), preferred_element_type=F32) * (XA_HD ** -0.5 * LOG2E)
              for hd in range(XA_HEADS)]
    gates = [proj(7 + n) for n in range(N_BRANCH)]
    heads = []
    for hd in range(XA_HEADS):
        s = scores[hd]
        e = jnp.exp2(s - jnp.max(s, axis=-1, keepdims=True))
        pr = e * (1.0 / jnp.sum(e, axis=-1, keepdims=True))
        heads.append(_dot(pr.astype(BF16), v_ref[0, :, hd * XA_HD:(hd + 1) * XA_HD]))
    o_xa = (jnp.concatenate(heads, axis=1) * _silu(a_z)).astype(BF16)

    merged = None
    for n, o in enumerate((o_pool, o_sgu, o_xa)):
        t = _sigmoid(gates[n]) * _dot(o, wdown_ref[n])
        merged = t if merged is None else merged + t
    xn = x + _dot(merged.astype(BF16), wout_ref[...])
    y_ref[0] = _rmsnorm(xn, gf_ref[...])

    @pl.when(step == pl.num_programs(0) * pl.num_programs(1) - 1)
    def _():
        _last_step(xs_ref, attn_ref, gf_ref, wdown_ref, wout_ref, opool_hbm, osgu_hbm, az_hbm,
                   (g0_hbm, g1_hbm, g2_hbm), stage_ref, last_sem, qs_ref, ys_hbm)


def _prompt_layer(x, kb, vb, gin, win, poolw, pscale, sgug, sguw, sgub_t, wdown, wout, gf,
                  xs_tiles, hwin, sk, sv):
    nb, seq, _ = x.shape
    ts = SEQ_TILE
    n_tiles = seq // ts
    ns = sk.shape[0]
    rb = ns // (nb * n_tiles)
    assert rb * nb * n_tiles == ns
    sample_row = jax.ShapeDtypeStruct((ns, BRANCH_W), F32)
    sample_tiles = jax.ShapeDtypeStruct((ns * ROW_TILES, LANES), F32)
    in_cols = win.shape[1]
    assert in_cols % W_STAGE_COLS == 0 and D_MODEL % W_STAGE_COLS == 0 and D_MODEL % W_STAGE_ROWS == 0
    step = lambda b, i: b * n_tiles + i
    hbm = lambda: pl.BlockSpec(memory_space=pl.ANY)
    return pl.pallas_call(
        _prompt_kernel,
        grid=(nb, n_tiles),
        in_specs=[pl.BlockSpec((1, ts, D_MODEL), lambda b, i: (b, i, 0)),
                  pl.BlockSpec((1, N_MEM, BRANCH_W), lambda b, i: (b, 0, 0)),
                  pl.BlockSpec((1, N_MEM, BRANCH_W), lambda b, i: (b, 0, 0)),
                  _resident((1, D_MODEL)),
                  hbm(),
                  _resident((POOL_GROUPS, POOL_GW, POOL_GW)),
                  _resident((1, BRANCH_W)),
                  _resident((1, BRANCH_W)),
                  _resident((SGU_GROUPS, SGU_CHUNK, SGU_CHUNK)),
                  _resident((SGU_CHUNK, SGU_GROUPS)),
                  hbm(),
                  hbm(),
                  _resident((1, D_MODEL)),
                  _resident((ns * ROW_TILES, LANES)),
                  _resident((ns, BRANCH_W)),
                  pl.BlockSpec((rb, N_MEM, KV_ROWS, LANES), lambda b, i: (step(b, i), 0, 0, 0)),
                  pl.BlockSpec((rb, N_MEM, KV_ROWS, LANES), lambda b, i: (step(b, i), 0, 0, 0))],
        out_specs=[pl.BlockSpec((1, ts, D_MODEL), lambda b, i: (b, i, 0)),
                   pl.BlockSpec((1, HIST_ROWS, BRANCH_W), lambda b, i: (b, 0, 0))] + [hbm()] * 9,
        out_shape=[jax.ShapeDtypeStruct((nb, seq, D_MODEL), F32),
                   jax.ShapeDtypeStruct((nb, HIST_ROWS, BRANCH_W), F32),
                   sample_row, sample_tiles] + [sample_row] * 6 + [sample_tiles],
        scratch_shapes=[pltpu.VMEM((HIST_ROWS + ts, BRANCH_W), F32),
                        pltpu.VMEM((D_MODEL, in_cols), BF16),
                        pltpu.VMEM((N_BRANCH, BRANCH_W, D_MODEL), BF16),
                        pltpu.VMEM((D_MODEL, D_MODEL), BF16),
                        pltpu.VMEM((W_STAGE_SLOTS, W_STAGE_ROWS, W_STAGE_COLS), F32),
                        pltpu.SemaphoreType.DMA((W_STAGE_SLOTS,)),
                        pltpu.VMEM((ns * KV_ROWS, LANES), F32),
                        pltpu.VMEM((ns * KV_ROWS, LANES), F32),
                        pltpu.VMEM((2, ns, BRANCH_W), F32),
                        pltpu.SemaphoreType.DMA((2,)),
                        pltpu.SemaphoreType.DMA((2 * N_BRANCH,))],
        compiler_params=pltpu.CompilerParams(dimension_semantics=("arbitrary", "arbitrary"),
                                             vmem_limit_bytes=V7X_VMEM_LIMIT_BYTES),
        name="prompt_layer",
    )(x, kb, vb, gin, win, poolw, pscale, sgug, sguw, sgub_t, wdown, wout, gf, xs_tiles, hwin, sk, sv)


def _prep_kernel(hist_hbm, mem_ref, gmem_ref, wkv_ref,
                 newhist_hbm, hwin_ref, k_ref, v_ref, kb_ref, vb_ref, wkv_s, hist_s, hist_sem):
    c = pl.program_id(0)
    load = pltpu.make_async_copy(hist_hbm, hist_s, hist_sem.at[0])
    roll = pltpu.make_async_copy(hist_s.at[pl.ds(1, POOL_HIST - 1)], newhist_hbm.at[pl.ds(0, POOL_HIST - 1)],
                                 hist_sem.at[1])
    fill = pltpu.make_async_copy(hist_s.at[0], newhist_hbm.at[POOL_HIST - 1], hist_sem.at[2])

    @pl.when(c == 0)
    def _():
        load.start()
        wkv_s[...] = wkv_ref[...].astype(BF16)

    @pl.when(c == 1)
    def _():
        load.wait()
        for g, w in enumerate(POOL_WINDOWS):
            sl = slice(g * POOL_GW, (g + 1) * POOL_GW)
            win = hist_s[POOL_HIST - 1, :, sl]
            for j in range(2, w):
                win = win + hist_s[POOL_HIST - j, :, sl]
            hwin_ref[:, sl] = win
        roll.start()
        hist_s[0] = jnp.zeros(hist_s.shape[1:], F32)
        fill.start()

    _mem_kv_body(mem_ref, gmem_ref, wkv_s, k_ref, v_ref, kb_ref, vb_ref)

    @pl.when(c == pl.num_programs(0) - 1)
    def _():
        roll.wait()
        fill.wait()


def _prep(hist, mem, gmem, wkv):
    n = hist.shape[1]
    nb = mem.shape[0]
    assert nb >= 3
    rows_blk = lambda: pl.BlockSpec((1, N_MEM * KV_ROWS, LANES), lambda b: (b, 0, 0))
    flat_blk = lambda: pl.BlockSpec((1, N_MEM, BRANCH_W), lambda b: (b, 0, 0))
    hbm = lambda: pl.BlockSpec(memory_space=pl.ANY)
    return pl.pallas_call(
        _prep_kernel,
        grid=(nb,),
        in_specs=[hbm(),
                  pl.BlockSpec((1, N_MEM, D_MODEL), lambda b: (b, 0, 0)),
                  _resident((1, D_MODEL)),
                  _resident((D_MODEL, 2 * BRANCH_W))],
        out_specs=[hbm(), _resident((n, BRANCH_W)),
                   rows_blk(), rows_blk(), flat_blk(), flat_blk()],
        out_shape=[jax.ShapeDtypeStruct(hist.shape, F32),
                   jax.ShapeDtypeStruct((n, BRANCH_W), F32),
                   jax.ShapeDtypeStruct((nb, N_MEM * KV_ROWS, LANES), F32),
                   jax.ShapeDtypeStruct((nb, N_MEM * KV_ROWS, LANES), F32),
                   jax.ShapeDtypeStruct((nb, N_MEM, BRANCH_W), BF16),
                   jax.ShapeDtypeStruct((nb, N_MEM, BRANCH_W), BF16)],
        scratch_shapes=[pltpu.VMEM((D_MODEL, 2 * BRANCH_W), BF16),
                        pltpu.VMEM(hist.shape, F32),
                        pltpu.SemaphoreType.DMA((3,))],
        compiler_params=pltpu.CompilerParams(dimension_semantics=("arbitrary",),
                                             vmem_limit_bytes=V7X_VMEM_LIMIT_BYTES),
        name="prep",
    )(hist, mem, gmem, wkv)


def kernel(x_prompt, x_sample, state_pool, cache_mem_k, cache_mem_v, mem_prompt, norm_in_g, w_in,
           pool_w, pool_scale, sgu_norm_g, sgu_w, sgu_b, mem_norm_g, w_kv, w_down, w_out, norm_f_g):
    depth = w_in.shape[0]
    assert depth == 1, "single-layer step"
    nb, seq, _ = x_prompt.shape
    ns, dec_seq, _ = x_sample.shape
    assert dec_seq == 1 and seq % SEQ_TILE == 0 and seq >= HIST_ROWS

    row = lambda a: a.reshape(1, -1)
    gin, pscale, sgug, gmem, gf = (row(norm_in_g[0]), row(pool_scale[0]), row(sgu_norm_g[0]),
                                   row(mem_norm_g[0]), row(norm_f_g))
    win, poolw, wkv, wdown, wout, sguw, sgub = (w_in[0], pool_w[0], w_kv[0], w_down[0], w_out[0],
                                                sgu_w[0], sgu_b[0])
    sgub_t = sgub.T

    x_tiles = x_sample.reshape(ns * ROW_TILES, LANES)
    hist_s = jnp.transpose(state_pool[0], (1, 0, 2))
    new_hist_s, hwin_s, k_rows, v_rows, kb, vb = _prep(hist_s, mem_prompt, gmem, wkv)

    y_prompt, hist_p, p_in_s, vn_s, _, _, _, _, _, _, y_tiles = _prompt_layer(
        x_prompt, kb, vb, gin, win, poolw, pscale, sgug, sguw, sgub_t, wdown, wout, gf, x_tiles, hwin_s,
        _to_kv_rows(cache_mem_k[0].reshape(ns, N_MEM, BRANCH_W)),
        _to_kv_rows(cache_mem_v[0].reshape(ns, N_MEM, BRANCH_W)))

    new_pool_p = hist_p[None, :, HIST_ROWS - POOL_HIST:, :]
    new_hist_s = lax.dynamic_update_slice(new_hist_s, p_in_s[None], (POOL_HIST - 1, 0, 0))
    new_pool_s = jnp.transpose(new_hist_s, (1, 0, 2))[None]
    kv_out = lambda a: _from_kv_rows(a.reshape(nb, N_MEM, KV_ROWS, LANES))[None]
    return (y_prompt, y_tiles.reshape(ns, 1, D_MODEL), new_pool_p, new_pool_s,
            kv_out(k_rows), kv_out(v_rows), vn_s.reshape(1, ns, 1, BRANCH_W))
```

```python
import jax
import jax.numpy as jnp
from jax import lax
from jax.experimental import pallas as pl
from jax.experimental.pallas import tpu as pltpu

D_MODEL = 1024
BRANCH_W = 1024
N_BRANCH = 3
N_PROJ = 7 + N_BRANCH
POOL_WINDOWS = (2, 4, 8, 16)
POOL_GROUPS = len(POOL_WINDOWS)
POOL_GW = BRANCH_W // POOL_GROUPS
POOL_HIST = max(POOL_WINDOWS) - 1
HIST_ROWS = POOL_HIST + 1
SGU_CHUNK = 128
SGU_GROUPS = 4
SGU_GW = BRANCH_W // SGU_GROUPS
N_MEM = 256
XA_HEADS = 4
XA_HD = BRANCH_W // XA_HEADS
EPS = 1e-6
PAST_LEN = 16384

SEQ_TILE = 256
MEM_REQ_BLOCK = 2
LANES = 128
XA_LANE_TILES = XA_HD // LANES
KV_ROWS = XA_HEADS * XA_LANE_TILES
LOG2E = 1.4426950408889634
W_STAGE_SLOTS = 8
W_STAGE_ROWS = 128
W_STAGE_COLS = 1024
V7X_VMEM_LIMIT_BYTES = 62 * 1024 * 1024

F32 = jnp.float32
BF16 = jnp.bfloat16

_sigmoid = jax.nn.sigmoid


def _rmsnorm(x, g):
    return x * lax.rsqrt(jnp.mean(x * x, axis=-1, keepdims=True) + EPS) * g


def _silu(z):
    return z * _sigmoid(z)


def _dot(a, b):
    return jnp.dot(a, b, preferred_element_type=F32)


def _resident(shape):
    zeros = (0,) * len(shape)
    return pl.BlockSpec(shape, lambda *_: zeros, pipeline_mode=pl.Buffered(1))


def _to_kv_rows(a):
    lead = a.shape[:-1]
    a = a.reshape(*lead, XA_HEADS, XA_LANE_TILES, LANES)
    return jnp.swapaxes(a, -3, -2).reshape(*lead, KV_ROWS, LANES)


ROW_TILES = D_MODEL // LANES
NATURAL_ORDER = tuple(range(ROW_TILES))
KV_ROW_ORDER = tuple((j % XA_HEADS) * XA_LANE_TILES + j // XA_HEADS for j in range(KV_ROWS))


def _load_row_tiles(ref, order=NATURAL_ORDER):
    n = ref.shape[0] // len(order)
    tiles = [None] * len(order)
    for j, t in enumerate(order):
        tiles[t] = ref[pl.ds(j, n, stride=len(order)), :]
    return jnp.concatenate(tiles, axis=1)


def _store_row_tiles(ref, val, order=NATURAL_ORDER):
    n = val.shape[0]
    for j, t in enumerate(order):
        ref[pl.ds(j, n, stride=len(order)), :] = val[:, t * LANES:(t + 1) * LANES]


def _from_kv_rows(a):
    lead = a.shape[:-2]
    a = a.reshape(*lead, XA_LANE_TILES, XA_HEADS, LANES)
    return jnp.swapaxes(a, -3, -2).reshape(*lead, XA_HEADS, XA_HD)


def _mem_kv_body(mem_ref, g_ref, wkv_s, k_ref, v_ref, kb_ref, vb_ref):
    n_req = mem_ref.shape[0]
    mem = jnp.concatenate([mem_ref[r] for r in range(n_req)], axis=0)
    kv = _dot(_rmsnorm(mem, g_ref[...]).astype(BF16), wkv_s[...])
    for r in range(n_req):
        k = kv[r * N_MEM:(r + 1) * N_MEM, :BRANCH_W]
        v = kv[r * N_MEM:(r + 1) * N_MEM, BRANCH_W:]
        kb_ref[r] = k.astype(BF16)
        vb_ref[r] = v.astype(BF16)
        for h in range(XA_HEADS):
            for lt in range(XA_LANE_TILES):
                cols = slice(h * XA_HD + lt * LANES, h * XA_HD + (lt + 1) * LANES)
                rows = pl.ds(lt * XA_HEADS + h, N_MEM, stride=KV_ROWS)
                k_ref[r, rows, :] = k[:, cols]
                v_ref[r, rows, :] = v[:, cols]


def _stage_weights(copies, stage_ref, sem_ref, after_chunk):
    slots = stage_ref.shape[0]
    ahead = slots - 1

    def chunk_copy(k):
        return pltpu.make_async_copy(copies[k][0], stage_ref.at[k % slots], sem_ref.at[k % slots])

    for k in range(min(ahead, len(copies))):
        chunk_copy(k).start()
    for k, (_, dst) in enumerate(copies):
        if k + ahead < len(copies):
            chunk_copy(k + ahead).start()
        chunk_copy(k).wait()
        dst[...] = stage_ref[k % slots].astype(BF16)
        after_chunk(k)


class _HbmWriter:
    def __init__(self, stage_ref, sem_ref):
        self.stage_ref, self.sem_ref, self.pending, self.count = stage_ref, sem_ref, [], 0

    def write(self, value, dst_hbm):
        slots = self.stage_ref.shape[0]
        slot = self.count % slots
        if len(self.pending) == slots:
            self.pending.pop(0).wait()
        self.stage_ref[slot] = value
        copy = pltpu.make_async_copy(self.stage_ref.at[slot], dst_hbm, self.sem_ref.at[slot])
        copy.start()
        self.pending.append(copy)
        self.count += 1

    def finish(self):
        for copy in self.pending:
            copy.wait()
        self.pending = []


def _first_step(xs_ref, hwin_ref, gin_ref, poolw_ref, pscale_ref, sgug_ref, sguw_ref, sgub_ref,
                win_hbm, wdown_hbm, wout_hbm, win_ref, wdown_ref, wout_ref, stage_ref, stage_sem,
                qs_ref, tiles_ref, tiles_sem, ostage_ref, ostage_sem,
                pin_hbm, vn_hbm, opool_hbm, osgu_hbm, az_hbm, gate_hbms):
    _, sr, sc = stage_ref.shape
    assert sc == BRANCH_W and D_MODEL % sr == 0
    k_slabs = D_MODEL // sr
    tiles = lambda ref: [(pl.ds(r * sr, sr), pl.ds(c * sc, sc))
                         for c in range(ref.shape[-1] // sc) for r in range(ref.shape[-2] // sr)]
    copies = [(win_hbm.at[r, c], win_ref.at[r, c]) for r, c in tiles(win_ref)]
    n_win = len(copies)
    copies += [(wdown_hbm.at[n, r, c], wdown_ref.at[n, r, c]) for n in range(N_BRANCH) for r, c in tiles(wdown_ref)]
    copies += [(wout_hbm.at[r, c], wout_ref.at[r, c]) for r, c in tiles(wout_ref)]

    hbs = _rmsnorm(_load_row_tiles(xs_ref), gin_ref[...]).astype(BF16)
    writer = _HbmWriter(ostage_ref, ostage_sem)
    kept = {}

    def finish_chunk(c, val):
        if c == 0:
            kept["p_in"] = val
            writer.write(val, pin_hbm)
        elif c == 1:
            p_in = kept["p_in"]
            mixed = []
            for g, w in enumerate(POOL_WINDOWS):
                sl = slice(g * POOL_GW, (g + 1) * POOL_GW)
                d = (p_in[:, sl] + hwin_ref[:, sl]) / float(min(w, PAST_LEN + 1)) - p_in[:, sl]
                mixed.append(_dot(d.astype(BF16), poolw_ref[g].astype(BF16)))
            writer.write(jnp.concatenate(mixed, axis=1) * pscale_ref[...] * _silu(val), opool_hbm)
        elif c == 2:
            kept["u"] = val
        elif c == 3:
            vn = _rmsnorm(val, sgug_ref[...])
            _store_row_tiles(tiles_ref, vn)
            kept["vn_copy"] = pltpu.make_async_copy(tiles_ref, vn_hbm, tiles_sem.at[0])
            kept["vn_copy"].start()
            gated = [vn[:, g * SGU_GW:(g + 1) * SGU_GW] * sguw_ref[g, 0:1, 0:1] + sgub_ref[0:1, g:g + 1]
                     for g in range(SGU_GROUPS)]
            kept["u_gated"] = kept["u"] * jnp.concatenate(gated, axis=1)
        elif c == 4:
            writer.write(kept["u_gated"] * _silu(val), osgu_hbm)
        elif c == 5:
            _store_row_tiles(qs_ref, val, KV_ROW_ORDER)
        elif c == 6:
            writer.write(val, az_hbm)
        else:
            writer.write(val, gate_hbms[c - 7])

    def after_chunk(k):
        if k >= n_win:
            return
        c, r = divmod(k, k_slabs)
        part = _dot(hbs[:, r * sr:(r + 1) * sr], win_ref[r * sr:(r + 1) * sr, c * sc:(c + 1) * sc])
        kept["acc"] = part if r == 0 else kept["acc"] + part
        if r == k_slabs - 1:
            finish_chunk(c, kept["acc"])

    _stage_weights(copies, stage_ref, stage_sem, after_chunk)
    writer.finish()
    kept["vn_copy"].wait()


def _last_step(xs_ref, attn_ref, gf_ref, wdown_ref, wout_ref, opool_hbm, osgu_hbm, az_hbm, gate_hbms,
               stage_ref, last_sem, tiles_ref, ys_hbm):
    ns = xs_ref.shape[0] // ROW_TILES
    srcs = (opool_hbm, osgu_hbm, az_hbm) + tuple(gate_hbms)
    per_slot = stage_ref.shape[1] // ns
    bufs = [stage_ref.at[k // per_slot, pl.ds((k % per_slot) * ns, ns), :] for k in range(len(srcs))]
    copies = [pltpu.make_async_copy(src, buf, last_sem.at[k]) for k, (src, buf) in enumerate(zip(srcs, bufs))]
    for copy in copies:
        copy.start()
    for copy in copies:
        copy.wait()
    o_pool, o_sgu, a_z, *gates = [buf[...] for buf in bufs]
    o_xa = _load_row_tiles(attn_ref, KV_ROW_ORDER) * _silu(a_z)
    merged = None
    for n, o in enumerate((o_pool, o_sgu, o_xa)):
        t = _sigmoid(gates[n]) * _dot(o.astype(BF16), wdown_ref[n])
        merged = t if merged is None else merged + t
    xn = _load_row_tiles(xs_ref) + _dot(merged.astype(BF16), wout_ref[...])
    _store_row_tiles(tiles_ref, _rmsnorm(xn, gf_ref[...]))
    out = pltpu.make_async_copy(tiles_ref, ys_hbm, last_sem.at[0])
    out.start()
    out.wait()


def _sample_attention(get_q, k_ref, v_ref, put_o):
    for r in range(k_ref.shape[0]):
        q = get_q(r) * (XA_HD ** -0.5 * LOG2E)
        part = k_ref[r] * q
        part = part + pltpu.roll(part, XA_HEADS, axis=1)
        s = jnp.sum(part, axis=-1, keepdims=True)
        e = jnp.exp2(s - jnp.max(s, axis=0, keepdims=True))
        put_o(r, jnp.sum(e * v_ref[r], axis=0) / jnp.sum(e, axis=0))


def _prompt_kernel(x_ref, k_ref, v_ref, gin_ref, win_hbm, poolw_ref, pscale_ref, sgug_ref,
                   sguw_ref, sgub_ref, wdown_hbm, wout_hbm, gf_ref, xs_ref, hwin_ref, sk_ref, sv_ref,
                   y_ref, hist_ref, pin_hbm, vn_hbm, opool_hbm, osgu_hbm, az_hbm, g0_hbm, g1_hbm, g2_hbm, ys_hbm,
                   ext_ref, win_ref, wdown_ref, wout_ref, stage_ref, stage_sem, qs_ref, attn_ref,
                   ostage_ref, ostage_sem, last_sem):
    i = pl.program_id(1)
    ts = x_ref.shape[1]
    step = pl.program_id(0) * pl.num_programs(1) + i

    @pl.when(step == 0)
    def _():
        _first_step(xs_ref, hwin_ref, gin_ref, poolw_ref, pscale_ref, sgug_ref, sguw_ref, sgub_ref,
                    win_hbm, wdown_hbm, wout_hbm, win_ref, wdown_ref, wout_ref, stage_ref, stage_sem,
                    qs_ref, attn_ref, last_sem, ostage_ref, ostage_sem,
                    pin_hbm, vn_hbm, opool_hbm, osgu_hbm, az_hbm, (g0_hbm, g1_hbm, g2_hbm))

    @pl.when(i == 0)
    def _():
        ext_ref[0:HIST_ROWS, :] = jnp.zeros((HIST_ROWS, BRANCH_W), F32)

    q_rows = sk_ref.shape[0] * KV_ROWS
    q_blk = qs_ref[pl.ds(pl.multiple_of(step * q_rows, q_rows), q_rows), :]

    def put_attention(r, out):
        attn_ref[pl.ds(pl.multiple_of(step * q_rows + r * KV_ROWS, KV_ROWS), KV_ROWS), :] = out

    _sample_attention(lambda r: q_blk[r * KV_ROWS:(r + 1) * KV_ROWS, :], sk_ref, sv_ref, put_attention)

    x = x_ref[0]
    hb = _rmsnorm(x, gin_ref[...]).astype(BF16)

    def proj(c):
        return _dot(hb, win_ref[:, c * BRANCH_W:(c + 1) * BRANCH_W])


    p_in = proj(0)
    p_z = proj(1)
    ext_ref[HIST_ROWS:HIST_ROWS + ts, :] = p_in
    pos = i * ts + lax.broadcasted_iota(jnp.int32, (ts, 1), 0)
    pooled = []
    for g, w in enumerate(POOL_WINDOWS):
        sl = slice(g * POOL_GW, (g + 1) * POOL_GW)
        win = ext_ref[:, sl]
        span = 1
        while span < w:
            win = win + pltpu.roll(win, span, axis=0)
            span *= 2
        inv_cnt = 1.0 / jnp.minimum(w, pos + 1).astype(F32)
        pooled.append((win[HIST_ROWS:, :] * inv_cnt - p_in[:, sl]).astype(BF16))
    ext_ref[0:HIST_ROWS, :] = p_in[ts - HIST_ROWS:, :]
    hist_ref[0] = p_in[ts - HIST_ROWS:, :]

    v = proj(3)
    u = proj(2)
    s_z = proj(4)
    mixed = [_dot(pooled[g], poolw_ref[g].astype(BF16)) for g in range(POOL_GROUPS)]
    o_pool = (jnp.concatenate(mixed, axis=1) * pscale_ref[...] * _silu(p_z)).astype(BF16)

    vnb = _rmsnorm(v, sgug_ref[...]).astype(BF16)
    tril = (lax.broadcasted_iota(jnp.int32, (SGU_CHUNK, SGU_CHUNK), 0)
            >= lax.broadcasted_iota(jnp.int32, (SGU_CHUNK, SGU_CHUNK), 1))
    ws = [jnp.where(tril, sguw_ref[g], 0.0).astype(BF16) for g in range(SGU_GROUPS)]
    q = proj(5)
    a_z = proj(6)
    rows = []
    for c in range(ts // SGU_CHUNK):
        rs = slice(c * SGU_CHUNK, (c + 1) * SGU_CHUNK)
        cols = [_dot(ws[g], vnb[rs, g * SGU_GW:(g + 1) * SGU_GW]) + sgub_ref[:, g:g + 1]
                for g in range(SGU_GROUPS)]
        rows.append(jnp.concatenate(cols, axis=1))
    o_sgu = (u * jnp.concatenate(rows, axis=0) * _silu(s_z)).astype(BF16)

    qb = q.astype(BF16)
    scores = [lax.dot_general(qb[:, hd * XA_HD:(hd + 1) * XA_HD], k_ref[0, :, hd * XA_HD:(hd + 1) * XA_HD],
                              (((1,), (1,)), ((), ())), preferred_element_type=F32) * (XA_HD ** -0.5 * LOG2E)
              for hd in range(XA_HEADS)]
    gates = [proj(7 + n) for n in range(N_BRANCH)]
    heads = []
    for hd in range(XA_HEADS):
        s = scores[hd]
        e = jnp.exp2(s - jnp.max(s, axis=-1, keepdims=True))
        pr = e * (1.0 / jnp.sum(e, axis=-1, keepdims=True))
        heads.append(_dot(pr.astype(BF16), v_ref[0, :, hd * XA_HD:(hd + 1) * XA_HD]))
    o_xa = (jnp.concatenate(heads, axis=1) * _silu(a_z)).astype(BF16)

    merged = None
    for n, o in enumerate((o_pool, o_sgu, o_xa)):
        t = _sigmoid(gates[n]) * _dot(o, wdown_ref[n])
        merged = t if merged is None else merged + t
    xn = x + _dot(merged.astype(BF16), wout_ref[...])
    y_ref[0] = _rmsnorm(xn, gf_ref[...])

    @pl.when(step == pl.num_programs(0) * pl.num_programs(1) - 1)
    def _():
        _last_step(xs_ref, attn_ref, gf_ref, wdown_ref, wout_ref, opool_hbm, osgu_hbm, az_hbm,
                   (g0_hbm, g1_hbm, g2_hbm), stage_ref, last_sem, qs_ref, ys_hbm)


def _prompt_layer(x, kb, vb, gin, win, poolw, pscale, sgug, sguw, sgub_t, wdown, wout, gf,
                  xs_tiles, hwin, sk, sv):
    nb, seq, _ = x.shape
    ts = SEQ_TILE
    n_tiles = seq // ts
    ns = sk.shape[0]
    rb = ns // (nb * n_tiles)
    assert rb * nb * n_tiles == ns
    sample_row = jax.ShapeDtypeStruct((ns, BRANCH_W), F32)
    sample_tiles = jax.ShapeDtypeStruct((ns * ROW_TILES, LANES), F32)
    in_cols = win.shape[1]
    assert in_cols % W_STAGE_COLS == 0 and D_MODEL % W_STAGE_COLS == 0 and D_MODEL % W_STAGE_ROWS == 0
    step = lambda b, i: b * n_tiles + i
    hbm = lambda: pl.BlockSpec(memory_space=pl.ANY)
    return pl.pallas_call(
        _prompt_kernel,
        grid=(nb, n_tiles),
        in_specs=[pl.BlockSpec((1, ts, D_MODEL), lambda b, i: (b, i, 0)),
                  pl.BlockSpec((1, N_MEM, BRANCH_W), lambda b, i: (b, 0, 0)),
                  pl.BlockSpec((1, N_MEM, BRANCH_W), lambda b, i: (b, 0, 0)),
                  _resident((1, D_MODEL)),
                  hbm(),
                  _resident((POOL_GROUPS, POOL_GW, POOL_GW)),
                  _resident((1, BRANCH_W)),
                  _resident((1, BRANCH_W)),
                  _resident((SGU_GROUPS, SGU_CHUNK, SGU_CHUNK)),
                  _resident((SGU_CHUNK, SGU_GROUPS)),
                  hbm(),
                  hbm(),
                  _resident((1, D_MODEL)),
                  _resident((ns * ROW_TILES, LANES)),
                  _resident((ns, BRANCH_W)),
                  pl.BlockSpec((rb, N_MEM, KV_ROWS, LANES), lambda b, i: (step(b, i), 0, 0, 0)),
                  pl.BlockSpec((rb, N_MEM, KV_ROWS, LANES), lambda b, i: (step(b, i), 0, 0, 0))],
        out_specs=[pl.BlockSpec((1, ts, D_MODEL), lambda b, i: (b, i, 0)),
                   pl.BlockSpec((1, HIST_ROWS, BRANCH_W), lambda b, i: (b, 0, 0))] + [hbm()] * 9,
        out_shape=[jax.ShapeDtypeStruct((nb, seq, D_MODEL), F32),
                   jax.ShapeDtypeStruct((nb, HIST_ROWS, BRANCH_W), F32),
                   sample_row, sample_tiles] + [sample_row] * 6 + [sample_tiles],
        scratch_shapes=[pltpu.VMEM((HIST_ROWS + ts, BRANCH_W), F32),
                        pltpu.VMEM((D_MODEL, in_cols), BF16),
                        pltpu.VMEM((N_BRANCH, BRANCH_W, D_MODEL), BF16),
                        pltpu.VMEM((D_MODEL, D_MODEL), BF16),
                        pltpu.VMEM((W_STAGE_SLOTS, W_STAGE_ROWS, W_STAGE_COLS), F32),
                        pltpu.SemaphoreType.DMA((W_STAGE_SLOTS,)),
                        pltpu.VMEM((ns * KV_ROWS, LANES), F32),
                        pltpu.VMEM((ns * KV_ROWS, LANES), F32),
                        pltpu.VMEM((2, ns, BRANCH_W), F32),
                        pltpu.SemaphoreType.DMA((2,)),
                        pltpu.SemaphoreType.DMA((2 * N_BRANCH,))],
        compiler_params=pltpu.CompilerParams(dimension_semantics=("arbitrary", "arbitrary"),
                                             vmem_limit_bytes=V7X_VMEM_LIMIT_BYTES),
        name="prompt_layer",
    )(x, kb, vb, gin, win, poolw, pscale, sgug, sguw, sgub_t, wdown, wout, gf, xs_tiles, hwin, sk, sv)


def _prep_kernel(hist_hbm, mem_ref, gmem_ref, wkv_ref,
                 newhist_hbm, hwin_ref, k_ref, v_ref, kb_ref, vb_ref, wkv_s, hist_s, hist_sem):
    c = pl.program_id(0)
    load = pltpu.make_async_copy(hist_hbm, hist_s, hist_sem.at[0])
    roll = pltpu.make_async_copy(hist_s.at[pl.ds(1, POOL_HIST - 1)], newhist_hbm.at[pl.ds(0, POOL_HIST - 1)],
                                 hist_sem.at[1])
    fill = pltpu.make_async_copy(hist_s.at[0], newhist_hbm.at[POOL_HIST - 1], hist_sem.at[2])

    @pl.when(c == 0)
    def _():
        load.start()
        wkv_s[...] = wkv_ref[...].astype(BF16)

    @pl.when(c == 1)
    def _():
        load.wait()
        for g, w in enumerate(POOL_WINDOWS):
            sl = slice(g * POOL_GW, (g + 1) * POOL_GW)
            win = hist_s[POOL_HIST - 1, :, sl]
            for j in range(2, w):
                win = win + hist_s[POOL_HIST - j, :, sl]
            hwin_ref[:, sl] = win
        roll.start()
        hist_s[0] = jnp.zeros(hist_s.shape[1:], F32)
        fill.start()

    _mem_kv_body(mem_ref, gmem_ref, wkv_s, k_ref, v_ref, kb_ref, vb_ref)

    @pl.when(c == pl.num_programs(0) - 1)
    def _():
        roll.wait()
        fill.wait()


def _prep(hist, mem, gmem, wkv):
    n = hist.shape[1]
    nb = mem.shape[0]
    rq = MEM_REQ_BLOCK
    assert nb % rq == 0 and nb // rq >= 3
    rows_blk = lambda: pl.BlockSpec((rq, N_MEM * KV_ROWS, LANES), lambda b: (b, 0, 0))
    flat_blk = lambda: pl.BlockSpec((rq, N_MEM, BRANCH_W), lambda b: (b, 0, 0))
    hbm = lambda: pl.BlockSpec(memory_space=pl.ANY)
    return pl.pallas_call(
        _prep_kernel,
        grid=(nb // rq,),
        in_specs=[hbm(),
                  pl.BlockSpec((rq, N_MEM, D_MODEL), lambda b: (b, 0, 0)),
                  _resident((1, D_MODEL)),
                  _resident((D_MODEL, 2 * BRANCH_W))],
        out_specs=[hbm(), _resident((n, BRANCH_W)),
                   rows_blk(), rows_blk(), flat_blk(), flat_blk()],
        out_shape=[jax.ShapeDtypeStruct(hist.shape, F32),
                   jax.ShapeDtypeStruct((n, BRANCH_W), F32),
                   jax.ShapeDtypeStruct((nb, N_MEM * KV_ROWS, LANES), F32),
                   jax.ShapeDtypeStruct((nb, N_MEM * KV_ROWS, LANES), F32),
                   jax.ShapeDtypeStruct((nb, N_MEM, BRANCH_W), BF16),
                   jax.ShapeDtypeStruct((nb, N_MEM, BRANCH_W), BF16)],
        scratch_shapes=[pltpu.VMEM((D_MODEL, 2 * BRANCH_W), BF16),
                        pltpu.VMEM(hist.shape, F32),
                        pltpu.SemaphoreType.DMA((3,))],
        compiler_params=pltpu.CompilerParams(dimension_semantics=("arbitrary",),
                                             vmem_limit_bytes=V7X_VMEM_LIMIT_BYTES),
        name="prep",
    )(hist, mem, gmem, wkv)


def kernel(x_prompt, x_sample, state_pool, cache_mem_k, cache_mem_v, mem_prompt, norm_in_g, w_in,
           pool_w, pool_scale, sgu_norm_g, sgu_w, sgu_b, mem_norm_g, w_kv, w_down, w_out, norm_f_g):
    depth = w_in.shape[0]
    assert depth == 1, "single-layer step"
    nb, seq, _ = x_prompt.shape
    ns, dec_seq, _ = x_sample.shape
    assert dec_seq == 1 and seq % SEQ_TILE == 0 and seq >= HIST_ROWS

    row = lambda a: a.reshape(1, -1)
    gin, pscale, sgug, gmem, gf = (row(norm_in_g[0]), row(pool_scale[0]), row(sgu_norm_g[0]),
                                   row(mem_norm_g[0]), row(norm_f_g))
    win, poolw, wkv, wdown, wout, sguw, sgub = (w_in[0], pool_w[0], w_kv[0], w_down[0], w_out[0],
                                                sgu_w[0], sgu_b[0])
    sgub_t = sgub.T

    x_tiles = x_sample.reshape(ns * ROW_TILES, LANES)
    hist_s = jnp.transpose(state_pool[0], (1, 0, 2))
    new_hist_s, hwin_s, k_rows, v_rows, kb, vb = _prep(hist_s, mem_prompt, gmem, wkv)

    y_prompt, hist_p, p_in_s, vn_s, _, _, _, _, _, _, y_tiles = _prompt_layer(
        x_prompt, kb, vb, gin, win, poolw, pscale, sgug, sguw, sgub_t, wdown, wout, gf, x_tiles, hwin_s,
        _to_kv_rows(cache_mem_k[0].reshape(ns, N_MEM, BRANCH_W)),
        _to_kv_rows(cache_mem_v[0].reshape(ns, N_MEM, BRANCH_W)))

    new_pool_p = hist_p[None, :, HIST_ROWS - POOL_HIST:, :]
    new_hist_s = lax.dynamic_update_slice(new_hist_s, p_in_s[None], (POOL_HIST - 1, 0, 0))
    new_pool_s = jnp.transpose(new_hist_s, (1, 0, 2))[None]
    kv_out = lambda a: _from_kv_rows(a.reshape(nb, N_MEM, KV_ROWS, LANES))[None]
    return (y_prompt, y_tiles.reshape(ns, 1, D_MODEL), new_pool_p, new_pool_s,
            kv_out(k_rows), kv_out(v_rows), vn_s.reshape(1, ns, 1, BRANCH_W))
```

```python
import jax
import jax.numpy as jnp
from jax import lax
from jax.experimental import pallas as pl
from jax.experimental.pallas import tpu as pltpu

D_MODEL = 1024
BRANCH_W = 1024
N_BRANCH = 3
N_PROJ = 7 + N_BRANCH
POOL_WINDOWS = (2, 4, 8, 16)
POOL_GROUPS = len(POOL_WINDOWS)
POOL_GW = BRANCH_W // POOL_GROUPS
POOL_HIST = max(POOL_WINDOWS) - 1
HIST_ROWS = POOL_HIST + 1
SGU_CHUNK = 128
SGU_GROUPS = 4
SGU_GW = BRANCH_W // SGU_GROUPS
N_MEM = 256
XA_HEADS = 4
XA_HD = BRANCH_W // XA_HEADS
EPS = 1e-6
PAST_LEN = 16384

SEQ_TILE = 256
MEM_REQ_BLOCK = 2
LANES = 128
XA_LANE_TILES = XA_HD // LANES
KV_ROWS = XA_HEADS * XA_LANE_TILES
LOG2E = 1.4426950408889634
W_STAGE_SLOTS = 8
W_STAGE_ROWS = 128
W_STAGE_COLS = 1024
V7X_VMEM_LIMIT_BYTES = 62 * 1024 * 1024

F32 = jnp.float32
BF16 = jnp.bfloat16

_sigmoid = jax.nn.sigmoid


def _rmsnorm(x, g):
    return x * lax.rsqrt(jnp.mean(x * x, axis=-1, keepdims=True) + EPS) * g


def _silu(z):
    return z * _sigmoid(z)


def _dot(a, b):
    return jnp.dot(a, b, preferred_element_type=F32)


def _resident(shape):
    zeros = (0,) * len(shape)
    return pl.BlockSpec(shape, lambda *_: zeros, pipeline_mode=pl.Buffered(1))


def _to_kv_rows(a):
    lead = a.shape[:-1]
    a = a.reshape(*lead, XA_HEADS, XA_LANE_TILES, LANES)
    return jnp.swapaxes(a, -3, -2).reshape(*lead, KV_ROWS, LANES)


ROW_TILES = D_MODEL // LANES
NATURAL_ORDER = tuple(range(ROW_TILES))
KV_ROW_ORDER = tuple((j % XA_HEADS) * XA_LANE_TILES + j // XA_HEADS for j in range(KV_ROWS))


def _load_row_tiles(ref, order=NATURAL_ORDER):
    n = ref.shape[0] // len(order)
    tiles = [None] * len(order)
    for j, t in enumerate(order):
        tiles[t] = ref[pl.ds(j, n, stride=len(order)), :]
    return jnp.concatenate(tiles, axis=1)


def _store_row_tiles(ref, val, order=NATURAL_ORDER):
    n = val.shape[0]
    for j, t in enumerate(order):
        ref[pl.ds(j, n, stride=len(order)), :] = val[:, t * LANES:(t + 1) * LANES]


def _from_kv_rows(a):
    lead = a.shape[:-2]
    a = a.reshape(*lead, XA_LANE_TILES, XA_HEADS, LANES)
    return jnp.swapaxes(a, -3, -2).reshape(*lead, XA_HEADS, XA_HD)


def _mem_kv_body(mem_ref, g_ref, wkv_s, k_ref, v_ref, kb_ref, vb_ref):
    n_req = mem_ref.shape[0]
    mem = jnp.concatenate([mem_ref[r] for r in range(n_req)], axis=0)
    kv = _dot(_rmsnorm(mem, g_ref[...]).astype(BF16), wkv_s[...])
    for r in range(n_req):
        k = kv[r * N_MEM:(r + 1) * N_MEM, :BRANCH_W]
        v = kv[r * N_MEM:(r + 1) * N_MEM, BRANCH_W:]
        kb_ref[r] = k.astype(BF16)
        vb_ref[r] = v.astype(BF16)
        for h in range(XA_HEADS):
            for lt in range(XA_LANE_TILES):
                cols = slice(h * XA_HD + lt * LANES, h * XA_HD + (lt + 1) * LANES)
                rows = pl.ds(lt * XA_HEADS + h, N_MEM, stride=KV_ROWS)
                k_ref[r, rows, :] = k[:, cols]
                v_ref[r, rows, :] = v[:, cols]


def _stage_weights(copies, stage_ref, sem_ref, after_chunk):
    slots = stage_ref.shape[0]
    ahead = slots - 1

    def chunk_copy(k):
        return pltpu.make_async_copy(copies[k][0], stage_ref.at[k % slots], sem_ref.at[k % slots])

    for k in range(min(ahead, len(copies))):
        chunk_copy(k).start(priority=k % 2)
    for k, (_, dst) in enumerate(copies):
        if k + ahead < len(copies):
            chunk_copy(k + ahead).start(priority=(k + ahead) % 2)
        chunk_copy(k).wait()
        dst[...] = stage_ref[k % slots].astype(BF16)
        after_chunk(k)


class _HbmWriter:
    def __init__(self, stage_ref, sem_ref):
        self.stage_ref, self.sem_ref, self.pending, self.count = stage_ref, sem_ref, [], 0

    def write(self, value, dst_hbm):
        slots = self.stage_ref.shape[0]
        slot = self.count % slots
        if len(self.pending) == slots:
            self.pending.pop(0).wait()
        self.stage_ref[slot] = value
        copy = pltpu.make_async_copy(self.stage_ref.at[slot], dst_hbm, self.sem_ref.at[slot])
        copy.start()
        self.pending.append(copy)
        self.count += 1

    def finish(self):
        for copy in self.pending:
            copy.wait()
        self.pending = []


def _first_step(xs_ref, hwin_ref, gin_ref, poolw_ref, pscale_ref, sgug_ref, sguw_ref, sgub_ref,
                win_hbm, wdown_hbm, wout_hbm, win_ref, wdown_ref, wout_ref, stage_ref, stage_sem,
                qs_ref, tiles_ref, tiles_sem, ostage_ref, ostage_sem,
                pin_hbm, vn_hbm, opool_hbm, osgu_hbm, az_hbm, gate_hbms):
    _, sr, sc = stage_ref.shape
    assert sc == BRANCH_W and D_MODEL % sr == 0
    k_slabs = D_MODEL // sr
    tiles = lambda ref: [(pl.ds(r * sr, sr), pl.ds(c * sc, sc))
                         for c in range(ref.shape[-1] // sc) for r in range(ref.shape[-2] // sr)]
    copies = [(win_hbm.at[r, c], win_ref.at[r, c]) for r, c in tiles(win_ref)]
    n_win = len(copies)
    copies += [(wdown_hbm.at[n, r, c], wdown_ref.at[n, r, c]) for n in range(N_BRANCH) for r, c in tiles(wdown_ref)]
    copies += [(wout_hbm.at[r, c], wout_ref.at[r, c]) for r, c in tiles(wout_ref)]

    hbs = _rmsnorm(_load_row_tiles(xs_ref), gin_ref[...]).astype(BF16)
    writer = _HbmWriter(ostage_ref, ostage_sem)
    kept = {}

    def finish_chunk(c, val):
        if c == 0:
            kept["p_in"] = val
            writer.write(val, pin_hbm)
        elif c == 1:
            p_in = kept["p_in"]
            mixed = []
            for g, w in enumerate(POOL_WINDOWS):
                sl = slice(g * POOL_GW, (g + 1) * POOL_GW)
                d = (p_in[:, sl] + hwin_ref[:, sl]) / float(min(w, PAST_LEN + 1)) - p_in[:, sl]
                mixed.append(_dot(d.astype(BF16), poolw_ref[g].astype(BF16)))
            writer.write(jnp.concatenate(mixed, axis=1) * pscale_ref[...] * _silu(val), opool_hbm)
        elif c == 2:
            kept["u"] = val
        elif c == 3:
            vn = _rmsnorm(val, sgug_ref[...])
            _store_row_tiles(tiles_ref, vn)
            kept["vn_copy"] = pltpu.make_async_copy(tiles_ref, vn_hbm, tiles_sem.at[0])
            kept["vn_copy"].start()
            gated = [vn[:, g * SGU_GW:(g + 1) * SGU_GW] * sguw_ref[g, 0:1, 0:1] + sgub_ref[0:1, g:g + 1]
                     for g in range(SGU_GROUPS)]
            kept["u_gated"] = kept["u"] * jnp.concatenate(gated, axis=1)
        elif c == 4:
            writer.write(kept["u_gated"] * _silu(val), osgu_hbm)
        elif c == 5:
            _store_row_tiles(qs_ref, val, KV_ROW_ORDER)
        elif c == 6:
            writer.write(val, az_hbm)
        else:
            writer.write(val, gate_hbms[c - 7])

    def after_chunk(k):
        if k >= n_win:
            return
        c, r = divmod(k, k_slabs)
        part = _dot(hbs[:, r * sr:(r + 1) * sr], win_ref[r * sr:(r + 1) * sr, c * sc:(c + 1) * sc])
        kept["acc"] = part if r == 0 else kept["acc"] + part
        if r == k_slabs - 1:
            finish_chunk(c, kept["acc"])

    _stage_weights(copies, stage_ref, stage_sem, after_chunk)
    writer.finish()
    kept["vn_copy"].wait()


def _last_step(xs_ref, attn_ref, gf_ref, wdown_ref, wout_ref, opool_hbm, osgu_hbm, az_hbm, gate_hbms,
               stage_ref, last_sem, tiles_ref, ys_hbm):
    ns = xs_ref.shape[0] // ROW_TILES
    srcs = (opool_hbm, osgu_hbm, az_hbm) + tuple(gate_hbms)
    per_slot = stage_ref.shape[1] // ns
    bufs = [stage_ref.at[k // per_slot, pl.ds((k % per_slot) * ns, ns), :] for k in range(len(srcs))]
    copies = [pltpu.make_async_copy(src, buf, last_sem.at[k]) for k, (src, buf) in enumerate(zip(srcs, bufs))]
    for copy in copies:
        copy.start()
    for copy in copies:
        copy.wait()
    o_pool, o_sgu, a_z, *gates = [buf[...] for buf in bufs]
    o_xa = _load_row_tiles(attn_ref, KV_ROW_ORDER) * _silu(a_z)
    merged = None
    for n, o in enumerate((o_pool, o_sgu, o_xa)):
        t = _sigmoid(gates[n]) * _dot(o.astype(BF16), wdown_ref[n])
        merged = t if merged is None else merged + t
    xn = _load_row_tiles(xs_ref) + _dot(merged.astype(BF16), wout_ref[...])
    _store_row_tiles(tiles_ref, _rmsnorm(xn, gf_ref[...]))
    out = pltpu.make_async_copy(tiles_ref, ys_hbm, last_sem.at[0])
    out.start()
    out.wait()


def _sample_attention(get_q, k_ref, v_ref, put_o):
    for r in range(k_ref.shape[0]):
        q = get_q(r) * (XA_HD ** -0.5 * LOG2E)
        part = k_ref[r] * q
        part = part + pltpu.roll(part, XA_HEADS, axis=1)
        s = jnp.sum(part, axis=-1, keepdims=True)
        e = jnp.exp2(s - jnp.max(s, axis=0, keepdims=True))
        put_o(r, jnp.sum(e * v_ref[r], axis=0) / jnp.sum(e, axis=0))


def _prompt_kernel(x_ref, k_ref, v_ref, gin_ref, win_hbm, poolw_ref, pscale_ref, sgug_ref,
                   sguw_ref, sgub_ref, wdown_hbm, wout_hbm, gf_ref, xs_ref, hwin_ref, sk_ref, sv_ref,
                   y_ref, hist_ref, pin_hbm, vn_hbm, opool_hbm, osgu_hbm, az_hbm, g0_hbm, g1_hbm, g2_hbm, ys_hbm,
                   ext_ref, win_ref, wdown_ref, wout_ref, stage_ref, stage_sem, qs_ref, attn_ref,
                   ostage_ref, ostage_sem, last_sem):
    i = pl.program_id(1)
    ts = x_ref.shape[1]
    step = pl.program_id(0) * pl.num_programs(1) + i

    @pl.when(step == 0)
    def _():
        _first_step(xs_ref, hwin_ref, gin_ref, poolw_ref, pscale_ref, sgug_ref, sguw_ref, sgub_ref,
                    win_hbm, wdown_hbm, wout_hbm, win_ref, wdown_ref, wout_ref, stage_ref, stage_sem,
                    qs_ref, attn_ref, last_sem, ostage_ref, ostage_sem,
                    pin_hbm, vn_hbm, opool_hbm, osgu_hbm, az_hbm, (g0_hbm, g1_hbm, g2_hbm))

    @pl.when(i == 0)
    def _():
        ext_ref[0:HIST_ROWS, :] = jnp.zeros((HIST_ROWS, BRANCH_W), F32)

    q_rows = sk_ref.shape[0] * KV_ROWS
    q_blk = qs_ref[pl.ds(pl.multiple_of(step * q_rows, q_rows), q_rows), :]

    def put_attention(r, out):
        attn_ref[pl.ds(pl.multiple_of(step * q_rows + r * KV_ROWS, KV_ROWS), KV_ROWS), :] = out

    _sample_attention(lambda r: q_blk[r * KV_ROWS:(r + 1) * KV_ROWS, :], sk_ref, sv_ref, put_attention)

    x = x_ref[0]
    hb = _rmsnorm(x, gin_ref[...]).astype(BF16)

    def proj(c):
        return _dot(hb, win_ref[:, c * BRANCH_W:(c + 1) * BRANCH_W])


    p_in = proj(0)
    p_z = proj(1)
    ext_ref[HIST_ROWS:HIST_ROWS + ts, :] = p_in
    pos = i * ts + lax.broadcasted_iota(jnp.int32, (ts, 1), 0)
    pooled = []
    for g, w in enumerate(POOL_WINDOWS):
        sl = slice(g * POOL_GW, (g + 1) * POOL_GW)
        win = ext_ref[:, sl]
        span = 1
        while span < w:
            win = win + pltpu.roll(win, span, axis=0)
            span *= 2
        inv_cnt = 1.0 / jnp.minimum(w, pos + 1).astype(F32)
        pooled.append((win[HIST_ROWS:, :] * inv_cnt - p_in[:, sl]).astype(BF16))
    ext_ref[0:HIST_ROWS, :] = p_in[ts - HIST_ROWS:, :]
    hist_ref[0] = p_in[ts - HIST_ROWS:, :]

    v = proj(3)
    u = proj(2)
    s_z = proj(4)
    mixed = [_dot(pooled[g], poolw_ref[g].astype(BF16)) for g in range(POOL_GROUPS)]
    o_pool = (jnp.concatenate(mixed, axis=1) * pscale_ref[...] * _silu(p_z)).astype(BF16)

    vnb = _rmsnorm(v, sgug_ref[...]).astype(BF16)
    tril = (lax.broadcasted_iota(jnp.int32, (SGU_CHUNK, SGU_CHUNK), 0)
            >= lax.broadcasted_iota(jnp.int32, (SGU_CHUNK, SGU_CHUNK), 1))
    ws = [jnp.where(tril, sguw_ref[g], 0.0).astype(BF16) for g in range(SGU_GROUPS)]
    q = proj(5)
    a_z = proj(6)
    rows = []
    for c in range(ts // SGU_CHUNK):
        rs = slice(c * SGU_CHUNK, (c + 1) * SGU_CHUNK)
        cols = [_dot(ws[g], vnb[rs, g * SGU_GW:(g + 1) * SGU_GW]) + sgub_ref[:, g:g + 1]
                for g in range(SGU_GROUPS)]
        rows.append(jnp.concatenate(cols, axis=1))
    o_sgu = (u * jnp.concatenate(rows, axis=0) * _silu(s_z)).astype(BF16)

    qb = q.astype(BF16)
    scores = [lax.dot_general(qb[:, hd * XA_HD:(hd + 1) * XA_HD], k_ref[0, :, hd * XA_HD:(hd + 1) * XA_HD],
                              (((1,), (1,)), ((), ())), preferred_element_type=F32) * (XA_HD ** -0.5 * LOG2E)
              for hd in range(XA_HEADS)]
    gates = [proj(7 + n) for n in range(N_BRANCH)]
    heads = []
    for hd in range(XA_HEADS):
        s = scores[hd]
        e = jnp.exp2(s - jnp.max(s, axis=-1, keepdims=True))
        pr = e * (1.0 / jnp.sum(e, axis=-1, keepdims=True))
        heads.append(_dot(pr.astype(BF16), v_ref[0, :, hd * XA_HD:(hd + 1) * XA_HD]))
    o_xa = (jnp.concatenate(heads, axis=1) * _silu(a_z)).astype(BF16)

    merged = None
    for n, o in enumerate((o_pool, o_sgu, o_xa)):
        t = _sigmoid(gates[n]) * _dot(o, wdown_ref[n])
        merged = t if merged is None else merged + t
    xn = x + _dot(merged.astype(BF16), wout_ref[...])
    y_ref[0] = _rmsnorm(xn, gf_ref[...])

    @pl.when(step == pl.num_programs(0) * pl.num_programs(1) - 1)
    def _():
        _last_step(xs_ref, attn_ref, gf_ref, wdown_ref, wout_ref, opool_hbm, osgu_hbm, az_hbm,
                   (g0_hbm, g1_hbm, g2_hbm), stage_ref, last_sem, qs_ref, ys_hbm)


def _prompt_layer(x, kb, vb, gin, win, poolw, pscale, sgug, sguw, sgub_t, wdown, wout, gf,
                  xs_tiles, hwin, sk, sv):
    nb, seq, _ = x.shape
    ts = SEQ_TILE
    n_tiles = seq // ts
    ns = sk.shape[0]
    rb = ns // (nb * n_tiles)
    assert rb * nb * n_tiles == ns
    sample_row = jax.ShapeDtypeStruct((ns, BRANCH_W), F32)
    sample_tiles = jax.ShapeDtypeStruct((ns * ROW_TILES, LANES), F32)
    in_cols = win.shape[1]
    assert in_cols % W_STAGE_COLS == 0 and D_MODEL % W_STAGE_COLS == 0 and D_MODEL % W_STAGE_ROWS == 0
    step = lambda b, i: b * n_tiles + i
    hbm = lambda: pl.BlockSpec(memory_space=pl.ANY)
    return pl.pallas_call(
        _prompt_kernel,
        grid=(nb, n_tiles),
        in_specs=[pl.BlockSpec((1, ts, D_MODEL), lambda b, i: (b, i, 0)),
                  pl.BlockSpec((1, N_MEM, BRANCH_W), lambda b, i: (b, 0, 0)),
                  pl.BlockSpec((1, N_MEM, BRANCH_W), lambda b, i: (b, 0, 0)),
                  _resident((1, D_MODEL)),
                  hbm(),
                  _resident((POOL_GROUPS, POOL_GW, POOL_GW)),
                  _resident((1, BRANCH_W)),
                  _resident((1, BRANCH_W)),
                  _resident((SGU_GROUPS, SGU_CHUNK, SGU_CHUNK)),
                  _resident((SGU_CHUNK, SGU_GROUPS)),
                  hbm(),
                  hbm(),
                  _resident((1, D_MODEL)),
                  _resident((ns * ROW_TILES, LANES)),
                  _resident((ns, BRANCH_W)),
                  pl.BlockSpec((rb, N_MEM, KV_ROWS, LANES), lambda b, i: (step(b, i), 0, 0, 0)),
                  pl.BlockSpec((rb, N_MEM, KV_ROWS, LANES), lambda b, i: (step(b, i), 0, 0, 0))],
        out_specs=[pl.BlockSpec((1, ts, D_MODEL), lambda b, i: (b, i, 0)),
                   pl.BlockSpec((1, HIST_ROWS, BRANCH_W), lambda b, i: (b, 0, 0))] + [hbm()] * 9,
        out_shape=[jax.ShapeDtypeStruct((nb, seq, D_MODEL), F32),
                   jax.ShapeDtypeStruct((nb, HIST_ROWS, BRANCH_W), F32),
                   sample_row, sample_tiles] + [sample_row] * 6 + [sample_tiles],
        scratch_shapes=[pltpu.VMEM((HIST_ROWS + ts, BRANCH_W), F32),
                        pltpu.VMEM((D_MODEL, in_cols), BF16),
                        pltpu.VMEM((N_BRANCH, BRANCH_W, D_MODEL), BF16),
                        pltpu.VMEM((D_MODEL, D_MODEL), BF16),
                        pltpu.VMEM((W_STAGE_SLOTS, W_STAGE_ROWS, W_STAGE_COLS), F32),
                        pltpu.SemaphoreType.DMA((W_STAGE_SLOTS,)),
                        pltpu.VMEM((ns * KV_ROWS, LANES), F32),
                        pltpu.VMEM((ns * KV_ROWS, LANES), F32),
                        pltpu.VMEM((2, ns, BRANCH_W), F32),
                        pltpu.SemaphoreType.DMA((2,)),
                        pltpu.SemaphoreType.DMA((2 * N_BRANCH,))],
        compiler_params=pltpu.CompilerParams(dimension_semantics=("arbitrary", "arbitrary"),
                                             vmem_limit_bytes=V7X_VMEM_LIMIT_BYTES),
        name="prompt_layer",
    )(x, kb, vb, gin, win, poolw, pscale, sgug, sguw, sgub_t, wdown, wout, gf, xs_tiles, hwin, sk, sv)


def _prep_kernel(hist_hbm, mem_ref, gmem_ref, wkv_ref,
                 newhist_hbm, hwin_ref, k_ref, v_ref, kb_ref, vb_ref, wkv_s, hist_s, hist_sem):
    c = pl.program_id(0)
    load = pltpu.make_async_copy(hist_hbm, hist_s, hist_sem.at[0])
    roll = pltpu.make_async_copy(hist_s.at[pl.ds(1, POOL_HIST - 1)], newhist_hbm.at[pl.ds(0, POOL_HIST - 1)],
                                 hist_sem.at[1])
    fill = pltpu.make_async_copy(hist_s.at[0], newhist_hbm.at[POOL_HIST - 1], hist_sem.at[2])

    @pl.when(c == 0)
    def _():
        load.start()
        wkv_s[...] = wkv_ref[...].astype(BF16)

    @pl.when(c == 1)
    def _():
        load.wait()
        for g, w in enumerate(POOL_WINDOWS):
            sl = slice(g * POOL_GW, (g + 1) * POOL_GW)
            win = hist_s[POOL_HIST - 1, :, sl]
            for j in range(2, w):
                win = win + hist_s[POOL_HIST - j, :, sl]
            hwin_ref[:, sl] = win
        roll.start()
        hist_s[0] = jnp.zeros(hist_s.shape[1:], F32)
        fill.start()

    _mem_kv_body(mem_ref, gmem_ref, wkv_s, k_ref, v_ref, kb_ref, vb_ref)

    @pl.when(c == pl.num_programs(0) - 1)
    def _():
        roll.wait()
        fill.wait()


def _prep(hist, mem, gmem, wkv):
    n = hist.shape[1]
    nb = mem.shape[0]
    rq = MEM_REQ_BLOCK
    assert nb % rq == 0 and nb // rq >= 3
    rows_blk = lambda: pl.BlockSpec((rq, N_MEM * KV_ROWS, LANES), lambda b: (b, 0, 0))
    flat_blk = lambda: pl.BlockSpec((rq, N_MEM, BRANCH_W), lambda b: (b, 0, 0))
    hbm = lambda: pl.BlockSpec(memory_space=pl.ANY)
    return pl.pallas_call(
        _prep_kernel,
        grid=(nb // rq,),
        in_specs=[hbm(),
                  pl.BlockSpec((rq, N_MEM, D_MODEL), lambda b: (b, 0, 0)),
                  _resident((1, D_MODEL)),
                  _resident((D_MODEL, 2 * BRANCH_W))],
        out_specs=[hbm(), _resident((n, BRANCH_W)),
                   rows_blk(), rows_blk(), flat_blk(), flat_blk()],
        out_shape=[jax.ShapeDtypeStruct(hist.shape, F32),
                   jax.ShapeDtypeStruct((n, BRANCH_W), F32),
                   jax.ShapeDtypeStruct((nb, N_MEM * KV_ROWS, LANES), F32),
                   jax.ShapeDtypeStruct((nb, N_MEM * KV_ROWS, LANES), F32),
                   jax.ShapeDtypeStruct((nb, N_MEM, BRANCH_W), BF16),
                   jax.ShapeDtypeStruct((nb, N_MEM, BRANCH_W), BF16)],
        scratch_shapes=[pltpu.VMEM((D_MODEL, 2 * BRANCH_W), BF16),
                        pltpu.VMEM(hist.shape, F32),
                        pltpu.SemaphoreType.DMA((3,))],
        compiler_params=pltpu.CompilerParams(dimension_semantics=("arbitrary",),
                                             vmem_limit_bytes=V7X_VMEM_LIMIT_BYTES),
        name="prep",
    )(hist, mem, gmem, wkv)


def kernel(x_prompt, x_sample, state_pool, cache_mem_k, cache_mem_v, mem_prompt, norm_in_g, w_in,
           pool_w, pool_scale, sgu_norm_g, sgu_w, sgu_b, mem_norm_g, w_kv, w_down, w_out, norm_f_g):
    depth = w_in.shape[0]
    assert depth == 1, "single-layer step"
    nb, seq, _ = x_prompt.shape
    ns, dec_seq, _ = x_sample.shape
    assert dec_seq == 1 and seq % SEQ_TILE == 0 and seq >= HIST_ROWS

    row = lambda a: a.reshape(1, -1)
    gin, pscale, sgug, gmem, gf = (row(norm_in_g[0]), row(pool_scale[0]), row(sgu_norm_g[0]),
                                   row(mem_norm_g[0]), row(norm_f_g))
    win, poolw, wkv, wdown, wout, sguw, sgub = (w_in[0], pool_w[0], w_kv[0], w_down[0], w_out[0],
                                                sgu_w[0], sgu_b[0])
    sgub_t = sgub.T

    x_tiles = x_sample.reshape(ns * ROW_TILES, LANES)
    hist_s = jnp.transpose(state_pool[0], (1, 0, 2))
    new_hist_s, hwin_s, k_rows, v_rows, kb, vb = _prep(hist_s, mem_prompt, gmem, wkv)

    y_prompt, hist_p, p_in_s, vn_s, _, _, _, _, _, _, y_tiles = _prompt_layer(
        x_prompt, kb, vb, gin, win, poolw, pscale, sgug, sguw, sgub_t, wdown, wout, gf, x_tiles, hwin_s,
        _to_kv_rows(cache_mem_k[0].reshape(ns, N_MEM, BRANCH_W)),
        _to_kv_rows(cache_mem_v[0].reshape(ns, N_MEM, BRANCH_W)))

    new_pool_p = hist_p[None, :, HIST_ROWS - POOL_HIST:, :]
    new_hist_s = lax.dynamic_update_slice(new_hist_s, p_in_s[None], (POOL_HIST - 1, 0, 0))
    new_pool_s = jnp.transpose(new_hist_s, (1, 0, 2))[None]
    kv_out = lambda a: _from_kv_rows(a.reshape(nb, N_MEM, KV_ROWS, LANES))[None]
    return (y_prompt, y_tiles.reshape(ns, 1, D_MODEL), new_pool_p, new_pool_s,
            kv_out(k_rows), kv_out(v_rows), vn_s.reshape(1, ns, 1, BRANCH_W))
```

```python
import jax
import jax.numpy as jnp
from jax import lax
from jax.experimental import pallas as pl
from jax.experimental.pallas import tpu as pltpu

D_MODEL = 1024
BRANCH_W = 1024
N_BRANCH = 3
N_PROJ = 7 + N_BRANCH
POOL_WINDOWS = (2, 4, 8, 16)
POOL_GROUPS = len(POOL_WINDOWS)
POOL_GW = BRANCH_W // POOL_GROUPS
POOL_HIST = max(POOL_WINDOWS) - 1
HIST_ROWS = POOL_HIST + 1
SGU_CHUNK = 128
SGU_GROUPS = 4
SGU_GW = BRANCH_W // SGU_GROUPS
N_MEM = 256
XA_HEADS = 4
XA_HD = BRANCH_W // XA_HEADS
EPS = 1e-6
PAST_LEN = 16384

SEQ_TILE = 256
MEM_REQ_BLOCK = 2
SAMPLE_ATTN_CHUNK = 16
LANES = 128
XA_LANE_TILES = XA_HD // LANES
KV_ROWS = XA_HEADS * XA_LANE_TILES
LOG2E = 1.4426950408889634
W_STAGE_SLOTS = 8
W_STAGE_ROWS = 128
W_STAGE_COLS = 1024
V7X_VMEM_LIMIT_BYTES = 62 * 1024 * 1024

F32 = jnp.float32
BF16 = jnp.bfloat16

_sigmoid = jax.nn.sigmoid


def _rmsnorm(x, g):
    return x * lax.rsqrt(jnp.mean(x * x, axis=-1, keepdims=True) + EPS) * g


def _silu(z):
    return z * _sigmoid(z)


def _dot(a, b):
    return jnp.dot(a, b, preferred_element_type=F32)


def _resident(shape):
    zeros = (0,) * len(shape)
    return pl.BlockSpec(shape, lambda *_: zeros, pipeline_mode=pl.Buffered(1))


def _to_kv_rows(a):
    lead = a.shape[:-1]
    a = a.reshape(*lead, XA_HEADS, XA_LANE_TILES, LANES)
    return jnp.swapaxes(a, -3, -2).reshape(*lead, KV_ROWS, LANES)


ROW_TILES = D_MODEL // LANES
NATURAL_ORDER = tuple(range(ROW_TILES))
KV_ROW_ORDER = tuple((j % XA_HEADS) * XA_LANE_TILES + j // XA_HEADS for j in range(KV_ROWS))


def _load_row_tiles(ref, order=NATURAL_ORDER):
    n = ref.shape[0] // len(order)
    tiles = [None] * len(order)
    for j, t in enumerate(order):
        tiles[t] = ref[pl.ds(j, n, stride=len(order)), :]
    return jnp.concatenate(tiles, axis=1)


def _store_row_tiles(ref, val, order=NATURAL_ORDER):
    n = val.shape[0]
    for j, t in enumerate(order):
        ref[pl.ds(j, n, stride=len(order)), :] = val[:, t * LANES:(t + 1) * LANES]


def _from_kv_rows(a):
    lead = a.shape[:-2]
    a = a.reshape(*lead, XA_LANE_TILES, XA_HEADS, LANES)
    return jnp.swapaxes(a, -3, -2).reshape(*lead, XA_HEADS, XA_HD)


def _mem_kv_body(mem_ref, g_ref, wkv_s, k_ref, v_ref, kb_ref, vb_ref):
    n_req = mem_ref.shape[0]
    mem = jnp.concatenate([mem_ref[r] for r in range(n_req)], axis=0)
    kv = _dot(_rmsnorm(mem, g_ref[...]).astype(BF16), wkv_s[...])
    for r in range(n_req):
        k = kv[r * N_MEM:(r + 1) * N_MEM, :BRANCH_W]
        v = kv[r * N_MEM:(r + 1) * N_MEM, BRANCH_W:]
        kb_ref[r] = k.astype(BF16)
        vb_ref[r] = v.astype(BF16)
        for h in range(XA_HEADS):
            for lt in range(XA_LANE_TILES):
                cols = slice(h * XA_HD + lt * LANES, h * XA_HD + (lt + 1) * LANES)
                rows = pl.ds(lt * XA_HEADS + h, N_MEM, stride=KV_ROWS)
                k_ref[r, rows, :] = k[:, cols]
                v_ref[r, rows, :] = v[:, cols]


def _stage_weights(copies, stage_ref, sem_ref, after_chunk):
    slots = stage_ref.shape[0]
    ahead = slots - 1

    def chunk_copy(k):
        return pltpu.make_async_copy(copies[k][0], stage_ref.at[k % slots], sem_ref.at[k % slots])

    for k in range(min(ahead, len(copies))):
        chunk_copy(k).start()
    for k, (_, dst) in enumerate(copies):
        if k + ahead < len(copies):
            chunk_copy(k + ahead).start()
        chunk_copy(k).wait()
        dst[...] = stage_ref[k % slots].astype(BF16)
        after_chunk(k)


class _HbmWriter:
    def __init__(self, stage_ref, sem_ref):
        self.stage_ref, self.sem_ref, self.pending, self.count = stage_ref, sem_ref, [], 0

    def write(self, value, dst_hbm):
        slots = self.stage_ref.shape[0]
        slot = self.count % slots
        if len(self.pending) == slots:
            self.pending.pop(0).wait()
        self.stage_ref[slot] = value
        copy = pltpu.make_async_copy(self.stage_ref.at[slot], dst_hbm, self.sem_ref.at[slot])
        copy.start()
        self.pending.append(copy)
        self.count += 1

    def finish(self):
        for copy in self.pending:
            copy.wait()
        self.pending = []


def _first_step(xs_ref, hwin_ref, gin_ref, poolw_ref, pscale_ref, sgug_ref, sguw_ref, sgub_ref,
                win_hbm, wdown_hbm, wout_hbm, win_ref, wdown_ref, wout_ref, stage_ref, stage_sem,
                qs_ref, tiles_ref, tiles_sem, ostage_ref, ostage_sem,
                pin_hbm, vn_hbm, opool_hbm, osgu_hbm, az_hbm, gate_hbms):
    _, sr, sc = stage_ref.shape
    assert sc == BRANCH_W and D_MODEL % sr == 0
    k_slabs = D_MODEL // sr
    tiles = lambda ref: [(pl.ds(r * sr, sr), pl.ds(c * sc, sc))
                         for c in range(ref.shape[-1] // sc) for r in range(ref.shape[-2] // sr)]
    copies = [(win_hbm.at[r, c], win_ref.at[r, c]) for r, c in tiles(win_ref)]
    n_win = len(copies)
    copies += [(wdown_hbm.at[n, r, c], wdown_ref.at[n, r, c]) for n in range(N_BRANCH) for r, c in tiles(wdown_ref)]
    copies += [(wout_hbm.at[r, c], wout_ref.at[r, c]) for r, c in tiles(wout_ref)]

    hbs = _rmsnorm(_load_row_tiles(xs_ref), gin_ref[...]).astype(BF16)
    writer = _HbmWriter(ostage_ref, ostage_sem)
    kept = {}

    def finish_chunk(c, val):
        if c == 0:
            kept["p_in"] = val
            writer.write(val, pin_hbm)
        elif c == 1:
            p_in = kept["p_in"]
            mixed = []
            for g, w in enumerate(POOL_WINDOWS):
                sl = slice(g * POOL_GW, (g + 1) * POOL_GW)
                d = (p_in[:, sl] + hwin_ref[:, sl]) / float(min(w, PAST_LEN + 1)) - p_in[:, sl]
                mixed.append(_dot(d.astype(BF16), poolw_ref[g].astype(BF16)))
            writer.write(jnp.concatenate(mixed, axis=1) * pscale_ref[...] * _silu(val), opool_hbm)
        elif c == 2:
            kept["u"] = val
        elif c == 3:
            vn = _rmsnorm(val, sgug_ref[...])
            _store_row_tiles(tiles_ref, vn)
            kept["vn_copy"] = pltpu.make_async_copy(tiles_ref, vn_hbm, tiles_sem.at[0])
            kept["vn_copy"].start()
            gated = [vn[:, g * SGU_GW:(g + 1) * SGU_GW] * sguw_ref[g, 0:1, 0:1] + sgub_ref[0:1, g:g + 1]
                     for g in range(SGU_GROUPS)]
            kept["u_gated"] = kept["u"] * jnp.concatenate(gated, axis=1)
        elif c == 4:
            writer.write(kept["u_gated"] * _silu(val), osgu_hbm)
        elif c == 5:
            _store_row_tiles(qs_ref, val, KV_ROW_ORDER)
        elif c == 6:
            writer.write(val, az_hbm)
        else:
            writer.write(val, gate_hbms[c - 7])

    def after_chunk(k):
        if k >= n_win:
            return
        c, r = divmod(k, k_slabs)
        part = _dot(hbs[:, r * sr:(r + 1) * sr], win_ref[r * sr:(r + 1) * sr, c * sc:(c + 1) * sc])
        kept["acc"] = part if r == 0 else kept["acc"] + part
        if r == k_slabs - 1:
            finish_chunk(c, kept["acc"])

    _stage_weights(copies, stage_ref, stage_sem, after_chunk)
    writer.finish()
    kept["vn_copy"].wait()


def _last_step(xs_ref, attn_ref, gf_ref, wdown_ref, wout_ref, opool_hbm, osgu_hbm, az_hbm, gate_hbms,
               stage_ref, last_sem, tiles_ref, ys_hbm):
    ns = xs_ref.shape[0] // ROW_TILES
    srcs = (opool_hbm, osgu_hbm, az_hbm) + tuple(gate_hbms)
    per_slot = stage_ref.shape[1] // ns
    bufs = [stage_ref.at[k // per_slot, pl.ds((k % per_slot) * ns, ns), :] for k in range(len(srcs))]
    copies = [pltpu.make_async_copy(src, buf, last_sem.at[k]) for k, (src, buf) in enumerate(zip(srcs, bufs))]
    for copy in copies:
        copy.start()
    for copy in copies:
        copy.wait()
    o_pool, o_sgu, a_z, *gates = [buf[...] for buf in bufs]
    o_xa = _load_row_tiles(attn_ref, KV_ROW_ORDER) * _silu(a_z)
    merged = None
    for n, o in enumerate((o_pool, o_sgu, o_xa)):
        t = _sigmoid(gates[n]) * _dot(o.astype(BF16), wdown_ref[n])
        merged = t if merged is None else merged + t
    xn = _load_row_tiles(xs_ref) + _dot(merged.astype(BF16), wout_ref[...])
    _store_row_tiles(tiles_ref, _rmsnorm(xn, gf_ref[...]))
    out = pltpu.make_async_copy(tiles_ref, ys_hbm, last_sem.at[0])
    out.start()
    out.wait()


def _sample_attention(get_q, k_ref, v_ref, put_o):
    for r in range(k_ref.shape[0]):
        q = get_q(r) * (XA_HD ** -0.5 * LOG2E)
        m = l = acc = None
        for c in range(N_MEM // SAMPLE_ATTN_CHUNK):
            rows = slice(c * SAMPLE_ATTN_CHUNK, (c + 1) * SAMPLE_ATTN_CHUNK)
            part = k_ref[r, rows] * q
            part = part + pltpu.roll(part, XA_HEADS, axis=1)
            s = jnp.sum(part, axis=-1, keepdims=True)
            m_chunk = jnp.max(s, axis=0, keepdims=True)
            m_new = m_chunk if m is None else jnp.maximum(m, m_chunk)
            e = jnp.exp2(s - m_new)
            l_chunk = jnp.sum(e, axis=0)
            acc_chunk = jnp.sum(e * v_ref[r, rows], axis=0)
            if m is None:
                l, acc = l_chunk, acc_chunk
            else:
                rescale = jnp.exp2(m - m_new)[0]
                l, acc = l * rescale + l_chunk, acc * rescale + acc_chunk
            m = m_new
        put_o(r, acc / l)


def _prompt_kernel(x_ref, k_ref, v_ref, gin_ref, win_hbm, poolw_ref, pscale_ref, sgug_ref,
                   sguw_ref, sgub_ref, wdown_hbm, wout_hbm, gf_ref, xs_ref, hwin_ref, sk_ref, sv_ref,
                   y_ref, hist_ref, pin_hbm, vn_hbm, opool_hbm, osgu_hbm, az_hbm, g0_hbm, g1_hbm, g2_hbm, ys_hbm,
                   ext_ref, win_ref, wdown_ref, wout_ref, stage_ref, stage_sem, qs_ref, attn_ref,
                   ostage_ref, ostage_sem, last_sem):
    i = pl.program_id(1)
    ts = x_ref.shape[1]
    step = pl.program_id(0) * pl.num_programs(1) + i

    @pl.when(step == 0)
    def _():
        _first_step(xs_ref, hwin_ref, gin_ref, poolw_ref, pscale_ref, sgug_ref, sguw_ref, sgub_ref,
                    win_hbm, wdown_hbm, wout_hbm, win_ref, wdown_ref, wout_ref, stage_ref, stage_sem,
                    qs_ref, attn_ref, last_sem, ostage_ref, ostage_sem,
                    pin_hbm, vn_hbm, opool_hbm, osgu_hbm, az_hbm, (g0_hbm, g1_hbm, g2_hbm))

    @pl.when(i == 0)
    def _():
        ext_ref[0:HIST_ROWS, :] = jnp.zeros((HIST_ROWS, BRANCH_W), F32)

    q_rows = sk_ref.shape[0] * KV_ROWS
    q_blk = qs_ref[pl.ds(pl.multiple_of(step * q_rows, q_rows), q_rows), :]

    def put_attention(r, out):
        attn_ref[pl.ds(pl.multiple_of(step * q_rows + r * KV_ROWS, KV_ROWS), KV_ROWS), :] = out

    _sample_attention(lambda r: q_blk[r * KV_ROWS:(r + 1) * KV_ROWS, :], sk_ref, sv_ref, put_attention)

    x = x_ref[0]
    hb = _rmsnorm(x, gin_ref[...]).astype(BF16)

    def proj(c):
        return _dot(hb, win_ref[:, c * BRANCH_W:(c + 1) * BRANCH_W])


    p_in = proj(0)
    p_z = proj(1)
    ext_ref[HIST_ROWS:HIST_ROWS + ts, :] = p_in
    pos = i * ts + lax.broadcasted_iota(jnp.int32, (ts, 1), 0)
    pooled = []
    for g, w in enumerate(POOL_WINDOWS):
        sl = slice(g * POOL_GW, (g + 1) * POOL_GW)
        win = ext_ref[:, sl]
        span = 1
        while span < w:
            win = win + pltpu.roll(win, span, axis=0)
            span *= 2
        inv_cnt = 1.0 / jnp.minimum(w, pos + 1).astype(F32)
        pooled.append((win[HIST_ROWS:, :] * inv_cnt - p_in[:, sl]).astype(BF16))
    ext_ref[0:HIST_ROWS, :] = p_in[ts - HIST_ROWS:, :]
    hist_ref[0] = p_in[ts - HIST_ROWS:, :]

    v = proj(3)
    u = proj(2)
    s_z = proj(4)
    mixed = [_dot(pooled[g], poolw_ref[g].astype(BF16)) for g in range(POOL_GROUPS)]
    o_pool = (jnp.concatenate(mixed, axis=1) * pscale_ref[...] * _silu(p_z)).astype(BF16)

    vnb = _rmsnorm(v, sgug_ref[...]).astype(BF16)
    tril = (lax.broadcasted_iota(jnp.int32, (SGU_CHUNK, SGU_CHUNK), 0)
            >= lax.broadcasted_iota(jnp.int32, (SGU_CHUNK, SGU_CHUNK), 1))
    ws = [jnp.where(tril, sguw_ref[g], 0.0).astype(BF16) for g in range(SGU_GROUPS)]
    q = proj(5)
    a_z = proj(6)
    rows = []
    for c in range(ts // SGU_CHUNK):
        rs = slice(c * SGU_CHUNK, (c + 1) * SGU_CHUNK)
        cols = [_dot(ws[g], vnb[rs, g * SGU_GW:(g + 1) * SGU_GW]) + sgub_ref[:, g:g + 1]
                for g in range(SGU_GROUPS)]
        rows.append(jnp.concatenate(cols, axis=1))
    o_sgu = (u * jnp.concatenate(rows, axis=0) * _silu(s_z)).astype(BF16)

    qb = q.astype(BF16)
    scores = [lax.dot_general(qb[:, hd * XA_HD:(hd + 1) * XA_HD], k_ref[0, :, hd * XA_HD:(hd + 1) * XA_HD],
                              (((1,), (1,)), ((), ())), preferred_element_type=F32) * (XA_HD ** -0.5 * LOG2E)
              for hd in range(XA_HEADS)]
    gates = [proj(7 + n) for n in range(N_BRANCH)]
    heads = []
    for hd in range(XA_HEADS):
        s = scores[hd]
        e = jnp.exp2(s - jnp.max(s, axis=-1, keepdims=True))
        pr = e * (1.0 / jnp.sum(e, axis=-1, keepdims=True))
        heads.append(_dot(pr.astype(BF16), v_ref[0, :, hd * XA_HD:(hd + 1) * XA_HD]))
    o_xa = (jnp.concatenate(heads, axis=1) * _silu(a_z)).astype(BF16)

    merged = None
    for n, o in enumerate((o_pool, o_sgu, o_xa)):
        t = _sigmoid(gates[n]) * _dot(o, wdown_ref[n])
        merged = t if merged is None else merged + t
    xn = x + _dot(merged.astype(BF16), wout_ref[...])
    y_ref[0] = _rmsnorm(xn, gf_ref[...])

    @pl.when(step == pl.num_programs(0) * pl.num_programs(1) - 1)
    def _():
        _last_step(xs_ref, attn_ref, gf_ref, wdown_ref, wout_ref, opool_hbm, osgu_hbm, az_hbm,
                   (g0_hbm, g1_hbm, g2_hbm), stage_ref, last_sem, qs_ref, ys_hbm)


def _prompt_layer(x, kb, vb, gin, win, poolw, pscale, sgug, sguw, sgub_t, wdown, wout, gf,
                  xs_tiles, hwin, sk, sv):
    nb, seq, _ = x.shape
    ts = SEQ_TILE
    n_tiles = seq // ts
    ns = sk.shape[0]
    rb = ns // (nb * n_tiles)
    assert rb * nb * n_tiles == ns
    sample_row = jax.ShapeDtypeStruct((ns, BRANCH_W), F32)
    sample_tiles = jax.ShapeDtypeStruct((ns * ROW_TILES, LANES), F32)
    in_cols = win.shape[1]
    assert in_cols % W_STAGE_COLS == 0 and D_MODEL % W_STAGE_COLS == 0 and D_MODEL % W_STAGE_ROWS == 0
    step = lambda b, i: b * n_tiles + i
    hbm = lambda: pl.BlockSpec(memory_space=pl.ANY)
    return pl.pallas_call(
        _prompt_kernel,
        grid=(nb, n_tiles),
        in_specs=[pl.BlockSpec((1, ts, D_MODEL), lambda b, i: (b, i, 0)),
                  pl.BlockSpec((1, N_MEM, BRANCH_W), lambda b, i: (b, 0, 0)),
                  pl.BlockSpec((1, N_MEM, BRANCH_W), lambda b, i: (b, 0, 0)),
                  _resident((1, D_MODEL)),
                  hbm(),
                  _resident((POOL_GROUPS, POOL_GW, POOL_GW)),
                  _resident((1, BRANCH_W)),
                  _resident((1, BRANCH_W)),
                  _resident((SGU_GROUPS, SGU_CHUNK, SGU_CHUNK)),
                  _resident((SGU_CHUNK, SGU_GROUPS)),
                  hbm(),
                  hbm(),
                  _resident((1, D_MODEL)),
                  _resident((ns * ROW_TILES, LANES)),
                  _resident((ns, BRANCH_W)),
                  pl.BlockSpec((rb, N_MEM, KV_ROWS, LANES), lambda b, i: (step(b, i), 0, 0, 0)),
                  pl.BlockSpec((rb, N_MEM, KV_ROWS, LANES), lambda b, i: (step(b, i), 0, 0, 0))],
        out_specs=[pl.BlockSpec((1, ts, D_MODEL), lambda b, i: (b, i, 0)),
                   pl.BlockSpec((1, HIST_ROWS, BRANCH_W), lambda b, i: (b, 0, 0))] + [hbm()] * 9,
        out_shape=[jax.ShapeDtypeStruct((nb, seq, D_MODEL), F32),
                   jax.ShapeDtypeStruct((nb, HIST_ROWS, BRANCH_W), F32),
                   sample_row, sample_tiles] + [sample_row] * 6 + [sample_tiles],
        scratch_shapes=[pltpu.VMEM((HIST_ROWS + ts, BRANCH_W), F32),
                        pltpu.VMEM((D_MODEL, in_cols), BF16),
                        pltpu.VMEM((N_BRANCH, BRANCH_W, D_MODEL), BF16),
                        pltpu.VMEM((D_MODEL, D_MODEL), BF16),
                        pltpu.VMEM((W_STAGE_SLOTS, W_STAGE_ROWS, W_STAGE_COLS), F32),
                        pltpu.SemaphoreType.DMA((W_STAGE_SLOTS,)),
                        pltpu.VMEM((ns * KV_ROWS, LANES), F32),
                        pltpu.VMEM((ns * KV_ROWS, LANES), F32),
                        pltpu.VMEM((2, ns, BRANCH_W), F32),
                        pltpu.SemaphoreType.DMA((2,)),
                        pltpu.SemaphoreType.DMA((2 * N_BRANCH,))],
        compiler_params=pltpu.CompilerParams(dimension_semantics=("arbitrary", "arbitrary"),
                                             vmem_limit_bytes=V7X_VMEM_LIMIT_BYTES),
        name="prompt_layer",
    )(x, kb, vb, gin, win, poolw, pscale, sgug, sguw, sgub_t, wdown, wout, gf, xs_tiles, hwin, sk, sv)


def _prep_kernel(hist_hbm, mem_ref, gmem_ref, wkv_ref,
                 newhist_hbm, hwin_ref, k_ref, v_ref, kb_ref, vb_ref, wkv_s, hist_s, hist_sem):
    c = pl.program_id(0)
    load = pltpu.make_async_copy(hist_hbm, hist_s, hist_sem.at[0])
    roll = pltpu.make_async_copy(hist_s.at[pl.ds(1, POOL_HIST - 1)], newhist_hbm.at[pl.ds(0, POOL_HIST - 1)],
                                 hist_sem.at[1])
    fill = pltpu.make_async_copy(hist_s.at[0], newhist_hbm.at[POOL_HIST - 1], hist_sem.at[2])

    @pl.when(c == 0)
    def _():
        load.start()
        wkv_s[...] = wkv_ref[...].astype(BF16)

    @pl.when(c == 1)
    def _():
        load.wait()
        for g, w in enumerate(POOL_WINDOWS):
            sl = slice(g * POOL_GW, (g + 1) * POOL_GW)
            win = hist_s[POOL_HIST - 1, :, sl]
            for j in range(2, w):
                win = win + hist_s[POOL_HIST - j, :, sl]
            hwin_ref[:, sl] = win
        roll.start()
        hist_s[0] = jnp.zeros(hist_s.shape[1:], F32)
        fill.start()

    _mem_kv_body(mem_ref, gmem_ref, wkv_s, k_ref, v_ref, kb_ref, vb_ref)

    @pl.when(c == pl.num_programs(0) - 1)
    def _():
        roll.wait()
        fill.wait()


def _prep(hist, mem, gmem, wkv):
    n = hist.shape[1]
    nb = mem.shape[0]
    rq = MEM_REQ_BLOCK
    assert nb % rq == 0 and nb // rq >= 3
    rows_blk = lambda: pl.BlockSpec((rq, N_MEM * KV_ROWS, LANES), lambda b: (b, 0, 0))
    flat_blk = lambda: pl.BlockSpec((rq, N_MEM, BRANCH_W), lambda b: (b, 0, 0))
    hbm = lambda: pl.BlockSpec(memory_space=pl.ANY)
    return pl.pallas_call(
        _prep_kernel,
        grid=(nb // rq,),
        in_specs=[hbm(),
                  pl.BlockSpec((rq, N_MEM, D_MODEL), lambda b: (b, 0, 0)),
                  _resident((1, D_MODEL)),
                  _resident((D_MODEL, 2 * BRANCH_W))],
        out_specs=[hbm(), _resident((n, BRANCH_W)),
                   rows_blk(), rows_blk(), flat_blk(), flat_blk()],
        out_shape=[jax.ShapeDtypeStruct(hist.shape, F32),
                   jax.ShapeDtypeStruct((n, BRANCH_W), F32),
                   jax.ShapeDtypeStruct((nb, N_MEM * KV_ROWS, LANES), F32),
                   jax.ShapeDtypeStruct((nb, N_MEM * KV_ROWS, LANES), F32),
                   jax.ShapeDtypeStruct((nb, N_MEM, BRANCH_W), BF16),
                   jax.ShapeDtypeStruct((nb, N_MEM, BRANCH_W), BF16)],
        scratch_shapes=[pltpu.VMEM((D_MODEL, 2 * BRANCH_W), BF16),
                        pltpu.VMEM(hist.shape, F32),
                        pltpu.SemaphoreType.DMA((3,))],
        compiler_params=pltpu.CompilerParams(dimension_semantics=("arbitrary",),
                                             vmem_limit_bytes=V7X_VMEM_LIMIT_BYTES),
        name="prep",
    )(hist, mem, gmem, wkv)


def kernel(x_prompt, x_sample, state_pool, cache_mem_k, cache_mem_v, mem_prompt, norm_in_g, w_in,
           pool_w, pool_scale, sgu_norm_g, sgu_w, sgu_b, mem_norm_g, w_kv, w_down, w_out, norm_f_g):
    depth = w_in.shape[0]
    assert depth == 1, "single-layer step"
    nb, seq, _ = x_prompt.shape
    ns, dec_seq, _ = x_sample.shape
    assert dec_seq == 1 and seq % SEQ_TILE == 0 and seq >= HIST_ROWS

    row = lambda a: a.reshape(1, -1)
    gin, pscale, sgug, gmem, gf = (row(norm_in_g[0]), row(pool_scale[0]), row(sgu_norm_g[0]),
                                   row(mem_norm_g[0]), row(norm_f_g))
    win, poolw, wkv, wdown, wout, sguw, sgub = (w_in[0], pool_w[0], w_kv[0], w_down[0], w_out[0],
                                                sgu_w[0], sgu_b[0])
    sgub_t = sgub.T

    x_tiles = x_sample.reshape(ns * ROW_TILES, LANES)
    hist_s = jnp.transpose(state_pool[0], (1, 0, 2))
    new_hist_s, hwin_s, k_rows, v_rows, kb, vb = _prep(hist_s, mem_prompt, gmem, wkv)

    y_prompt, hist_p, p_in_s, vn_s, _, _, _, _, _, _, y_tiles = _prompt_layer(
        x_prompt, kb, vb, gin, win, poolw, pscale, sgug, sguw, sgub_t, wdown, wout, gf, x_tiles, hwin_s,
        _to_kv_rows(cache_mem_k[0].reshape(ns, N_MEM, BRANCH_W)),
        _to_kv_rows(cache_mem_v[0].reshape(ns, N_MEM, BRANCH_W)))

    new_pool_p = hist_p[None, :, HIST_ROWS - POOL_HIST:, :]
    new_hist_s = lax.dynamic_update_slice(new_hist_s, p_in_s[None], (POOL_HIST - 1, 0, 0))
    new_pool_s = jnp.transpose(new_hist_s, (1, 0, 2))[None]
    kv_out = lambda a: _from_kv_rows(a.reshape(nb, N_MEM, KV_ROWS, LANES))[None]
    return (y_prompt, y_tiles.reshape(ns, 1, D_MODEL), new_pool_p, new_pool_s,
            kv_out(k_rows), kv_out(v_rows), vn_s.reshape(1, ns, 1, BRANCH_W))
```

```python
import jax
import jax.numpy as jnp
from jax import lax
from jax.experimental import pallas as pl
from jax.experimental.pallas import tpu as pltpu

D_MODEL = 1024
BRANCH_W = 1024
N_BRANCH = 3
N_PROJ = 7 + N_BRANCH
POOL_WINDOWS = (2, 4, 8, 16)
POOL_GROUPS = len(POOL_WINDOWS)
POOL_GW = BRANCH_W // POOL_GROUPS
POOL_HIST = max(POOL_WINDOWS) - 1
HIST_ROWS = POOL_HIST + 1
SGU_CHUNK = 128
SGU_GROUPS = 4
SGU_GW = BRANCH_W // SGU_GROUPS
N_MEM = 256
XA_HEADS = 4
XA_HD = BRANCH_W // XA_HEADS
EPS = 1e-6
PAST_LEN = 16384

SEQ_TILE = 256
MEM_REQ_BLOCK = 2
SAMPLE_ATTN_CHUNK = 16
LANES = 128
XA_LANE_TILES = XA_HD // LANES
KV_ROWS = XA_HEADS * XA_LANE_TILES
LOG2E = 1.4426950408889634
W_STAGE_SLOTS = 8
W_STAGE_ROWS = 128
W_STAGE_COLS = 1024
V7X_VMEM_LIMIT_BYTES = 62 * 1024 * 1024

F32 = jnp.float32
BF16 = jnp.bfloat16

_sigmoid = jax.nn.sigmoid


def _rmsnorm(x, g):
    return x * lax.rsqrt(jnp.mean(x * x, axis=-1, keepdims=True) + EPS) * g


def _silu(z):
    return z * _sigmoid(z)


def _dot(a, b):
    return jnp.dot(a, b, preferred_element_type=F32)


def _resident(shape):
    zeros = (0,) * len(shape)
    return pl.BlockSpec(shape, lambda *_: zeros, pipeline_mode=pl.Buffered(1))


def _to_kv_rows(a):
    lead = a.shape[:-1]
    a = a.reshape(*lead, XA_HEADS, XA_LANE_TILES, LANES)
    return jnp.swapaxes(a, -3, -2).reshape(*lead, KV_ROWS, LANES)


ROW_TILES = D_MODEL // LANES
NATURAL_ORDER = tuple(range(ROW_TILES))
KV_ROW_ORDER = tuple((j % XA_HEADS) * XA_LANE_TILES + j // XA_HEADS for j in range(KV_ROWS))


def _load_row_tiles(ref, order=NATURAL_ORDER):
    n = ref.shape[0] // len(order)
    tiles = [None] * len(order)
    for j, t in enumerate(order):
        tiles[t] = ref[pl.ds(j, n, stride=len(order)), :]
    return jnp.concatenate(tiles, axis=1)


def _store_row_tiles(ref, val, order=NATURAL_ORDER):
    n = val.shape[0]
    for j, t in enumerate(order):
        ref[pl.ds(j, n, stride=len(order)), :] = val[:, t * LANES:(t + 1) * LANES]


def _from_kv_rows(a):
    lead = a.shape[:-2]
    a = a.reshape(*lead, XA_LANE_TILES, XA_HEADS, LANES)
    return jnp.swapaxes(a, -3, -2).reshape(*lead, XA_HEADS, XA_HD)


def _mem_kv_body(mem_ref, g_ref, wkv_s, k_ref, v_ref, kvb_ref):
    n_req = mem_ref.shape[0]
    mem = jnp.concatenate([mem_ref[r] for r in range(n_req)], axis=0)
    kv = _dot(_rmsnorm(mem, g_ref[...]).astype(BF16), wkv_s[...])
    for r in range(n_req):
        kvb_ref[r] = kv[r * N_MEM:(r + 1) * N_MEM, :].astype(BF16)
        k = kv[r * N_MEM:(r + 1) * N_MEM, :BRANCH_W]
        v = kv[r * N_MEM:(r + 1) * N_MEM, BRANCH_W:]
        for h in range(XA_HEADS):
            for lt in range(XA_LANE_TILES):
                cols = slice(h * XA_HD + lt * LANES, h * XA_HD + (lt + 1) * LANES)
                rows = pl.ds(lt * XA_HEADS + h, N_MEM, stride=KV_ROWS)
                k_ref[r, rows, :] = k[:, cols]
                v_ref[r, rows, :] = v[:, cols]


def _stage_weights(copies, stage_ref, sem_ref, after_chunk):
    slots = stage_ref.shape[0]
    ahead = slots - 1

    def chunk_copy(k):
        return pltpu.make_async_copy(copies[k][0], stage_ref.at[k % slots], sem_ref.at[k % slots])

    for k in range(min(ahead, len(copies))):
        chunk_copy(k).start()
    for k, (_, dst) in enumerate(copies):
        if k + ahead < len(copies):
            chunk_copy(k + ahead).start()
        chunk_copy(k).wait()
        dst[...] = stage_ref[k % slots].astype(BF16)
        after_chunk(k)


class _HbmWriter:
    def __init__(self, stage_ref, sem_ref):
        self.stage_ref, self.sem_ref, self.pending, self.count = stage_ref, sem_ref, [], 0

    def write(self, value, dst_hbm):
        slots = self.stage_ref.shape[0]
        slot = self.count % slots
        if len(self.pending) == slots:
            self.pending.pop(0).wait()
        self.stage_ref[slot] = value
        copy = pltpu.make_async_copy(self.stage_ref.at[slot], dst_hbm, self.sem_ref.at[slot])
        copy.start()
        self.pending.append(copy)
        self.count += 1

    def finish(self):
        for copy in self.pending:
            copy.wait()
        self.pending = []


def _first_step(xs_ref, hwin_ref, gin_ref, poolw_ref, pscale_ref, sgug_ref, sguw_ref, sgub_ref,
                win_hbm, wdown_hbm, wout_hbm, win_ref, wdown_ref, wout_ref, stage_ref, stage_sem,
                qs_ref, tiles_ref, tiles_sem, ostage_ref, ostage_sem,
                pin_hbm, vn_hbm, opool_hbm, osgu_hbm, az_hbm, gate_hbms):
    _, sr, sc = stage_ref.shape
    assert sc == BRANCH_W and D_MODEL % sr == 0
    k_slabs = D_MODEL // sr
    tiles = lambda ref: [(pl.ds(r * sr, sr), pl.ds(c * sc, sc))
                         for c in range(ref.shape[-1] // sc) for r in range(ref.shape[-2] // sr)]
    copies = [(win_hbm.at[r, c], win_ref.at[r, c]) for r, c in tiles(win_ref)]
    n_win = len(copies)
    copies += [(wdown_hbm.at[n, r, c], wdown_ref.at[n, r, c]) for n in range(N_BRANCH) for r, c in tiles(wdown_ref)]
    copies += [(wout_hbm.at[r, c], wout_ref.at[r, c]) for r, c in tiles(wout_ref)]

    hbs = _rmsnorm(_load_row_tiles(xs_ref), gin_ref[...]).astype(BF16)
    writer = _HbmWriter(ostage_ref, ostage_sem)
    kept = {}

    def finish_chunk(c, val):
        if c == 0:
            kept["p_in"] = val
            writer.write(val, pin_hbm)
        elif c == 1:
            p_in = kept["p_in"]
            mixed = []
            for g, w in enumerate(POOL_WINDOWS):
                sl = slice(g * POOL_GW, (g + 1) * POOL_GW)
                d = (p_in[:, sl] + hwin_ref[:, sl]) / float(min(w, PAST_LEN + 1)) - p_in[:, sl]
                mixed.append(_dot(d.astype(BF16), poolw_ref[g].astype(BF16)))
            writer.write(jnp.concatenate(mixed, axis=1) * pscale_ref[...] * _silu(val), opool_hbm)
        elif c == 2:
            kept["u"] = val
        elif c == 3:
            vn = _rmsnorm(val, sgug_ref[...])
            _store_row_tiles(tiles_ref, vn)
            kept["vn_copy"] = pltpu.make_async_copy(tiles_ref, vn_hbm, tiles_sem.at[0])
            kept["vn_copy"].start()
            gated = [vn[:, g * SGU_GW:(g + 1) * SGU_GW] * sguw_ref[g, 0:1, 0:1] + sgub_ref[0:1, g:g + 1]
                     for g in range(SGU_GROUPS)]
            kept["u_gated"] = kept["u"] * jnp.concatenate(gated, axis=1)
        elif c == 4:
            writer.write(kept["u_gated"] * _silu(val), osgu_hbm)
        elif c == 5:
            _store_row_tiles(qs_ref, val, KV_ROW_ORDER)
        elif c == 6:
            writer.write(val, az_hbm)
        else:
            writer.write(val, gate_hbms[c - 7])

    def after_chunk(k):
        if k >= n_win:
            return
        c, r = divmod(k, k_slabs)
        part = _dot(hbs[:, r * sr:(r + 1) * sr], win_ref[r * sr:(r + 1) * sr, c * sc:(c + 1) * sc])
        kept["acc"] = part if r == 0 else kept["acc"] + part
        if r == k_slabs - 1:
            finish_chunk(c, kept["acc"])

    _stage_weights(copies, stage_ref, stage_sem, after_chunk)
    writer.finish()
    kept["vn_copy"].wait()


def _last_step(xs_ref, attn_ref, gf_ref, wdown_ref, wout_ref, opool_hbm, osgu_hbm, az_hbm, gate_hbms,
               stage_ref, last_sem, tiles_ref, ys_hbm):
    ns = xs_ref.shape[0] // ROW_TILES
    srcs = (opool_hbm, osgu_hbm, az_hbm) + tuple(gate_hbms)
    per_slot = stage_ref.shape[1] // ns
    bufs = [stage_ref.at[k // per_slot, pl.ds((k % per_slot) * ns, ns), :] for k in range(len(srcs))]
    copies = [pltpu.make_async_copy(src, buf, last_sem.at[k]) for k, (src, buf) in enumerate(zip(srcs, bufs))]
    for copy in copies:
        copy.start()
    for copy in copies:
        copy.wait()
    o_pool, o_sgu, a_z, *gates = [buf[...] for buf in bufs]
    o_xa = _load_row_tiles(attn_ref, KV_ROW_ORDER) * _silu(a_z)
    merged = None
    for n, o in enumerate((o_pool, o_sgu, o_xa)):
        t = _sigmoid(gates[n]) * _dot(o.astype(BF16), wdown_ref[n])
        merged = t if merged is None else merged + t
    xn = _load_row_tiles(xs_ref) + _dot(merged.astype(BF16), wout_ref[...])
    _store_row_tiles(tiles_ref, _rmsnorm(xn, gf_ref[...]))
    out = pltpu.make_async_copy(tiles_ref, ys_hbm, last_sem.at[0])
    out.start()
    out.wait()


def _sample_attention(get_q, k_ref, v_ref, put_o):
    for r in range(k_ref.shape[0]):
        q = get_q(r) * (XA_HD ** -0.5 * LOG2E)
        m = l = acc = None
        for c in range(N_MEM // SAMPLE_ATTN_CHUNK):
            rows = slice(c * SAMPLE_ATTN_CHUNK, (c + 1) * SAMPLE_ATTN_CHUNK)
            part = k_ref[r, rows] * q
            part = part + pltpu.roll(part, XA_HEADS, axis=1)
            s = jnp.sum(part, axis=-1, keepdims=True)
            m_chunk = jnp.max(s, axis=0, keepdims=True)
            m_new = m_chunk if m is None else jnp.maximum(m, m_chunk)
            e = jnp.exp2(s - m_new)
            l_chunk = jnp.sum(e, axis=0)
            acc_chunk = jnp.sum(e * v_ref[r, rows], axis=0)
            if m is None:
                l, acc = l_chunk, acc_chunk
            else:
                rescale = jnp.exp2(m - m_new)[0]
                l, acc = l * rescale + l_chunk, acc * rescale + acc_chunk
            m = m_new
        put_o(r, acc / l)


def _prompt_kernel(x_ref, kv_ref, gin_ref, win_hbm, poolw_ref, pscale_ref, sgug_ref,
                   sguw_ref, sgub_ref, wdown_hbm, wout_hbm, gf_ref, xs_ref, hwin_ref, sk_ref, sv_ref,
                   y_ref, hist_ref, pin_hbm, vn_hbm, opool_hbm, osgu_hbm, az_hbm, g0_hbm, g1_hbm, g2_hbm, ys_hbm,
                   ext_ref, win_ref, wdown_ref, wout_ref, stage_ref, stage_sem, qs_ref, attn_ref,
                   ostage_ref, ostage_sem, last_sem):
    i = pl.program_id(1)
    ts = x_ref.shape[1]
    step = pl.program_id(0) * pl.num_programs(1) + i

    @pl.when(step == 0)
    def _():
        _first_step(xs_ref, hwin_ref, gin_ref, poolw_ref, pscale_ref, sgug_ref, sguw_ref, sgub_ref,
                    win_hbm, wdown_hbm, wout_hbm, win_ref, wdown_ref, wout_ref, stage_ref, stage_sem,
                    qs_ref, attn_ref, last_sem, ostage_ref, ostage_sem,
                    pin_hbm, vn_hbm, opool_hbm, osgu_hbm, az_hbm, (g0_hbm, g1_hbm, g2_hbm))

    @pl.when(i == 0)
    def _():
        ext_ref[0:HIST_ROWS, :] = jnp.zeros((HIST_ROWS, BRANCH_W), F32)

    q_rows = sk_ref.shape[0] * KV_ROWS
    q_blk = qs_ref[pl.ds(pl.multiple_of(step * q_rows, q_rows), q_rows), :]

    def put_attention(r, out):
        attn_ref[pl.ds(pl.multiple_of(step * q_rows + r * KV_ROWS, KV_ROWS), KV_ROWS), :] = out

    _sample_attention(lambda r: q_blk[r * KV_ROWS:(r + 1) * KV_ROWS, :], sk_ref, sv_ref, put_attention)

    x = x_ref[0]
    hb = _rmsnorm(x, gin_ref[...]).astype(BF16)

    def proj(c):
        return _dot(hb, win_ref[:, c * BRANCH_W:(c + 1) * BRANCH_W])


    p_in = proj(0)
    p_z = proj(1)
    ext_ref[HIST_ROWS:HIST_ROWS + ts, :] = p_in
    pos = i * ts + lax.broadcasted_iota(jnp.int32, (ts, 1), 0)
    pooled = []
    for g, w in enumerate(POOL_WINDOWS):
        sl = slice(g * POOL_GW, (g + 1) * POOL_GW)
        win = ext_ref[:, sl]
        span = 1
        while span < w:
            win = win + pltpu.roll(win, span, axis=0)
            span *= 2
        inv_cnt = 1.0 / jnp.minimum(w, pos + 1).astype(F32)
        pooled.append((win[HIST_ROWS:, :] * inv_cnt - p_in[:, sl]).astype(BF16))
    ext_ref[0:HIST_ROWS, :] = p_in[ts - HIST_ROWS:, :]
    hist_ref[0] = p_in[ts - HIST_ROWS:, :]

    v = proj(3)
    u = proj(2)
    s_z = proj(4)
    mixed = [_dot(pooled[g], poolw_ref[g].astype(BF16)) for g in range(POOL_GROUPS)]
    o_pool = (jnp.concatenate(mixed, axis=1) * pscale_ref[...] * _silu(p_z)).astype(BF16)

    vnb = _rmsnorm(v, sgug_ref[...]).astype(BF16)
    tril = (lax.broadcasted_iota(jnp.int32, (SGU_CHUNK, SGU_CHUNK), 0)
            >= lax.broadcasted_iota(jnp.int32, (SGU_CHUNK, SGU_CHUNK), 1))
    ws = [jnp.where(tril, sguw_ref[g], 0.0).astype(BF16) for g in range(SGU_GROUPS)]
    q = proj(5)
    a_z = proj(6)
    rows = []
    for c in range(ts // SGU_CHUNK):
        rs = slice(c * SGU_CHUNK, (c + 1) * SGU_CHUNK)
        cols = [_dot(ws[g], vnb[rs, g * SGU_GW:(g + 1) * SGU_GW]) + sgub_ref[:, g:g + 1]
                for g in range(SGU_GROUPS)]
        rows.append(jnp.concatenate(cols, axis=1))
    o_sgu = (u * jnp.concatenate(rows, axis=0) * _silu(s_z)).astype(BF16)

    qb = q.astype(BF16)
    scores = [lax.dot_general(qb[:, hd * XA_HD:(hd + 1) * XA_HD], kv_ref[0, :, hd * XA_HD:(hd + 1) * XA_HD],
                              (((1,), (1,)), ((), ())), preferred_element_type=F32) * (XA_HD ** -0.5 * LOG2E)
              for hd in range(XA_HEADS)]
    gates = [proj(7 + n) for n in range(N_BRANCH)]
    heads = []
    for hd in range(XA_HEADS):
        s = scores[hd]
        e = jnp.exp2(s - jnp.max(s, axis=-1, keepdims=True))
        pr = e * (1.0 / jnp.sum(e, axis=-1, keepdims=True))
        heads.append(_dot(pr.astype(BF16), kv_ref[0, :, BRANCH_W + hd * XA_HD:BRANCH_W + (hd + 1) * XA_HD]))
    o_xa = (jnp.concatenate(heads, axis=1) * _silu(a_z)).astype(BF16)

    merged = None
    for n, o in enumerate((o_pool, o_sgu, o_xa)):
        t = _sigmoid(gates[n]) * _dot(o, wdown_ref[n])
        merged = t if merged is None else merged + t
    xn = x + _dot(merged.astype(BF16), wout_ref[...])
    y_ref[0] = _rmsnorm(xn, gf_ref[...])

    @pl.when(step == pl.num_programs(0) * pl.num_programs(1) - 1)
    def _():
        _last_step(xs_ref, attn_ref, gf_ref, wdown_ref, wout_ref, opool_hbm, osgu_hbm, az_hbm,
                   (g0_hbm, g1_hbm, g2_hbm), stage_ref, last_sem, qs_ref, ys_hbm)


def _prompt_layer(x, kvb, gin, win, poolw, pscale, sgug, sguw, sgub_t, wdown, wout, gf,
                  xs_tiles, hwin, sk, sv):
    nb, seq, _ = x.shape
    ts = SEQ_TILE
    n_tiles = seq // ts
    ns = sk.shape[0]
    rb = ns // (nb * n_tiles)
    assert rb * nb * n_tiles == ns
    sample_row = jax.ShapeDtypeStruct((ns, BRANCH_W), F32)
    sample_tiles = jax.ShapeDtypeStruct((ns * ROW_TILES, LANES), F32)
    in_cols = win.shape[1]
    assert in_cols % W_STAGE_COLS == 0 and D_MODEL % W_STAGE_COLS == 0 and D_MODEL % W_STAGE_ROWS == 0
    step = lambda b, i: b * n_tiles + i
    hbm = lambda: pl.BlockSpec(memory_space=pl.ANY)
    return pl.pallas_call(
        _prompt_kernel,
        grid=(nb, n_tiles),
        in_specs=[pl.BlockSpec((1, ts, D_MODEL), lambda b, i: (b, i, 0)),
                  pl.BlockSpec((1, N_MEM, 2 * BRANCH_W), lambda b, i: (b, 0, 0)),
                  _resident((1, D_MODEL)),
                  hbm(),
                  _resident((POOL_GROUPS, POOL_GW, POOL_GW)),
                  _resident((1, BRANCH_W)),
                  _resident((1, BRANCH_W)),
                  _resident((SGU_GROUPS, SGU_CHUNK, SGU_CHUNK)),
                  _resident((SGU_CHUNK, SGU_GROUPS)),
                  hbm(),
                  hbm(),
                  _resident((1, D_MODEL)),
                  _resident((ns * ROW_TILES, LANES)),
                  _resident((ns, BRANCH_W)),
                  pl.BlockSpec((rb, N_MEM, KV_ROWS, LANES), lambda b, i: (step(b, i), 0, 0, 0)),
                  pl.BlockSpec((rb, N_MEM, KV_ROWS, LANES), lambda b, i: (step(b, i), 0, 0, 0))],
        out_specs=[pl.BlockSpec((1, ts, D_MODEL), lambda b, i: (b, i, 0)),
                   pl.BlockSpec((1, HIST_ROWS, BRANCH_W), lambda b, i: (b, 0, 0))] + [hbm()] * 9,
        out_shape=[jax.ShapeDtypeStruct((nb, seq, D_MODEL), F32),
                   jax.ShapeDtypeStruct((nb, HIST_ROWS, BRANCH_W), F32),
                   sample_row, sample_tiles] + [sample_row] * 6 + [sample_tiles],
        scratch_shapes=[pltpu.VMEM((HIST_ROWS + ts, BRANCH_W), F32),
                        pltpu.VMEM((D_MODEL, in_cols), BF16),
                        pltpu.VMEM((N_BRANCH, BRANCH_W, D_MODEL), BF16),
                        pltpu.VMEM((D_MODEL, D_MODEL), BF16),
                        pltpu.VMEM((W_STAGE_SLOTS, W_STAGE_ROWS, W_STAGE_COLS), F32),
                        pltpu.SemaphoreType.DMA((W_STAGE_SLOTS,)),
                        pltpu.VMEM((ns * KV_ROWS, LANES), F32),
                        pltpu.VMEM((ns * KV_ROWS, LANES), F32),
                        pltpu.VMEM((2, ns, BRANCH_W), F32),
                        pltpu.SemaphoreType.DMA((2,)),
                        pltpu.SemaphoreType.DMA((2 * N_BRANCH,))],
        compiler_params=pltpu.CompilerParams(dimension_semantics=("arbitrary", "arbitrary"),
                                             vmem_limit_bytes=V7X_VMEM_LIMIT_BYTES),
        name="prompt_layer",
    )(x, kvb, gin, win, poolw, pscale, sgug, sguw, sgub_t, wdown, wout, gf, xs_tiles, hwin, sk, sv)


def _prep_kernel(hist_hbm, mem_ref, gmem_ref, wkv_ref,
                 newhist_hbm, hwin_ref, k_ref, v_ref, kvb_ref, wkv_s, hist_s, hist_sem):
    c = pl.program_id(0)
    load = pltpu.make_async_copy(hist_hbm, hist_s, hist_sem.at[0])
    roll = pltpu.make_async_copy(hist_s.at[pl.ds(1, POOL_HIST - 1)], newhist_hbm.at[pl.ds(0, POOL_HIST - 1)],
                                 hist_sem.at[1])
    fill = pltpu.make_async_copy(hist_s.at[0], newhist_hbm.at[POOL_HIST - 1], hist_sem.at[2])

    @pl.when(c == 0)
    def _():
        load.start()
        wkv_s[...] = wkv_ref[...].astype(BF16)

    @pl.when(c == 1)
    def _():
        load.wait()
        for g, w in enumerate(POOL_WINDOWS):
            sl = slice(g * POOL_GW, (g + 1) * POOL_GW)
            win = hist_s[POOL_HIST - 1, :, sl]
            for j in range(2, w):
                win = win + hist_s[POOL_HIST - j, :, sl]
            hwin_ref[:, sl] = win
        roll.start()
        hist_s[0] = jnp.zeros(hist_s.shape[1:], F32)
        fill.start()

    _mem_kv_body(mem_ref, gmem_ref, wkv_s, k_ref, v_ref, kvb_ref)

    @pl.when(c == pl.num_programs(0) - 1)
    def _():
        roll.wait()
        fill.wait()


def _prep(hist, mem, gmem, wkv):
    n = hist.shape[1]
    nb = mem.shape[0]
    rq = MEM_REQ_BLOCK
    assert nb % rq == 0 and nb // rq >= 3
    rows_blk = lambda: pl.BlockSpec((rq, N_MEM * KV_ROWS, LANES), lambda b: (b, 0, 0))
    flat_blk = lambda: pl.BlockSpec((rq, N_MEM, 2 * BRANCH_W), lambda b: (b, 0, 0))
    hbm = lambda: pl.BlockSpec(memory_space=pl.ANY)
    return pl.pallas_call(
        _prep_kernel,
        grid=(nb // rq,),
        in_specs=[hbm(),
                  pl.BlockSpec((rq, N_MEM, D_MODEL), lambda b: (b, 0, 0)),
                  _resident((1, D_MODEL)),
                  _resident((D_MODEL, 2 * BRANCH_W))],
        out_specs=[hbm(), _resident((n, BRANCH_W)),
                   rows_blk(), rows_blk(), flat_blk()],
        out_shape=[jax.ShapeDtypeStruct(hist.shape, F32),
                   jax.ShapeDtypeStruct((n, BRANCH_W), F32),
                   jax.ShapeDtypeStruct((nb, N_MEM * KV_ROWS, LANES), F32),
                   jax.ShapeDtypeStruct((nb, N_MEM * KV_ROWS, LANES), F32),
                   jax.ShapeDtypeStruct((nb, N_MEM, 2 * BRANCH_W), BF16)],
        scratch_shapes=[pltpu.VMEM((D_MODEL, 2 * BRANCH_W), BF16),
                        pltpu.VMEM(hist.shape, F32),
                        pltpu.SemaphoreType.DMA((3,))],
        compiler_params=pltpu.CompilerParams(dimension_semantics=("arbitrary",),
                                             vmem_limit_bytes=V7X_VMEM_LIMIT_BYTES),
        name="prep",
    )(hist, mem, gmem, wkv)


def kernel(x_prompt, x_sample, state_pool, cache_mem_k, cache_mem_v, mem_prompt, norm_in_g, w_in,
           pool_w, pool_scale, sgu_norm_g, sgu_w, sgu_b, mem_norm_g, w_kv, w_down, w_out, norm_f_g):
    depth = w_in.shape[0]
    assert depth == 1, "single-layer step"
    nb, seq, _ = x_prompt.shape
    ns, dec_seq, _ = x_sample.shape
    assert dec_seq == 1 and seq % SEQ_TILE == 0 and seq >= HIST_ROWS

    row = lambda a: a.reshape(1, -1)
    gin, pscale, sgug, gmem, gf = (row(norm_in_g[0]), row(pool_scale[0]), row(sgu_norm_g[0]),
                                   row(mem_norm_g[0]), row(norm_f_g))
    win, poolw, wkv, wdown, wout, sguw, sgub = (w_in[0], pool_w[0], w_kv[0], w_down[0], w_out[0],
                                                sgu_w[0], sgu_b[0])
    sgub_t = sgub.T

    x_tiles = x_sample.reshape(ns * ROW_TILES, LANES)
    hist_s = jnp.transpose(state_pool[0], (1, 0, 2))
    new_hist_s, hwin_s, k_rows, v_rows, kvb = _prep(hist_s, mem_prompt, gmem, wkv)

    y_prompt, hist_p, p_in_s, vn_s, _, _, _, _, _, _, y_tiles = _prompt_layer(
        x_prompt, kvb, gin, win, poolw, pscale, sgug, sguw, sgub_t, wdown, wout, gf, x_tiles, hwin_s,
        _to_kv_rows(cache_mem_k[0].reshape(ns, N_MEM, BRANCH_W)),
        _to_kv_rows(cache_mem_v[0].reshape(ns, N_MEM, BRANCH_W)))

    new_pool_p = hist_p[None, :, HIST_ROWS - POOL_HIST:, :]
    new_hist_s = lax.dynamic_update_slice(new_hist_s, p_in_s[None], (POOL_HIST - 1, 0, 0))
    new_pool_s = jnp.transpose(new_hist_s, (1, 0, 2))[None]
    kv_out = lambda a: _from_kv_rows(a.reshape(nb, N_MEM, KV_ROWS, LANES))[None]
    return (y_prompt, y_tiles.reshape(ns, 1, D_MODEL), new_pool_p, new_pool_s,
            kv_out(k_rows), kv_out(v_rows), vn_s.reshape(1, ns, 1, BRANCH_W))
```

```python
import jax
import jax.numpy as jnp
from jax import lax
from jax.experimental import pallas as pl
from jax.experimental.pallas import tpu as pltpu

D_MODEL = 1024
BRANCH_W = 1024
N_BRANCH = 3
N_PROJ = 7 + N_BRANCH
POOL_WINDOWS = (2, 4, 8, 16)
POOL_GROUPS = len(POOL_WINDOWS)
POOL_GW = BRANCH_W // POOL_GROUPS
POOL_HIST = max(POOL_WINDOWS) - 1
HIST_ROWS = POOL_HIST + 1
SGU_CHUNK = 128
SGU_GROUPS = 4
SGU_GW = BRANCH_W // SGU_GROUPS
N_MEM = 256
XA_HEADS = 4
XA_HD = BRANCH_W // XA_HEADS
EPS = 1e-6
PAST_LEN = 16384

SEQ_TILE = 256
MEM_REQ_BLOCK = 2
SAMPLE_ATTN_CHUNK = 16
LANES = 128
XA_LANE_TILES = XA_HD // LANES
KV_ROWS = XA_HEADS * XA_LANE_TILES
LOG2E = 1.4426950408889634
W_STAGE_SLOTS = 8
W_STAGE_ROWS = 128
W_STAGE_COLS = 1024
V7X_VMEM_LIMIT_BYTES = 62 * 1024 * 1024

F32 = jnp.float32
BF16 = jnp.bfloat16

_sigmoid = jax.nn.sigmoid


def _rmsnorm(x, g):
    return x * lax.rsqrt(jnp.mean(x * x, axis=-1, keepdims=True) + EPS) * g


def _silu(z):
    return z * _sigmoid(z)


def _dot(a, b):
    return jnp.dot(a, b, preferred_element_type=F32)


def _resident(shape):
    zeros = (0,) * len(shape)
    return pl.BlockSpec(shape, lambda *_: zeros, pipeline_mode=pl.Buffered(1))


def _to_kv_rows(a):
    lead = a.shape[:-1]
    a = a.reshape(*lead, XA_HEADS, XA_LANE_TILES, LANES)
    return jnp.swapaxes(a, -3, -2).reshape(*lead, KV_ROWS, LANES)


ROW_TILES = D_MODEL // LANES
NATURAL_ORDER = tuple(range(ROW_TILES))
KV_ROW_ORDER = tuple((j % XA_HEADS) * XA_LANE_TILES + j // XA_HEADS for j in range(KV_ROWS))


def _load_row_tiles(ref, order=NATURAL_ORDER):
    n = ref.shape[0] // len(order)
    tiles = [None] * len(order)
    for j, t in enumerate(order):
        tiles[t] = ref[pl.ds(j, n, stride=len(order)), :]
    return jnp.concatenate(tiles, axis=1)


def _store_row_tiles(ref, val, order=NATURAL_ORDER):
    n = val.shape[0]
    for j, t in enumerate(order):
        ref[pl.ds(j, n, stride=len(order)), :] = val[:, t * LANES:(t + 1) * LANES]


def _from_kv_rows(a):
    lead = a.shape[:-2]
    a = a.reshape(*lead, XA_LANE_TILES, XA_HEADS, LANES)
    return jnp.swapaxes(a, -3, -2).reshape(*lead, XA_HEADS, XA_HD)


def _mem_kv_body(mem_ref, g_ref, wkv_s, k_ref, v_ref, kvb_ref):
    n_req = mem_ref.shape[0]
    mem = jnp.concatenate([mem_ref[r] for r in range(n_req)], axis=0)
    kv = _dot(_rmsnorm(mem, g_ref[...]).astype(BF16), wkv_s[...])
    assert N_MEM == XA_HD
    for r in range(n_req):
        k = kv[r * N_MEM:(r + 1) * N_MEM, :BRANCH_W]
        v = kv[r * N_MEM:(r + 1) * N_MEM, BRANCH_W:]
        k_t = [k[:, h * XA_HD:(h + 1) * XA_HD].T for h in range(XA_HEADS)]
        kvb_ref[r] = jnp.concatenate(k_t + [v], axis=1).astype(BF16)
        for h in range(XA_HEADS):
            for lt in range(XA_LANE_TILES):
                cols = slice(h * XA_HD + lt * LANES, h * XA_HD + (lt + 1) * LANES)
                rows = pl.ds(lt * XA_HEADS + h, N_MEM, stride=KV_ROWS)
                k_ref[r, rows, :] = k[:, cols]
                v_ref[r, rows, :] = v[:, cols]


def _stage_weights(copies, stage_ref, sem_ref, after_chunk):
    slots = stage_ref.shape[0]
    ahead = slots - 1

    def chunk_copy(k):
        return pltpu.make_async_copy(copies[k][0], stage_ref.at[k % slots], sem_ref.at[k % slots])

    for k in range(min(ahead, len(copies))):
        chunk_copy(k).start()
    for k, (_, dst) in enumerate(copies):
        if k + ahead < len(copies):
            chunk_copy(k + ahead).start()
        chunk_copy(k).wait()
        dst[...] = stage_ref[k % slots].astype(BF16)
        after_chunk(k)


class _HbmWriter:
    def __init__(self, stage_ref, sem_ref):
        self.stage_ref, self.sem_ref, self.pending, self.count = stage_ref, sem_ref, [], 0

    def write(self, value, dst_hbm):
        slots = self.stage_ref.shape[0]
        slot = self.count % slots
        if len(self.pending) == slots:
            self.pending.pop(0).wait()
        self.stage_ref[slot] = value
        copy = pltpu.make_async_copy(self.stage_ref.at[slot], dst_hbm, self.sem_ref.at[slot])
        copy.start()
        self.pending.append(copy)
        self.count += 1

    def finish(self):
        for copy in self.pending:
            copy.wait()
        self.pending = []


def _first_step(xs_ref, hwin_ref, gin_ref, poolw_ref, pscale_ref, sgug_ref, sguw_ref, sgub_ref,
                win_hbm, wdown_hbm, wout_hbm, win_ref, wdown_ref, wout_ref, stage_ref, stage_sem,
                qs_ref, tiles_ref, tiles_sem, ostage_ref, ostage_sem,
                pin_hbm, vn_hbm, opool_hbm, osgu_hbm, az_hbm, gate_hbms):
    _, sr, sc = stage_ref.shape
    assert sc == BRANCH_W and D_MODEL % sr == 0
    k_slabs = D_MODEL // sr
    tiles = lambda ref: [(pl.ds(r * sr, sr), pl.ds(c * sc, sc))
                         for c in range(ref.shape[-1] // sc) for r in range(ref.shape[-2] // sr)]
    copies = [(win_hbm.at[r, c], win_ref.at[r, c]) for r, c in tiles(win_ref)]
    n_win = len(copies)
    copies += [(wdown_hbm.at[n, r, c], wdown_ref.at[n, r, c]) for n in range(N_BRANCH) for r, c in tiles(wdown_ref)]
    copies += [(wout_hbm.at[r, c], wout_ref.at[r, c]) for r, c in tiles(wout_ref)]

    hbs = _rmsnorm(_load_row_tiles(xs_ref), gin_ref[...]).astype(BF16)
    writer = _HbmWriter(ostage_ref, ostage_sem)
    kept = {}

    def finish_chunk(c, val):
        if c == 0:
            kept["p_in"] = val
            writer.write(val, pin_hbm)
        elif c == 1:
            p_in = kept["p_in"]
            mixed = []
            for g, w in enumerate(POOL_WINDOWS):
                sl = slice(g * POOL_GW, (g + 1) * POOL_GW)
                d = (p_in[:, sl] + hwin_ref[:, sl]) / float(min(w, PAST_LEN + 1)) - p_in[:, sl]
                mixed.append(_dot(d.astype(BF16), poolw_ref[g].astype(BF16)))
            writer.write(jnp.concatenate(mixed, axis=1) * pscale_ref[...] * _silu(val), opool_hbm)
        elif c == 2:
            kept["u"] = val
        elif c == 3:
            vn = _rmsnorm(val, sgug_ref[...])
            _store_row_tiles(tiles_ref, vn)
            kept["vn_copy"] = pltpu.make_async_copy(tiles_ref, vn_hbm, tiles_sem.at[0])
            kept["vn_copy"].start()
            gated = [vn[:, g * SGU_GW:(g + 1) * SGU_GW] * sguw_ref[g, 0:1, 0:1] + sgub_ref[0:1, g:g + 1]
                     for g in range(SGU_GROUPS)]
            kept["u_gated"] = kept["u"] * jnp.concatenate(gated, axis=1)
        elif c == 4:
            writer.write(kept["u_gated"] * _silu(val), osgu_hbm)
        elif c == 5:
            _store_row_tiles(qs_ref, val, KV_ROW_ORDER)
        elif c == 6:
            writer.write(val, az_hbm)
        else:
            writer.write(val, gate_hbms[c - 7])

    def after_chunk(k):
        if k >= n_win:
            return
        c, r = divmod(k, k_slabs)
        part = _dot(hbs[:, r * sr:(r + 1) * sr], win_ref[r * sr:(r + 1) * sr, c * sc:(c + 1) * sc])
        kept["acc"] = part if r == 0 else kept["acc"] + part
        if r == k_slabs - 1:
            finish_chunk(c, kept["acc"])

    _stage_weights(copies, stage_ref, stage_sem, after_chunk)
    writer.finish()
    kept["vn_copy"].wait()


def _last_step(xs_ref, attn_ref, gf_ref, wdown_ref, wout_ref, opool_hbm, osgu_hbm, az_hbm, gate_hbms,
               stage_ref, last_sem, tiles_ref, ys_hbm):
    ns = xs_ref.shape[0] // ROW_TILES
    srcs = (opool_hbm, osgu_hbm, az_hbm) + tuple(gate_hbms)
    per_slot = stage_ref.shape[1] // ns
    bufs = [stage_ref.at[k // per_slot, pl.ds((k % per_slot) * ns, ns), :] for k in range(len(srcs))]
    copies = [pltpu.make_async_copy(src, buf, last_sem.at[k]) for k, (src, buf) in enumerate(zip(srcs, bufs))]
    for copy in copies:
        copy.start()
    for copy in copies:
        copy.wait()
    o_pool, o_sgu, a_z, *gates = [buf[...] for buf in bufs]
    o_xa = _load_row_tiles(attn_ref, KV_ROW_ORDER) * _silu(a_z)
    merged = None
    for n, o in enumerate((o_pool, o_sgu, o_xa)):
        t = _sigmoid(gates[n]) * _dot(o.astype(BF16), wdown_ref[n])
        merged = t if merged is None else merged + t
    xn = _load_row_tiles(xs_ref) + _dot(merged.astype(BF16), wout_ref[...])
    _store_row_tiles(tiles_ref, _rmsnorm(xn, gf_ref[...]))
    out = pltpu.make_async_copy(tiles_ref, ys_hbm, last_sem.at[0])
    out.start()
    out.wait()


def _sample_attention(get_q, k_ref, v_ref, put_o):
    for r in range(k_ref.shape[0]):
        q = get_q(r) * (XA_HD ** -0.5 * LOG2E)
        m = l = acc = None
        for c in range(N_MEM // SAMPLE_ATTN_CHUNK):
            rows = slice(c * SAMPLE_ATTN_CHUNK, (c + 1) * SAMPLE_ATTN_CHUNK)
            part = k_ref[r, rows] * q
            part = part + pltpu.roll(part, XA_HEADS, axis=1)
            s = jnp.sum(part, axis=-1, keepdims=True)
            m_chunk = jnp.max(s, axis=0, keepdims=True)
            m_new = m_chunk if m is None else jnp.maximum(m, m_chunk)
            e = jnp.exp2(s - m_new)
            l_chunk = jnp.sum(e, axis=0)
            acc_chunk = jnp.sum(e * v_ref[r, rows], axis=0)
            if m is None:
                l, acc = l_chunk, acc_chunk
            else:
                rescale = jnp.exp2(m - m_new)[0]
                l, acc = l * rescale + l_chunk, acc * rescale + acc_chunk
            m = m_new
        put_o(r, acc / l)


def _prompt_kernel(x_ref, kv_ref, gin_ref, win_hbm, poolw_ref, pscale_ref, sgug_ref,
                   sguw_ref, sgub_ref, wdown_hbm, wout_hbm, gf_ref, xs_ref, hwin_ref, sk_ref, sv_ref,
                   y_ref, hist_ref, pin_hbm, vn_hbm, opool_hbm, osgu_hbm, az_hbm, g0_hbm, g1_hbm, g2_hbm, ys_hbm,
                   ext_ref, win_ref, wdown_ref, wout_ref, stage_ref, stage_sem, qs_ref, attn_ref,
                   ostage_ref, ostage_sem, last_sem):
    i = pl.program_id(1)
    ts = x_ref.shape[1]
    step = pl.program_id(0) * pl.num_programs(1) + i

    @pl.when(step == 0)
    def _():
        _first_step(xs_ref, hwin_ref, gin_ref, poolw_ref, pscale_ref, sgug_ref, sguw_ref, sgub_ref,
                    win_hbm, wdown_hbm, wout_hbm, win_ref, wdown_ref, wout_ref, stage_ref, stage_sem,
                    qs_ref, attn_ref, last_sem, ostage_ref, ostage_sem,
                    pin_hbm, vn_hbm, opool_hbm, osgu_hbm, az_hbm, (g0_hbm, g1_hbm, g2_hbm))

    @pl.when(i == 0)
    def _():
        ext_ref[0:HIST_ROWS, :] = jnp.zeros((HIST_ROWS, BRANCH_W), F32)

    q_rows = sk_ref.shape[0] * KV_ROWS
    q_blk = qs_ref[pl.ds(pl.multiple_of(step * q_rows, q_rows), q_rows), :]

    def put_attention(r, out):
        attn_ref[pl.ds(pl.multiple_of(step * q_rows + r * KV_ROWS, KV_ROWS), KV_ROWS), :] = out

    _sample_attention(lambda r: q_blk[r * KV_ROWS:(r + 1) * KV_ROWS, :], sk_ref, sv_ref, put_attention)

    x = x_ref[0]
    hb = _rmsnorm(x, gin_ref[...]).astype(BF16)

    def proj(c):
        return _dot(hb, win_ref[:, c * BRANCH_W:(c + 1) * BRANCH_W])


    p_in = proj(0)
    p_z = proj(1)
    ext_ref[HIST_ROWS:HIST_ROWS + ts, :] = p_in
    pos = i * ts + lax.broadcasted_iota(jnp.int32, (ts, 1), 0)
    pooled = []
    for g, w in enumerate(POOL_WINDOWS):
        sl = slice(g * POOL_GW, (g + 1) * POOL_GW)
        win = ext_ref[:, sl]
        span = 1
        while span < w:
            win = win + pltpu.roll(win, span, axis=0)
            span *= 2
        inv_cnt = 1.0 / jnp.minimum(w, pos + 1).astype(F32)
        pooled.append((win[HIST_ROWS:, :] * inv_cnt - p_in[:, sl]).astype(BF16))
    ext_ref[0:HIST_ROWS, :] = p_in[ts - HIST_ROWS:, :]
    hist_ref[0] = p_in[ts - HIST_ROWS:, :]

    v = proj(3)
    u = proj(2)
    s_z = proj(4)
    mixed = [_dot(pooled[g], poolw_ref[g].astype(BF16)) for g in range(POOL_GROUPS)]
    o_pool = (jnp.concatenate(mixed, axis=1) * pscale_ref[...] * _silu(p_z)).astype(BF16)

    vnb = _rmsnorm(v, sgug_ref[...]).astype(BF16)
    tril = (lax.broadcasted_iota(jnp.int32, (SGU_CHUNK, SGU_CHUNK), 0)
            >= lax.broadcasted_iota(jnp.int32, (SGU_CHUNK, SGU_CHUNK), 1))
    ws = [jnp.where(tril, sguw_ref[g], 0.0).astype(BF16) for g in range(SGU_GROUPS)]
    q = proj(5)
    a_z = proj(6)
    rows = []
    for c in range(ts // SGU_CHUNK):
        rs = slice(c * SGU_CHUNK, (c + 1) * SGU_CHUNK)
        cols = [_dot(ws[g], vnb[rs, g * SGU_GW:(g + 1) * SGU_GW]) + sgub_ref[:, g:g + 1]
                for g in range(SGU_GROUPS)]
        rows.append(jnp.concatenate(cols, axis=1))
    o_sgu = (u * jnp.concatenate(rows, axis=0) * _silu(s_z)).astype(BF16)

    qb = q.astype(BF16)
    scores = [_dot(qb[:, hd * XA_HD:(hd + 1) * XA_HD], kv_ref[0, :, hd * N_MEM:(hd + 1) * N_MEM])
              * (XA_HD ** -0.5 * LOG2E) for hd in range(XA_HEADS)]
    gates = [proj(7 + n) for n in range(N_BRANCH)]
    heads = []
    for hd in range(XA_HEADS):
        s = scores[hd]
        e = jnp.exp2(s - jnp.max(s, axis=-1, keepdims=True))
        pr = e * (1.0 / jnp.sum(e, axis=-1, keepdims=True))
        heads.append(_dot(pr.astype(BF16), kv_ref[0, :, BRANCH_W + hd * XA_HD:BRANCH_W + (hd + 1) * XA_HD]))
    o_xa = (jnp.concatenate(heads, axis=1) * _silu(a_z)).astype(BF16)

    merged = None
    for n, o in enumerate((o_pool, o_sgu, o_xa)):
        t = _sigmoid(gates[n]) * _dot(o, wdown_ref[n])
        merged = t if merged is None else merged + t
    xn = x + _dot(merged.astype(BF16), wout_ref[...])
    y_ref[0] = _rmsnorm(xn, gf_ref[...])

    @pl.when(step == pl.num_programs(0) * pl.num_programs(1) - 1)
    def _():
        _last_step(xs_ref, attn_ref, gf_ref, wdown_ref, wout_ref, opool_hbm, osgu_hbm, az_hbm,
                   (g0_hbm, g1_hbm, g2_hbm), stage_ref, last_sem, qs_ref, ys_hbm)


def _prompt_layer(x, kvb, gin, win, poolw, pscale, sgug, sguw, sgub_t, wdown, wout, gf,
                  xs_tiles, hwin, sk, sv):
    nb, seq, _ = x.shape
    ts = SEQ_TILE
    n_tiles = seq // ts
    ns = sk.shape[0]
    rb = ns // (nb * n_tiles)
    assert rb * nb * n_tiles == ns
    sample_row = jax.ShapeDtypeStruct((ns, BRANCH_W), F32)
    sample_tiles = jax.ShapeDtypeStruct((ns * ROW_TILES, LANES), F32)
    in_cols = win.shape[1]
    assert in_cols % W_STAGE_COLS == 0 and D_MODEL % W_STAGE_COLS == 0 and D_MODEL % W_STAGE_ROWS == 0
    step = lambda b, i: b * n_tiles + i
    hbm = lambda: pl.BlockSpec(memory_space=pl.ANY)
    return pl.pallas_call(
        _prompt_kernel,
        grid=(nb, n_tiles),
        in_specs=[pl.BlockSpec((1, ts, D_MODEL), lambda b, i: (b, i, 0)),
                  pl.BlockSpec((1, N_MEM, 2 * BRANCH_W), lambda b, i: (b, 0, 0)),
                  _resident((1, D_MODEL)),
                  hbm(),
                  _resident((POOL_GROUPS, POOL_GW, POOL_GW)),
                  _resident((1, BRANCH_W)),
                  _resident((1, BRANCH_W)),
                  _resident((SGU_GROUPS, SGU_CHUNK, SGU_CHUNK)),
                  _resident((SGU_CHUNK, SGU_GROUPS)),
                  hbm(),
                  hbm(),
                  _resident((1, D_MODEL)),
                  _resident((ns * ROW_TILES, LANES)),
                  _resident((ns, BRANCH_W)),
                  pl.BlockSpec((rb, N_MEM, KV_ROWS, LANES), lambda b, i: (step(b, i), 0, 0, 0)),
                  pl.BlockSpec((rb, N_MEM, KV_ROWS, LANES), lambda b, i: (step(b, i), 0, 0, 0))],
        out_specs=[pl.BlockSpec((1, ts, D_MODEL), lambda b, i: (b, i, 0)),
                   pl.BlockSpec((1, HIST_ROWS, BRANCH_W), lambda b, i: (b, 0, 0))] + [hbm()] * 9,
        out_shape=[jax.ShapeDtypeStruct((nb, seq, D_MODEL), F32),
                   jax.ShapeDtypeStruct((nb, HIST_ROWS, BRANCH_W), F32),
                   sample_row, sample_tiles] + [sample_row] * 6 + [sample_tiles],
        scratch_shapes=[pltpu.VMEM((HIST_ROWS + ts, BRANCH_W), F32),
                        pltpu.VMEM((D_MODEL, in_cols), BF16),
                        pltpu.VMEM((N_BRANCH, BRANCH_W, D_MODEL), BF16),
                        pltpu.VMEM((D_MODEL, D_MODEL), BF16),
                        pltpu.VMEM((W_STAGE_SLOTS, W_STAGE_ROWS, W_STAGE_COLS), F32),
                        pltpu.SemaphoreType.DMA((W_STAGE_SLOTS,)),
                        pltpu.VMEM((ns * KV_ROWS, LANES), F32),
                        pltpu.VMEM((ns * KV_ROWS, LANES), F32),
                        pltpu.VMEM((2, ns, BRANCH_W), F32),
                        pltpu.SemaphoreType.DMA((2,)),
                        pltpu.SemaphoreType.DMA((2 * N_BRANCH,))],
        compiler_params=pltpu.CompilerParams(dimension_semantics=("arbitrary", "arbitrary"),
                                             vmem_limit_bytes=V7X_VMEM_LIMIT_BYTES),
        name="prompt_layer",
    )(x, kvb, gin, win, poolw, pscale, sgug, sguw, sgub_t, wdown, wout, gf, xs_tiles, hwin, sk, sv)


def _prep_kernel(hist_hbm, mem_ref, gmem_ref, wkv_ref,
                 newhist_hbm, hwin_ref, k_ref, v_ref, kvb_ref, wkv_s, hist_s, hist_sem):
    c = pl.program_id(0)
    load = pltpu.make_async_copy(hist_hbm, hist_s, hist_sem.at[0])
    roll = pltpu.make_async_copy(hist_s.at[pl.ds(1, POOL_HIST - 1)], newhist_hbm.at[pl.ds(0, POOL_HIST - 1)],
                                 hist_sem.at[1])
    fill = pltpu.make_async_copy(hist_s.at[0], newhist_hbm.at[POOL_HIST - 1], hist_sem.at[2])

    @pl.when(c == 0)
    def _():
        load.start()
        wkv_s[...] = wkv_ref[...].astype(BF16)

    @pl.when(c == 1)
    def _():
        load.wait()
        for g, w in enumerate(POOL_WINDOWS):
            sl = slice(g * POOL_GW, (g + 1) * POOL_GW)
            win = hist_s[POOL_HIST - 1, :, sl]
            for j in range(2, w):
                win = win + hist_s[POOL_HIST - j, :, sl]
            hwin_ref[:, sl] = win
        roll.start()
        hist_s[0] = jnp.zeros(hist_s.shape[1:], F32)
        fill.start()

    _mem_kv_body(mem_ref, gmem_ref, wkv_s, k_ref, v_ref, kvb_ref)

    @pl.when(c == pl.num_programs(0) - 1)
    def _():
        roll.wait()
        fill.wait()


def _prep(hist, mem, gmem, wkv):
    n = hist.shape[1]
    nb = mem.shape[0]
    rq = MEM_REQ_BLOCK
    assert nb % rq == 0 and nb // rq >= 3
    rows_blk = lambda: pl.BlockSpec((rq, N_MEM * KV_ROWS, LANES), lambda b: (b, 0, 0))
    flat_blk = lambda: pl.BlockSpec((rq, N_MEM, 2 * BRANCH_W), lambda b: (b, 0, 0))
    hbm = lambda: pl.BlockSpec(memory_space=pl.ANY)
    return pl.pallas_call(
        _prep_kernel,
        grid=(nb // rq,),
        in_specs=[hbm(),
                  pl.BlockSpec((rq, N_MEM, D_MODEL), lambda b: (b, 0, 0)),
                  _resident((1, D_MODEL)),
                  _resident((D_MODEL, 2 * BRANCH_W))],
        out_specs=[hbm(), _resident((n, BRANCH_W)),
                   rows_blk(), rows_blk(), flat_blk()],
        out_shape=[jax.ShapeDtypeStruct(hist.shape, F32),
                   jax.ShapeDtypeStruct((n, BRANCH_W), F32),
                   jax.ShapeDtypeStruct((nb, N_MEM * KV_ROWS, LANES), F32),
                   jax.ShapeDtypeStruct((nb, N_MEM * KV_ROWS, LANES), F32),
                   jax.ShapeDtypeStruct((nb, N_MEM, 2 * BRANCH_W), BF16)],
        scratch_shapes=[pltpu.VMEM((D_MODEL, 2 * BRANCH_W), BF16),
                        pltpu.VMEM(hist.shape, F32),
                        pltpu.SemaphoreType.DMA((3,))],
        compiler_params=pltpu.CompilerParams(dimension_semantics=("arbitrary",),
                                             vmem_limit_bytes=V7X_VMEM_LIMIT_BYTES),
        name="prep",
    )(hist, mem, gmem, wkv)


def kernel(x_prompt, x_sample, state_pool, cache_mem_k, cache_mem_v, mem_prompt, norm_in_g, w_in,
           pool_w, pool_scale, sgu_norm_g, sgu_w, sgu_b, mem_norm_g, w_kv, w_down, w_out, norm_f_g):
    depth = w_in.shape[0]
    assert depth == 1, "single-layer step"
    nb, seq, _ = x_prompt.shape
    ns, dec_seq, _ = x_sample.shape
    assert dec_seq == 1 and seq % SEQ_TILE == 0 and seq >= HIST_ROWS

    row = lambda a: a.reshape(1, -1)
    gin, pscale, sgug, gmem, gf = (row(norm_in_g[0]), row(pool_scale[0]), row(sgu_norm_g[0]),
                                   row(mem_norm_g[0]), row(norm_f_g))
    win, poolw, wkv, wdown, wout, sguw, sgub = (w_in[0], pool_w[0], w_kv[0], w_down[0], w_out[0],
                                                sgu_w[0], sgu_b[0])
    sgub_t = sgub.T

    x_tiles = x_sample.reshape(ns * ROW_TILES, LANES)
    hist_s = jnp.transpose(state_pool[0], (1, 0, 2))
    new_hist_s, hwin_s, k_rows, v_rows, kvb = _prep(hist_s, mem_prompt, gmem, wkv)

    y_prompt, hist_p, p_in_s, vn_s, _, _, _, _, _, _, y_tiles = _prompt_layer(
        x_prompt, kvb, gin, win, poolw, pscale, sgug, sguw, sgub_t, wdown, wout, gf, x_tiles, hwin_s,
        _to_kv_rows(cache_mem_k[0].reshape(ns, N_MEM, BRANCH_W)),
        _to_kv_rows(cache_mem_v[0].reshape(ns, N_MEM, BRANCH_W)))

    new_pool_p = hist_p[None, :, HIST_ROWS - POOL_HIST:, :]
    new_hist_s = lax.dynamic_update_slice(new_hist_s, p_in_s[None], (POOL_HIST - 1, 0, 0))
    new_pool_s = jnp.transpose(new_hist_s, (1, 0, 2))[None]
    kv_out = lambda a: _from_kv_rows(a.reshape(nb, N_MEM, KV_ROWS, LANES))[None]
    return (y_prompt, y_tiles.reshape(ns, 1, D_MODEL), new_pool_p, new_pool_s,
            kv_out(k_rows), kv_out(v_rows), vn_s.reshape(1, ns, 1, BRANCH_W))
```

```python
import jax
import jax.numpy as jnp
from jax import lax
from jax.experimental import pallas as pl
from jax.experimental.pallas import tpu as pltpu

D_MODEL = 1024
BRANCH_W = 1024
N_BRANCH = 3
N_PROJ = 7 + N_BRANCH
POOL_WINDOWS = (2, 4, 8, 16)
POOL_GROUPS = len(POOL_WINDOWS)
POOL_GW = BRANCH_W // POOL_GROUPS
POOL_HIST = max(POOL_WINDOWS) - 1
HIST_ROWS = POOL_HIST + 1
SGU_CHUNK = 128
SGU_GROUPS = 4
SGU_GW = BRANCH_W // SGU_GROUPS
N_MEM = 256
XA_HEADS = 4
XA_HD = BRANCH_W // XA_HEADS
EPS = 1e-6
PAST_LEN = 16384

SEQ_TILE = 256
MEM_REQ_BLOCK = 2
SAMPLE_ATTN_CHUNK = 16
LANES = 128
XA_LANE_TILES = XA_HD // LANES
KV_ROWS = XA_HEADS * XA_LANE_TILES
LOG2E = 1.4426950408889634
W_STAGE_SLOTS = 8
W_STAGE_ROWS = 128
W_STAGE_COLS = 1024
V7X_VMEM_LIMIT_BYTES = 62 * 1024 * 1024

F32 = jnp.float32
BF16 = jnp.bfloat16

_sigmoid = jax.nn.sigmoid


def _rmsnorm(x, g):
    return x * lax.rsqrt(jnp.mean(x * x, axis=-1, keepdims=True) + EPS) * g


def _silu(z):
    return z * _sigmoid(z)


def _dot(a, b):
    return jnp.dot(a, b, preferred_element_type=F32)


def _resident(shape):
    zeros = (0,) * len(shape)
    return pl.BlockSpec(shape, lambda *_: zeros, pipeline_mode=pl.Buffered(1))


def _to_kv_rows(a):
    lead = a.shape[:-1]
    a = a.reshape(*lead, XA_HEADS, XA_LANE_TILES, LANES)
    return jnp.swapaxes(a, -3, -2).reshape(*lead, KV_ROWS, LANES)


ROW_TILES = D_MODEL // LANES
NATURAL_ORDER = tuple(range(ROW_TILES))
KV_ROW_ORDER = tuple((j % XA_HEADS) * XA_LANE_TILES + j // XA_HEADS for j in range(KV_ROWS))


def _load_row_tiles(ref, order=NATURAL_ORDER):
    n = ref.shape[0] // len(order)
    tiles = [None] * len(order)
    for j, t in enumerate(order):
        tiles[t] = ref[pl.ds(j, n, stride=len(order)), :]
    return jnp.concatenate(tiles, axis=1)


def _store_row_tiles(ref, val, order=NATURAL_ORDER):
    n = val.shape[0]
    for j, t in enumerate(order):
        ref[pl.ds(j, n, stride=len(order)), :] = val[:, t * LANES:(t + 1) * LANES]


def _from_kv_rows(a):
    lead = a.shape[:-2]
    a = a.reshape(*lead, XA_LANE_TILES, XA_HEADS, LANES)
    return jnp.swapaxes(a, -3, -2).reshape(*lead, XA_HEADS, XA_HD)


def _mem_kv_body(mem_ref, g_ref, wkv_s, k_ref, v_ref, kvb_ref):
    n_req = mem_ref.shape[0]
    mem = jnp.concatenate([mem_ref[r] for r in range(n_req)], axis=0)
    kv = _dot(_rmsnorm(mem, g_ref[...]).astype(BF16), wkv_s[...])
    assert N_MEM == XA_HD
    for r in range(n_req):
        k = kv[r * N_MEM:(r + 1) * N_MEM, :BRANCH_W]
        v = kv[r * N_MEM:(r + 1) * N_MEM, BRANCH_W:]
        k_t = [k[:, h * XA_HD:(h + 1) * XA_HD].T for h in range(XA_HEADS)]
        kvb_ref[r] = jnp.concatenate(k_t + [v], axis=1).astype(BF16)
        for h in range(XA_HEADS):
            for lt in range(XA_LANE_TILES):
                cols = slice(h * XA_HD + lt * LANES, h * XA_HD + (lt + 1) * LANES)
                rows = pl.ds(lt * XA_HEADS + h, N_MEM, stride=KV_ROWS)
                k_ref[r, rows, :] = k[:, cols]
                v_ref[r, rows, :] = v[:, cols]


def _stage_weights(copies, stage_ref, sem_ref, after_chunk):
    slots = stage_ref.shape[0]
    ahead = slots - 1

    def chunk_copy(k):
        return pltpu.make_async_copy(copies[k][0], stage_ref.at[k % slots], sem_ref.at[k % slots])

    for k in range(min(ahead, len(copies))):
        chunk_copy(k).start()
    for k, (_, dst) in enumerate(copies):
        if k + ahead < len(copies):
            chunk_copy(k + ahead).start()
        chunk_copy(k).wait()
        dst[...] = stage_ref[k % slots].astype(BF16)
        after_chunk(k)


class _HbmWriter:
    def __init__(self, stage_ref, sem_ref):
        self.stage_ref, self.sem_ref, self.pending, self.count = stage_ref, sem_ref, [], 0

    def write(self, value, dst_hbm):
        slots = self.stage_ref.shape[0]
        slot = self.count % slots
        if len(self.pending) == slots:
            self.pending.pop(0).wait()
        self.stage_ref[slot] = value
        copy = pltpu.make_async_copy(self.stage_ref.at[slot], dst_hbm, self.sem_ref.at[slot])
        copy.start()
        self.pending.append(copy)
        self.count += 1

    def finish(self):
        for copy in self.pending:
            copy.wait()
        self.pending = []


def _first_step(xs_ref, hwin_ref, gin_ref, poolw_ref, pscale_ref, sgug_ref, sguw_ref, sgub_ref,
                win_hbm, wdown_hbm, wout_hbm, win_ref, wdown_ref, wout_ref, stage_ref, stage_sem,
                qs_ref, tiles_ref, tiles_sem, ostage_ref, ostage_sem,
                pin_hbm, vn_hbm, opool_hbm, osgu_hbm, az_hbm, gate_hbms):
    _, sr, sc = stage_ref.shape
    assert sc == BRANCH_W and D_MODEL % sr == 0
    k_slabs = D_MODEL // sr
    tiles = lambda ref: [(pl.ds(r * sr, sr), pl.ds(c * sc, sc))
                         for c in range(ref.shape[-1] // sc) for r in range(ref.shape[-2] // sr)]
    copies = [(win_hbm.at[r, c], win_ref.at[r, c]) for r, c in tiles(win_ref)]
    n_win = len(copies)
    copies += [(wdown_hbm.at[n, r, c], wdown_ref.at[n, r, c]) for n in range(N_BRANCH) for r, c in tiles(wdown_ref)]
    copies += [(wout_hbm.at[r, c], wout_ref.at[r, c]) for r, c in tiles(wout_ref)]

    hbs = _rmsnorm(_load_row_tiles(xs_ref), gin_ref[...]).astype(BF16)
    writer = _HbmWriter(ostage_ref, ostage_sem)
    kept = {}

    def finish_chunk(c, val):
        if c == 0:
            kept["p_in"] = val
            writer.write(val, pin_hbm)
        elif c == 1:
            p_in = kept["p_in"]
            mixed = []
            for g, w in enumerate(POOL_WINDOWS):
                sl = slice(g * POOL_GW, (g + 1) * POOL_GW)
                d = (p_in[:, sl] + hwin_ref[:, sl]) / float(min(w, PAST_LEN + 1)) - p_in[:, sl]
                mixed.append(_dot(d.astype(BF16), poolw_ref[g].astype(BF16)))
            writer.write(jnp.concatenate(mixed, axis=1) * pscale_ref[...] * _silu(val), opool_hbm)
        elif c == 2:
            kept["u"] = val
        elif c == 3:
            vn = _rmsnorm(val, sgug_ref[...])
            _store_row_tiles(tiles_ref, vn)
            kept["vn_copy"] = pltpu.make_async_copy(tiles_ref, vn_hbm, tiles_sem.at[0])
            kept["vn_copy"].start()
            gated = [vn[:, g * SGU_GW:(g + 1) * SGU_GW] * sguw_ref[g, 0:1, 0:1] + sgub_ref[0:1, g:g + 1]
                     for g in range(SGU_GROUPS)]
            kept["u_gated"] = kept["u"] * jnp.concatenate(gated, axis=1)
        elif c == 4:
            writer.write(kept["u_gated"] * _silu(val), osgu_hbm)
        elif c == 5:
            _store_row_tiles(qs_ref, val, KV_ROW_ORDER)
        elif c == 6:
            writer.write(val, az_hbm)
        else:
            writer.write(val, gate_hbms[c - 7])

    def after_chunk(k):
        if k >= n_win:
            return
        c, r = divmod(k, k_slabs)
        part = _dot(hbs[:, r * sr:(r + 1) * sr], win_ref[r * sr:(r + 1) * sr, c * sc:(c + 1) * sc])
        kept["acc"] = part if r == 0 else kept["acc"] + part
        if r == k_slabs - 1:
            finish_chunk(c, kept["acc"])

    _stage_weights(copies, stage_ref, stage_sem, after_chunk)
    writer.finish()
    kept["vn_copy"].wait()


def _sample_branch_copies(srcs, stage_ref, last_sem):
    ns = srcs[0].shape[0]
    per_slot = stage_ref.shape[1] // ns
    bufs = [stage_ref.at[k // per_slot, pl.ds((k % per_slot) * ns, ns), :] for k in range(len(srcs))]
    return [pltpu.make_async_copy(src, buf, last_sem.at[k]) for k, (src, buf) in enumerate(zip(srcs, bufs))], bufs


def _last_step(xs_ref, attn_ref, gf_ref, wdown_ref, wout_ref, branch_hbms, stage_ref, last_sem, tiles_ref, ys_hbm):
    copies, bufs = _sample_branch_copies(branch_hbms, stage_ref, last_sem)
    for copy in copies:
        copy.wait()
    o_pool, o_sgu, a_z, *gates = [buf[...] for buf in bufs]
    o_xa = _load_row_tiles(attn_ref, KV_ROW_ORDER) * _silu(a_z)
    merged = None
    for n, o in enumerate((o_pool, o_sgu, o_xa)):
        t = _sigmoid(gates[n]) * _dot(o.astype(BF16), wdown_ref[n])
        merged = t if merged is None else merged + t
    xn = _load_row_tiles(xs_ref) + _dot(merged.astype(BF16), wout_ref[...])
    _store_row_tiles(tiles_ref, _rmsnorm(xn, gf_ref[...]))
    out = pltpu.make_async_copy(tiles_ref, ys_hbm, last_sem.at[0])
    out.start()
    out.wait()


def _sample_attention(get_q, k_ref, v_ref, put_o):
    for r in range(k_ref.shape[0]):
        q = get_q(r) * (XA_HD ** -0.5 * LOG2E)
        m = l = acc = None
        for c in range(N_MEM // SAMPLE_ATTN_CHUNK):
            rows = slice(c * SAMPLE_ATTN_CHUNK, (c + 1) * SAMPLE_ATTN_CHUNK)
            part = k_ref[r, rows] * q
            part = part + pltpu.roll(part, XA_HEADS, axis=1)
            s = jnp.sum(part, axis=-1, keepdims=True)
            m_chunk = jnp.max(s, axis=0, keepdims=True)
            m_new = m_chunk if m is None else jnp.maximum(m, m_chunk)
            e = jnp.exp2(s - m_new)
            l_chunk = jnp.sum(e, axis=0)
            acc_chunk = jnp.sum(e * v_ref[r, rows], axis=0)
            if m is None:
                l, acc = l_chunk, acc_chunk
            else:
                rescale = jnp.exp2(m - m_new)[0]
                l, acc = l * rescale + l_chunk, acc * rescale + acc_chunk
            m = m_new
        put_o(r, acc / l)


def _prompt_kernel(x_ref, kv_ref, gin_ref, win_hbm, poolw_ref, pscale_ref, sgug_ref,
                   sguw_ref, sgub_ref, wdown_hbm, wout_hbm, gf_ref, xs_ref, hwin_ref, sk_ref, sv_ref,
                   y_ref, hist_ref, pin_hbm, vn_hbm, opool_hbm, osgu_hbm, az_hbm, g0_hbm, g1_hbm, g2_hbm, ys_hbm,
                   ext_ref, win_ref, wdown_ref, wout_ref, stage_ref, stage_sem, qs_ref, attn_ref,
                   ostage_ref, ostage_sem, last_sem):
    i = pl.program_id(1)
    ts = x_ref.shape[1]
    step = pl.program_id(0) * pl.num_programs(1) + i

    @pl.when(step == 0)
    def _():
        _first_step(xs_ref, hwin_ref, gin_ref, poolw_ref, pscale_ref, sgug_ref, sguw_ref, sgub_ref,
                    win_hbm, wdown_hbm, wout_hbm, win_ref, wdown_ref, wout_ref, stage_ref, stage_sem,
                    qs_ref, attn_ref, last_sem, ostage_ref, ostage_sem,
                    pin_hbm, vn_hbm, opool_hbm, osgu_hbm, az_hbm, (g0_hbm, g1_hbm, g2_hbm))

    @pl.when(i == 0)
    def _():
        ext_ref[0:HIST_ROWS, :] = jnp.zeros((HIST_ROWS, BRANCH_W), F32)

    is_last = step == pl.num_programs(0) * pl.num_programs(1) - 1
    branch_hbms = (opool_hbm, osgu_hbm, az_hbm, g0_hbm, g1_hbm, g2_hbm)

    @pl.when(is_last)
    def _():
        for copy in _sample_branch_copies(branch_hbms, stage_ref, last_sem)[0]:
            copy.start()

    q_rows = sk_ref.shape[0] * KV_ROWS
    q_blk = qs_ref[pl.ds(pl.multiple_of(step * q_rows, q_rows), q_rows), :]

    def put_attention(r, out):
        attn_ref[pl.ds(pl.multiple_of(step * q_rows + r * KV_ROWS, KV_ROWS), KV_ROWS), :] = out

    _sample_attention(lambda r: q_blk[r * KV_ROWS:(r + 1) * KV_ROWS, :], sk_ref, sv_ref, put_attention)

    x = x_ref[0]
    hb = _rmsnorm(x, gin_ref[...]).astype(BF16)

    def proj(c):
        return _dot(hb, win_ref[:, c * BRANCH_W:(c + 1) * BRANCH_W])


    p_in = proj(0)
    p_z = proj(1)
    ext_ref[HIST_ROWS:HIST_ROWS + ts, :] = p_in
    pos = i * ts + lax.broadcasted_iota(jnp.int32, (ts, 1), 0)
    pooled = []
    for g, w in enumerate(POOL_WINDOWS):
        sl = slice(g * POOL_GW, (g + 1) * POOL_GW)
        win = ext_ref[:, sl]
        span = 1
        while span < w:
            win = win + pltpu.roll(win, span, axis=0)
            span *= 2
        inv_cnt = 1.0 / jnp.minimum(w, pos + 1).astype(F32)
        pooled.append((win[HIST_ROWS:, :] * inv_cnt - p_in[:, sl]).astype(BF16))
    ext_ref[0:HIST_ROWS, :] = p_in[ts - HIST_ROWS:, :]
    hist_ref[0] = p_in[ts - HIST_ROWS:, :]

    v = proj(3)
    u = proj(2)
    s_z = proj(4)
    mixed = [_dot(pooled[g], poolw_ref[g].astype(BF16)) for g in range(POOL_GROUPS)]
    o_pool = (jnp.concatenate(mixed, axis=1) * pscale_ref[...] * _silu(p_z)).astype(BF16)

    vnb = _rmsnorm(v, sgug_ref[...]).astype(BF16)
    tril = (lax.broadcasted_iota(jnp.int32, (SGU_CHUNK, SGU_CHUNK), 0)
            >= lax.broadcasted_iota(jnp.int32, (SGU_CHUNK, SGU_CHUNK), 1))
    ws = [jnp.where(tril, sguw_ref[g], 0.0).astype(BF16) for g in range(SGU_GROUPS)]
    q = proj(5)
    a_z = proj(6)
    rows = []
    for c in range(ts // SGU_CHUNK):
        rs = slice(c * SGU_CHUNK, (c + 1) * SGU_CHUNK)
        cols = [_dot(ws[g], vnb[rs, g * SGU_GW:(g + 1) * SGU_GW]) + sgub_ref[:, g:g + 1]
                for g in range(SGU_GROUPS)]
        rows.append(jnp.concatenate(cols, axis=1))
    o_sgu = (u * jnp.concatenate(rows, axis=0) * _silu(s_z)).astype(BF16)

    qb = q.astype(BF16)
    scores = [_dot(qb[:, hd * XA_HD:(hd + 1) * XA_HD], kv_ref[0, :, hd * N_MEM:(hd + 1) * N_MEM])
              * (XA_HD ** -0.5 * LOG2E) for hd in range(XA_HEADS)]
    gates = [proj(7 + n) for n in range(N_BRANCH)]
    heads = []
    for hd in range(XA_HEADS):
        s = scores[hd]
        e = jnp.exp2(s - jnp.max(s, axis=-1, keepdims=True))
        pr = e * (1.0 / jnp.sum(e, axis=-1, keepdims=True))
        heads.append(_dot(pr.astype(BF16), kv_ref[0, :, BRANCH_W + hd * XA_HD:BRANCH_W + (hd + 1) * XA_HD]))
    o_xa = (jnp.concatenate(heads, axis=1) * _silu(a_z)).astype(BF16)

    merged = None
    for n, o in enumerate((o_pool, o_sgu, o_xa)):
        t = _sigmoid(gates[n]) * _dot(o, wdown_ref[n])
        merged = t if merged is None else merged + t
    xn = x + _dot(merged.astype(BF16), wout_ref[...])
    y_ref[0] = _rmsnorm(xn, gf_ref[...])

    @pl.when(is_last)
    def _():
        _last_step(xs_ref, attn_ref, gf_ref, wdown_ref, wout_ref, branch_hbms, stage_ref, last_sem, qs_ref, ys_hbm)


def _prompt_layer(x, kvb, gin, win, poolw, pscale, sgug, sguw, sgub_t, wdown, wout, gf,
                  xs_tiles, hwin, sk, sv):
    nb, seq, _ = x.shape
    ts = SEQ_TILE
    n_tiles = seq // ts
    ns = sk.shape[0]
    rb = ns // (nb * n_tiles)
    assert rb * nb * n_tiles == ns
    assert nb * n_tiles > 1
    sample_row = jax.ShapeDtypeStruct((ns, BRANCH_W), F32)
    sample_tiles = jax.ShapeDtypeStruct((ns * ROW_TILES, LANES), F32)
    in_cols = win.shape[1]
    assert in_cols % W_STAGE_COLS == 0 and D_MODEL % W_STAGE_COLS == 0 and D_MODEL % W_STAGE_ROWS == 0
    step = lambda b, i: b * n_tiles + i
    hbm = lambda: pl.BlockSpec(memory_space=pl.ANY)
    return pl.pallas_call(
        _prompt_kernel,
        grid=(nb, n_tiles),
        in_specs=[pl.BlockSpec((1, ts, D_MODEL), lambda b, i: (b, i, 0)),
                  pl.BlockSpec((1, N_MEM, 2 * BRANCH_W), lambda b, i: (b, 0, 0)),
                  _resident((1, D_MODEL)),
                  hbm(),
                  _resident((POOL_GROUPS, POOL_GW, POOL_GW)),
                  _resident((1, BRANCH_W)),
                  _resident((1, BRANCH_W)),
                  _resident((SGU_GROUPS, SGU_CHUNK, SGU_CHUNK)),
                  _resident((SGU_CHUNK, SGU_GROUPS)),
                  hbm(),
                  hbm(),
                  _resident((1, D_MODEL)),
                  _resident((ns * ROW_TILES, LANES)),
                  _resident((ns, BRANCH_W)),
                  pl.BlockSpec((rb, N_MEM, KV_ROWS, LANES), lambda b, i: (step(b, i), 0, 0, 0)),
                  pl.BlockSpec((rb, N_MEM, KV_ROWS, LANES), lambda b, i: (step(b, i), 0, 0, 0))],
        out_specs=[pl.BlockSpec((1, ts, D_MODEL), lambda b, i: (b, i, 0)),
                   pl.BlockSpec((1, HIST_ROWS, BRANCH_W), lambda b, i: (b, 0, 0))] + [hbm()] * 9,
        out_shape=[jax.ShapeDtypeStruct((nb, seq, D_MODEL), F32),
                   jax.ShapeDtypeStruct((nb, HIST_ROWS, BRANCH_W), F32),
                   sample_row, sample_tiles] + [sample_row] * 6 + [sample_tiles],
        scratch_shapes=[pltpu.VMEM((HIST_ROWS + ts, BRANCH_W), F32),
                        pltpu.VMEM((D_MODEL, in_cols), BF16),
                        pltpu.VMEM((N_BRANCH, BRANCH_W, D_MODEL), BF16),
                        pltpu.VMEM((D_MODEL, D_MODEL), BF16),
                        pltpu.VMEM((W_STAGE_SLOTS, W_STAGE_ROWS, W_STAGE_COLS), F32),
                        pltpu.SemaphoreType.DMA((W_STAGE_SLOTS,)),
                        pltpu.VMEM((ns * KV_ROWS, LANES), F32),
                        pltpu.VMEM((ns * KV_ROWS, LANES), F32),
                        pltpu.VMEM((2, ns, BRANCH_W), F32),
                        pltpu.SemaphoreType.DMA((2,)),
                        pltpu.SemaphoreType.DMA((2 * N_BRANCH,))],
        compiler_params=pltpu.CompilerParams(dimension_semantics=("arbitrary", "arbitrary"),
                                             vmem_limit_bytes=V7X_VMEM_LIMIT_BYTES),
        name="prompt_layer",
    )(x, kvb, gin, win, poolw, pscale, sgug, sguw, sgub_t, wdown, wout, gf, xs_tiles, hwin, sk, sv)


def _prep_kernel(hist_hbm, mem_ref, gmem_ref, wkv_ref,
                 newhist_hbm, hwin_ref, k_ref, v_ref, kvb_ref, wkv_s, hist_s, hist_sem):
    c = pl.program_id(0)
    load = pltpu.make_async_copy(hist_hbm, hist_s, hist_sem.at[0])
    roll = pltpu.make_async_copy(hist_s.at[pl.ds(1, POOL_HIST - 1)], newhist_hbm.at[pl.ds(0, POOL_HIST - 1)],
                                 hist_sem.at[1])
    fill = pltpu.make_async_copy(hist_s.at[0], newhist_hbm.at[POOL_HIST - 1], hist_sem.at[2])

    @pl.when(c == 0)
    def _():
        load.start()
        wkv_s[...] = wkv_ref[...].astype(BF16)

    @pl.when(c == 1)
    def _():
        load.wait()
        for g, w in enumerate(POOL_WINDOWS):
            sl = slice(g * POOL_GW, (g + 1) * POOL_GW)
            win = hist_s[POOL_HIST - 1, :, sl]
            for j in range(2, w):
                win = win + hist_s[POOL_HIST - j, :, sl]
            hwin_ref[:, sl] = win
        roll.start()
        hist_s[0] = jnp.zeros(hist_s.shape[1:], F32)
        fill.start()

    _mem_kv_body(mem_ref, gmem_ref, wkv_s, k_ref, v_ref, kvb_ref)

    @pl.when(c == pl.num_programs(0) - 1)
    def _():
        roll.wait()
        fill.wait()


def _prep(hist, mem, gmem, wkv):
    n = hist.shape[1]
    nb = mem.shape[0]
    rq = MEM_REQ_BLOCK
    assert nb % rq == 0 and nb // rq >= 3
    rows_blk = lambda: pl.BlockSpec((rq, N_MEM * KV_ROWS, LANES), lambda b: (b, 0, 0))
    flat_blk = lambda: pl.BlockSpec((rq, N_MEM, 2 * BRANCH_W), lambda b: (b, 0, 0))
    hbm = lambda: pl.BlockSpec(memory_space=pl.ANY)
    return pl.pallas_call(
        _prep_kernel,
        grid=(nb // rq,),
        in_specs=[hbm(),
                  pl.BlockSpec((rq, N_MEM, D_MODEL), lambda b: (b, 0, 0)),
                  _resident((1, D_MODEL)),
                  _resident((D_MODEL, 2 * BRANCH_W))],
        out_specs=[hbm(), _resident((n, BRANCH_W)),
                   rows_blk(), rows_blk(), flat_blk()],
        out_shape=[jax.ShapeDtypeStruct(hist.shape, F32),
                   jax.ShapeDtypeStruct((n, BRANCH_W), F32),
                   jax.ShapeDtypeStruct((nb, N_MEM * KV_ROWS, LANES), F32),
                   jax.ShapeDtypeStruct((nb, N_MEM * KV_ROWS, LANES), F32),
                   jax.ShapeDtypeStruct((nb, N_MEM, 2 * BRANCH_W), BF16)],
        scratch_shapes=[pltpu.VMEM((D_MODEL, 2 * BRANCH_W), BF16),
                        pltpu.VMEM(hist.shape, F32),
                        pltpu.SemaphoreType.DMA((3,))],
        compiler_params=pltpu.CompilerParams(dimension_semantics=("arbitrary",),
                                             vmem_limit_bytes=V7X_VMEM_LIMIT_BYTES),
        name="prep",
    )(hist, mem, gmem, wkv)


def kernel(x_prompt, x_sample, state_pool, cache_mem_k, cache_mem_v, mem_prompt, norm_in_g, w_in,
           pool_w, pool_scale, sgu_norm_g, sgu_w, sgu_b, mem_norm_g, w_kv, w_down, w_out, norm_f_g):
    depth = w_in.shape[0]
    assert depth == 1, "single-layer step"
    nb, seq, _ = x_prompt.shape
    ns, dec_seq, _ = x_sample.shape
    assert dec_seq == 1 and seq % SEQ_TILE == 0 and seq >= HIST_ROWS

    row = lambda a: a.reshape(1, -1)
    gin, pscale, sgug, gmem, gf = (row(norm_in_g[0]), row(pool_scale[0]), row(sgu_norm_g[0]),
                                   row(mem_norm_g[0]), row(norm_f_g))
    win, poolw, wkv, wdown, wout, sguw, sgub = (w_in[0], pool_w[0], w_kv[0], w_down[0], w_out[0],
                                                sgu_w[0], sgu_b[0])
    sgub_t = sgub.T

    x_tiles = x_sample.reshape(ns * ROW_TILES, LANES)
    hist_s = jnp.transpose(state_pool[0], (1, 0, 2))
    new_hist_s, hwin_s, k_rows, v_rows, kvb = _prep(hist_s, mem_prompt, gmem, wkv)

    y_prompt, hist_p, p_in_s, vn_s, _, _, _, _, _, _, y_tiles = _prompt_layer(
        x_prompt, kvb, gin, win, poolw, pscale, sgug, sguw, sgub_t, wdown, wout, gf, x_tiles, hwin_s,
        _to_kv_rows(cache_mem_k[0].reshape(ns, N_MEM, BRANCH_W)),
        _to_kv_rows(cache_mem_v[0].reshape(ns, N_MEM, BRANCH_W)))

    new_pool_p = hist_p[None, :, HIST_ROWS - POOL_HIST:, :]
    new_hist_s = lax.dynamic_update_slice(new_hist_s, p_in_s[None], (POOL_HIST - 1, 0, 0))
    new_pool_s = jnp.transpose(new_hist_s, (1, 0, 2))[None]
    kv_out = lambda a: _from_kv_rows(a.reshape(nb, N_MEM, KV_ROWS, LANES))[None]
    return (y_prompt, y_tiles.reshape(ns, 1, D_MODEL), new_pool_p, new_pool_s,
            kv_out(k_rows), kv_out(v_rows), vn_s.reshape(1, ns, 1, BRANCH_W))
```

```python
import jax
import jax.numpy as jnp
from jax import lax
from jax.experimental import pallas as pl
from jax.experimental.pallas import tpu as pltpu

D_MODEL = 1024
BRANCH_W = 1024
N_BRANCH = 3
N_PROJ = 7 + N_BRANCH
POOL_WINDOWS = (2, 4, 8, 16)
POOL_GROUPS = len(POOL_WINDOWS)
POOL_GW = BRANCH_W // POOL_GROUPS
POOL_HIST = max(POOL_WINDOWS) - 1
HIST_ROWS = POOL_HIST + 1
SGU_CHUNK = 128
SGU_GROUPS = 4
SGU_GW = BRANCH_W // SGU_GROUPS
N_MEM = 256
XA_HEADS = 4
XA_HD = BRANCH_W // XA_HEADS
EPS = 1e-6
PAST_LEN = 16384

SEQ_TILE = 256
MEM_REQ_BLOCK = 2
SAMPLE_ATTN_CHUNK = 8
LANES = 128
XA_LANE_TILES = XA_HD // LANES
KV_ROWS = XA_HEADS * XA_LANE_TILES
LOG2E = 1.4426950408889634
W_STAGE_SLOTS = 8
W_STAGE_ROWS = 128
W_STAGE_COLS = 1024
V7X_VMEM_LIMIT_BYTES = 62 * 1024 * 1024

F32 = jnp.float32
BF16 = jnp.bfloat16

_sigmoid = jax.nn.sigmoid


def _rmsnorm(x, g):
    return x * lax.rsqrt(jnp.mean(x * x, axis=-1, keepdims=True) + EPS) * g


def _silu(z):
    return z * _sigmoid(z)


def _dot(a, b):
    return jnp.dot(a, b, preferred_element_type=F32)


def _resident(shape):
    zeros = (0,) * len(shape)
    return pl.BlockSpec(shape, lambda *_: zeros, pipeline_mode=pl.Buffered(1))


def _to_kv_rows(a):
    lead = a.shape[:-1]
    a = a.reshape(*lead, XA_HEADS, XA_LANE_TILES, LANES)
    return jnp.swapaxes(a, -3, -2).reshape(*lead, KV_ROWS, LANES)


ROW_TILES = D_MODEL // LANES
NATURAL_ORDER = tuple(range(ROW_TILES))
KV_ROW_ORDER = tuple((j % XA_HEADS) * XA_LANE_TILES + j // XA_HEADS for j in range(KV_ROWS))


def _load_row_tiles(ref, order=NATURAL_ORDER):
    n = ref.shape[0] // len(order)
    tiles = [None] * len(order)
    for j, t in enumerate(order):
        tiles[t] = ref[pl.ds(j, n, stride=len(order)), :]
    return jnp.concatenate(tiles, axis=1)


def _store_row_tiles(ref, val, order=NATURAL_ORDER):
    n = val.shape[0]
    for j, t in enumerate(order):
        ref[pl.ds(j, n, stride=len(order)), :] = val[:, t * LANES:(t + 1) * LANES]


def _from_kv_rows(a):
    lead = a.shape[:-2]
    a = a.reshape(*lead, XA_LANE_TILES, XA_HEADS, LANES)
    return jnp.swapaxes(a, -3, -2).reshape(*lead, XA_HEADS, XA_HD)


def _mem_kv_body(mem_ref, g_ref, wkv_s, k_ref, v_ref, kvb_ref):
    n_req = mem_ref.shape[0]
    mem = jnp.concatenate([mem_ref[r] for r in range(n_req)], axis=0)
    kv = _dot(_rmsnorm(mem, g_ref[...]).astype(BF16), wkv_s[...])
    assert N_MEM == XA_HD
    for r in range(n_req):
        k = kv[r * N_MEM:(r + 1) * N_MEM, :BRANCH_W]
        v = kv[r * N_MEM:(r + 1) * N_MEM, BRANCH_W:]
        k_t = [k[:, h * XA_HD:(h + 1) * XA_HD].T for h in range(XA_HEADS)]
        kvb_ref[r] = jnp.concatenate(k_t + [v], axis=1).astype(BF16)
        for h in range(XA_HEADS):
            for lt in range(XA_LANE_TILES):
                cols = slice(h * XA_HD + lt * LANES, h * XA_HD + (lt + 1) * LANES)
                rows = pl.ds(lt * XA_HEADS + h, N_MEM, stride=KV_ROWS)
                k_ref[r, rows, :] = k[:, cols]
                v_ref[r, rows, :] = v[:, cols]


def _stage_weights(copies, stage_ref, sem_ref, after_chunk):
    slots = stage_ref.shape[0]
    ahead = slots - 1

    def chunk_copy(k):
        return pltpu.make_async_copy(copies[k][0], stage_ref.at[k % slots], sem_ref.at[k % slots])

    for k in range(min(ahead, len(copies))):
        chunk_copy(k).start()
    for k, (_, dst) in enumerate(copies):
        if k + ahead < len(copies):
            chunk_copy(k + ahead).start()
        chunk_copy(k).wait()
        dst[...] = stage_ref[k % slots].astype(BF16)
        after_chunk(k)


class _HbmWriter:
    def __init__(self, stage_ref, sem_ref):
        self.stage_ref, self.sem_ref, self.pending, self.count = stage_ref, sem_ref, [], 0

    def write(self, value, dst_hbm):
        slots = self.stage_ref.shape[0]
        slot = self.count % slots
        if len(self.pending) == slots:
            self.pending.pop(0).wait()
        self.stage_ref[slot] = value
        copy = pltpu.make_async_copy(self.stage_ref.at[slot], dst_hbm, self.sem_ref.at[slot])
        copy.start()
        self.pending.append(copy)
        self.count += 1

    def finish(self):
        for copy in self.pending:
            copy.wait()
        self.pending = []


def _first_step(xs_ref, hwin_ref, gin_ref, poolw_ref, pscale_ref, sgug_ref, sguw_ref, sgub_ref,
                win_hbm, wdown_hbm, wout_hbm, win_ref, wdown_ref, wout_ref, stage_ref, stage_sem,
                qs_ref, tiles_ref, tiles_sem, ostage_ref, ostage_sem,
                pin_hbm, vn_hbm, opool_hbm, osgu_hbm, az_hbm, gate_hbms):
    _, sr, sc = stage_ref.shape
    assert sc == BRANCH_W and D_MODEL % sr == 0
    k_slabs = D_MODEL // sr
    tiles = lambda ref: [(pl.ds(r * sr, sr), pl.ds(c * sc, sc))
                         for c in range(ref.shape[-1] // sc) for r in range(ref.shape[-2] // sr)]
    copies = [(win_hbm.at[r, c], win_ref.at[r, c]) for r, c in tiles(win_ref)]
    n_win = len(copies)
    copies += [(wdown_hbm.at[n, r, c], wdown_ref.at[n, r, c]) for n in range(N_BRANCH) for r, c in tiles(wdown_ref)]
    copies += [(wout_hbm.at[r, c], wout_ref.at[r, c]) for r, c in tiles(wout_ref)]

    hbs = _rmsnorm(_load_row_tiles(xs_ref), gin_ref[...]).astype(BF16)
    writer = _HbmWriter(ostage_ref, ostage_sem)
    kept = {}

    def finish_chunk(c, val):
        if c == 0:
            kept["p_in"] = val
            writer.write(val, pin_hbm)
        elif c == 1:
            p_in = kept["p_in"]
            mixed = []
            for g, w in enumerate(POOL_WINDOWS):
                sl = slice(g * POOL_GW, (g + 1) * POOL_GW)
                d = (p_in[:, sl] + hwin_ref[:, sl]) / float(min(w, PAST_LEN + 1)) - p_in[:, sl]
                mixed.append(_dot(d.astype(BF16), poolw_ref[g].astype(BF16)))
            writer.write(jnp.concatenate(mixed, axis=1) * pscale_ref[...] * _silu(val), opool_hbm)
        elif c == 2:
            kept["u"] = val
        elif c == 3:
            vn = _rmsnorm(val, sgug_ref[...])
            _store_row_tiles(tiles_ref, vn)
            kept["vn_copy"] = pltpu.make_async_copy(tiles_ref, vn_hbm, tiles_sem.at[0])
            kept["vn_copy"].start()
            gated = [vn[:, g * SGU_GW:(g + 1) * SGU_GW] * sguw_ref[g, 0:1, 0:1] + sgub_ref[0:1, g:g + 1]
                     for g in range(SGU_GROUPS)]
            kept["u_gated"] = kept["u"] * jnp.concatenate(gated, axis=1)
        elif c == 4:
            writer.write(kept["u_gated"] * _silu(val), osgu_hbm)
        elif c == 5:
            _store_row_tiles(qs_ref, val, KV_ROW_ORDER)
        elif c == 6:
            writer.write(val, az_hbm)
        else:
            writer.write(val, gate_hbms[c - 7])

    def after_chunk(k):
        if k >= n_win:
            return
        c, r = divmod(k, k_slabs)
        part = _dot(hbs[:, r * sr:(r + 1) * sr], win_ref[r * sr:(r + 1) * sr, c * sc:(c + 1) * sc])
        kept["acc"] = part if r == 0 else kept["acc"] + part
        if r == k_slabs - 1:
            finish_chunk(c, kept["acc"])

    _stage_weights(copies, stage_ref, stage_sem, after_chunk)
    writer.finish()
    kept["vn_copy"].wait()


def _sample_branch_copies(srcs, stage_ref, last_sem):
    ns = srcs[0].shape[0]
    per_slot = stage_ref.shape[1] // ns
    bufs = [stage_ref.at[k // per_slot, pl.ds((k % per_slot) * ns, ns), :] for k in range(len(srcs))]
    return [pltpu.make_async_copy(src, buf, last_sem.at[k]) for k, (src, buf) in enumerate(zip(srcs, bufs))], bufs


def _last_step(xs_ref, attn_ref, gf_ref, wdown_ref, wout_ref, branch_hbms, stage_ref, last_sem, tiles_ref, ys_hbm):
    copies, bufs = _sample_branch_copies(branch_hbms, stage_ref, last_sem)
    for copy in copies:
        copy.wait()
    o_pool, o_sgu, a_z, *gates = [buf[...] for buf in bufs]
    o_xa = _load_row_tiles(attn_ref, KV_ROW_ORDER) * _silu(a_z)
    merged = None
    for n, o in enumerate((o_pool, o_sgu, o_xa)):
        t = _sigmoid(gates[n]) * _dot(o.astype(BF16), wdown_ref[n])
        merged = t if merged is None else merged + t
    xn = _load_row_tiles(xs_ref) + _dot(merged.astype(BF16), wout_ref[...])
    _store_row_tiles(tiles_ref, _rmsnorm(xn, gf_ref[...]))
    out = pltpu.make_async_copy(tiles_ref, ys_hbm, last_sem.at[0])
    out.start()
    out.wait()


def _sample_attention(get_q, k_ref, v_ref, put_o):
    for r in range(k_ref.shape[0]):
        q = get_q(r) * (XA_HD ** -0.5 * LOG2E)
        m = l = acc = None
        for c in range(N_MEM // SAMPLE_ATTN_CHUNK):
            rows = slice(c * SAMPLE_ATTN_CHUNK, (c + 1) * SAMPLE_ATTN_CHUNK)
            part = k_ref[r, rows] * q
            part = part + pltpu.roll(part, XA_HEADS, axis=1)
            s = jnp.sum(part, axis=-1, keepdims=True)
            m_chunk = jnp.max(s, axis=0, keepdims=True)
            m_new = m_chunk if m is None else jnp.maximum(m, m_chunk)
            e = jnp.exp2(s - m_new)
            l_chunk = jnp.sum(e, axis=0)
            acc_chunk = jnp.sum(e * v_ref[r, rows], axis=0)
            if m is None:
                l, acc = l_chunk, acc_chunk
            else:
                rescale = jnp.exp2(m - m_new)[0]
                l, acc = l * rescale + l_chunk, acc * rescale + acc_chunk
            m = m_new
        put_o(r, acc / l)


def _prompt_kernel(x_ref, kv_ref, gin_ref, win_hbm, poolw_ref, pscale_ref, sgug_ref,
                   sguw_ref, sgub_ref, wdown_hbm, wout_hbm, gf_ref, xs_ref, hwin_ref, sk_ref, sv_ref,
                   y_ref, hist_ref, pin_hbm, vn_hbm, opool_hbm, osgu_hbm, az_hbm, g0_hbm, g1_hbm, g2_hbm, ys_hbm,
                   ext_ref, win_ref, wdown_ref, wout_ref, stage_ref, stage_sem, qs_ref, attn_ref,
                   ostage_ref, ostage_sem, last_sem):
    i = pl.program_id(1)
    ts = x_ref.shape[1]
    step = pl.program_id(0) * pl.num_programs(1) + i

    @pl.when(step == 0)
    def _():
        _first_step(xs_ref, hwin_ref, gin_ref, poolw_ref, pscale_ref, sgug_ref, sguw_ref, sgub_ref,
                    win_hbm, wdown_hbm, wout_hbm, win_ref, wdown_ref, wout_ref, stage_ref, stage_sem,
                    qs_ref, attn_ref, last_sem, ostage_ref, ostage_sem,
                    pin_hbm, vn_hbm, opool_hbm, osgu_hbm, az_hbm, (g0_hbm, g1_hbm, g2_hbm))

    @pl.when(i == 0)
    def _():
        ext_ref[0:HIST_ROWS, :] = jnp.zeros((HIST_ROWS, BRANCH_W), F32)

    is_last = step == pl.num_programs(0) * pl.num_programs(1) - 1
    branch_hbms = (opool_hbm, osgu_hbm, az_hbm, g0_hbm, g1_hbm, g2_hbm)

    @pl.when(is_last)
    def _():
        for copy in _sample_branch_copies(branch_hbms, stage_ref, last_sem)[0]:
            copy.start()

    q_rows = sk_ref.shape[0] * KV_ROWS
    q_blk = qs_ref[pl.ds(pl.multiple_of(step * q_rows, q_rows), q_rows), :]

    def put_attention(r, out):
        attn_ref[pl.ds(pl.multiple_of(step * q_rows + r * KV_ROWS, KV_ROWS), KV_ROWS), :] = out

    _sample_attention(lambda r: q_blk[r * KV_ROWS:(r + 1) * KV_ROWS, :], sk_ref, sv_ref, put_attention)

    x = x_ref[0]
    hb = _rmsnorm(x, gin_ref[...]).astype(BF16)

    def proj(c):
        return _dot(hb, win_ref[:, c * BRANCH_W:(c + 1) * BRANCH_W])


    p_in = proj(0)
    p_z = proj(1)
    ext_ref[HIST_ROWS:HIST_ROWS + ts, :] = p_in
    pos = i * ts + lax.broadcasted_iota(jnp.int32, (ts, 1), 0)
    pooled = []
    for g, w in enumerate(POOL_WINDOWS):
        sl = slice(g * POOL_GW, (g + 1) * POOL_GW)
        win = ext_ref[:, sl]
        span = 1
        while span < w:
            win = win + pltpu.roll(win, span, axis=0)
            span *= 2
        inv_cnt = 1.0 / jnp.minimum(w, pos + 1).astype(F32)
        pooled.append((win[HIST_ROWS:, :] * inv_cnt - p_in[:, sl]).astype(BF16))
    ext_ref[0:HIST_ROWS, :] = p_in[ts - HIST_ROWS:, :]
    hist_ref[0] = p_in[ts - HIST_ROWS:, :]

    v = proj(3)
    u = proj(2)
    s_z = proj(4)
    mixed = [_dot(pooled[g], poolw_ref[g].astype(BF16)) for g in range(POOL_GROUPS)]
    o_pool = (jnp.concatenate(mixed, axis=1) * pscale_ref[...] * _silu(p_z)).astype(BF16)

    vnb = _rmsnorm(v, sgug_ref[...]).astype(BF16)
    tril = (lax.broadcasted_iota(jnp.int32, (SGU_CHUNK, SGU_CHUNK), 0)
            >= lax.broadcasted_iota(jnp.int32, (SGU_CHUNK, SGU_CHUNK), 1))
    ws = [jnp.where(tril, sguw_ref[g], 0.0).astype(BF16) for g in range(SGU_GROUPS)]
    q = proj(5)
    a_z = proj(6)
    rows = []
    for c in range(ts // SGU_CHUNK):
        rs = slice(c * SGU_CHUNK, (c + 1) * SGU_CHUNK)
        cols = [_dot(ws[g], vnb[rs, g * SGU_GW:(g + 1) * SGU_GW]) + sgub_ref[:, g:g + 1]
                for g in range(SGU_GROUPS)]
        rows.append(jnp.concatenate(cols, axis=1))
    o_sgu = (u * jnp.concatenate(rows, axis=0) * _silu(s_z)).astype(BF16)

    qb = q.astype(BF16)
    scores = [_dot(qb[:, hd * XA_HD:(hd + 1) * XA_HD], kv_ref[0, :, hd * N_MEM:(hd + 1) * N_MEM])
              * (XA_HD ** -0.5 * LOG2E) for hd in range(XA_HEADS)]
    gates = [proj(7 + n) for n in range(N_BRANCH)]
    heads = []
    for hd in range(XA_HEADS):
        s = scores[hd]
        e = jnp.exp2(s - jnp.max(s, axis=-1, keepdims=True))
        pr = e * (1.0 / jnp.sum(e, axis=-1, keepdims=True))
        heads.append(_dot(pr.astype(BF16), kv_ref[0, :, BRANCH_W + hd * XA_HD:BRANCH_W + (hd + 1) * XA_HD]))
    o_xa = (jnp.concatenate(heads, axis=1) * _silu(a_z)).astype(BF16)

    merged = None
    for n, o in enumerate((o_pool, o_sgu, o_xa)):
        t = _sigmoid(gates[n]) * _dot(o, wdown_ref[n])
        merged = t if merged is None else merged + t
    xn = x + _dot(merged.astype(BF16), wout_ref[...])
    y_ref[0] = _rmsnorm(xn, gf_ref[...])

    @pl.when(is_last)
    def _():
        _last_step(xs_ref, attn_ref, gf_ref, wdown_ref, wout_ref, branch_hbms, stage_ref, last_sem, qs_ref, ys_hbm)


def _prompt_layer(x, kvb, gin, win, poolw, pscale, sgug, sguw, sgub_t, wdown, wout, gf,
                  xs_tiles, hwin, sk, sv):
    nb, seq, _ = x.shape
    ts = SEQ_TILE
    n_tiles = seq // ts
    ns = sk.shape[0]
    rb = ns // (nb * n_tiles)
    assert rb * nb * n_tiles == ns
    assert nb * n_tiles > 1
    sample_row = jax.ShapeDtypeStruct((ns, BRANCH_W), F32)
    sample_tiles = jax.ShapeDtypeStruct((ns * ROW_TILES, LANES), F32)
    in_cols = win.shape[1]
    assert in_cols % W_STAGE_COLS == 0 and D_MODEL % W_STAGE_COLS == 0 and D_MODEL % W_STAGE_ROWS == 0
    step = lambda b, i: b * n_tiles + i
    hbm = lambda: pl.BlockSpec(memory_space=pl.ANY)
    return pl.pallas_call(
        _prompt_kernel,
        grid=(nb, n_tiles),
        in_specs=[pl.BlockSpec((1, ts, D_MODEL), lambda b, i: (b, i, 0)),
                  pl.BlockSpec((1, N_MEM, 2 * BRANCH_W), lambda b, i: (b, 0, 0)),
                  _resident((1, D_MODEL)),
                  hbm(),
                  _resident((POOL_GROUPS, POOL_GW, POOL_GW)),
                  _resident((1, BRANCH_W)),
                  _resident((1, BRANCH_W)),
                  _resident((SGU_GROUPS, SGU_CHUNK, SGU_CHUNK)),
                  _resident((SGU_CHUNK, SGU_GROUPS)),
                  hbm(),
                  hbm(),
                  _resident((1, D_MODEL)),
                  _resident((ns * ROW_TILES, LANES)),
                  _resident((ns, BRANCH_W)),
                  pl.BlockSpec((rb, N_MEM, KV_ROWS, LANES), lambda b, i: (step(b, i), 0, 0, 0)),
                  pl.BlockSpec((rb, N_MEM, KV_ROWS, LANES), lambda b, i: (step(b, i), 0, 0, 0))],
        out_specs=[pl.BlockSpec((1, ts, D_MODEL), lambda b, i: (b, i, 0)),
                   pl.BlockSpec((1, HIST_ROWS, BRANCH_W), lambda b, i: (b, 0, 0))] + [hbm()] * 9,
        out_shape=[jax.ShapeDtypeStruct((nb, seq, D_MODEL), F32),
                   jax.ShapeDtypeStruct((nb, HIST_ROWS, BRANCH_W), F32),
                   sample_row, sample_tiles] + [sample_row] * 6 + [sample_tiles],
        scratch_shapes=[pltpu.VMEM((HIST_ROWS + ts, BRANCH_W), F32),
                        pltpu.VMEM((D_MODEL, in_cols), BF16),
                        pltpu.VMEM((N_BRANCH, BRANCH_W, D_MODEL), BF16),
                        pltpu.VMEM((D_MODEL, D_MODEL), BF16),
                        pltpu.VMEM((W_STAGE_SLOTS, W_STAGE_ROWS, W_STAGE_COLS), F32),
                        pltpu.SemaphoreType.DMA((W_STAGE_SLOTS,)),
                        pltpu.VMEM((ns * KV_ROWS, LANES), F32),
                        pltpu.VMEM((ns * KV_ROWS, LANES), F32),
                        pltpu.VMEM((2, ns, BRANCH_W), F32),
                        pltpu.SemaphoreType.DMA((2,)),
                        pltpu.SemaphoreType.DMA((2 * N_BRANCH,))],
        compiler_params=pltpu.CompilerParams(dimension_semantics=("arbitrary", "arbitrary"),
                                             vmem_limit_bytes=V7X_VMEM_LIMIT_BYTES),
        name="prompt_layer",
    )(x, kvb, gin, win, poolw, pscale, sgug, sguw, sgub_t, wdown, wout, gf, xs_tiles, hwin, sk, sv)


def _prep_kernel(hist_hbm, mem_ref, gmem_ref, wkv_ref,
                 newhist_hbm, hwin_ref, k_ref, v_ref, kvb_ref, wkv_s, hist_s, hist_sem):
    c = pl.program_id(0)
    load = pltpu.make_async_copy(hist_hbm, hist_s, hist_sem.at[0])
    roll = pltpu.make_async_copy(hist_s.at[pl.ds(1, POOL_HIST - 1)], newhist_hbm.at[pl.ds(0, POOL_HIST - 1)],
                                 hist_sem.at[1])
    fill = pltpu.make_async_copy(hist_s.at[0], newhist_hbm.at[POOL_HIST - 1], hist_sem.at[2])

    @pl.when(c == 0)
    def _():
        load.start()
        wkv_s[...] = wkv_ref[...].astype(BF16)

    @pl.when(c == 1)
    def _():
        load.wait()
        for g, w in enumerate(POOL_WINDOWS):
            sl = slice(g * POOL_GW, (g + 1) * POOL_GW)
            win = hist_s[POOL_HIST - 1, :, sl]
            for j in range(2, w):
                win = win + hist_s[POOL_HIST - j, :, sl]
            hwin_ref[:, sl] = win
        roll.start()
        hist_s[0] = jnp.zeros(hist_s.shape[1:], F32)
        fill.start()

    _mem_kv_body(mem_ref, gmem_ref, wkv_s, k_ref, v_ref, kvb_ref)

    @pl.when(c == pl.num_programs(0) - 1)
    def _():
        roll.wait()
        fill.wait()


def _prep(hist, mem, gmem, wkv):
    n = hist.shape[1]
    nb = mem.shape[0]
    rq = MEM_REQ_BLOCK
    assert nb % rq == 0 and nb // rq >= 3
    rows_blk = lambda: pl.BlockSpec((rq, N_MEM * KV_ROWS, LANES), lambda b: (b, 0, 0))
    flat_blk = lambda: pl.BlockSpec((rq, N_MEM, 2 * BRANCH_W), lambda b: (b, 0, 0))
    hbm = lambda: pl.BlockSpec(memory_space=pl.ANY)
    return pl.pallas_call(
        _prep_kernel,
        grid=(nb // rq,),
        in_specs=[hbm(),
                  pl.BlockSpec((rq, N_MEM, D_MODEL), lambda b: (b, 0, 0)),
                  _resident((1, D_MODEL)),
                  _resident((D_MODEL, 2 * BRANCH_W))],
        out_specs=[hbm(), _resident((n, BRANCH_W)),
                   rows_blk(), rows_blk(), flat_blk()],
        out_shape=[jax.ShapeDtypeStruct(hist.shape, F32),
                   jax.ShapeDtypeStruct((n, BRANCH_W), F32),
                   jax.ShapeDtypeStruct((nb, N_MEM * KV_ROWS, LANES), F32),
                   jax.ShapeDtypeStruct((nb, N_MEM * KV_ROWS, LANES), F32),
                   jax.ShapeDtypeStruct((nb, N_MEM, 2 * BRANCH_W), BF16)],
        scratch_shapes=[pltpu.VMEM((D_MODEL, 2 * BRANCH_W), BF16),
                        pltpu.VMEM(hist.shape, F32),
                        pltpu.SemaphoreType.DMA((3,))],
        compiler_params=pltpu.CompilerParams(dimension_semantics=("arbitrary",),
                                             vmem_limit_bytes=V7X_VMEM_LIMIT_BYTES),
        name="prep",
    )(hist, mem, gmem, wkv)


def kernel(x_prompt, x_sample, state_pool, cache_mem_k, cache_mem_v, mem_prompt, norm_in_g, w_in,
           pool_w, pool_scale, sgu_norm_g, sgu_w, sgu_b, mem_norm_g, w_kv, w_down, w_out, norm_f_g):
    depth = w_in.shape[0]
    assert depth == 1, "single-layer step"
    nb, seq, _ = x_prompt.shape
    ns, dec_seq, _ = x_sample.shape
    assert dec_seq == 1 and seq % SEQ_TILE == 0 and seq >= HIST_ROWS

    row = lambda a: a.reshape(1, -1)
    gin, pscale, sgug, gmem, gf = (row(norm_in_g[0]), row(pool_scale[0]), row(sgu_norm_g[0]),
                                   row(mem_norm_g[0]), row(norm_f_g))
    win, poolw, wkv, wdown, wout, sguw, sgub = (w_in[0], pool_w[0], w_kv[0], w_down[0], w_out[0],
                                                sgu_w[0], sgu_b[0])
    sgub_t = sgub.T

    x_tiles = x_sample.reshape(ns * ROW_TILES, LANES)
    hist_s = jnp.transpose(state_pool[0], (1, 0, 2))
    new_hist_s, hwin_s, k_rows, v_rows, kvb = _prep(hist_s, mem_prompt, gmem, wkv)

    y_prompt, hist_p, p_in_s, vn_s, _, _, _, _, _, _, y_tiles = _prompt_layer(
        x_prompt, kvb, gin, win, poolw, pscale, sgug, sguw, sgub_t, wdown, wout, gf, x_tiles, hwin_s,
        _to_kv_rows(cache_mem_k[0].reshape(ns, N_MEM, BRANCH_W)),
        _to_kv_rows(cache_mem_v[0].reshape(ns, N_MEM, BRANCH_W)))

    new_pool_p = hist_p[None, :, HIST_ROWS - POOL_HIST:, :]
    new_hist_s = lax.dynamic_update_slice(new_hist_s, p_in_s[None], (POOL_HIST - 1, 0, 0))
    new_pool_s = jnp.transpose(new_hist_s, (1, 0, 2))[None]
    kv_out = lambda a: _from_kv_rows(a.reshape(nb, N_MEM, KV_ROWS, LANES))[None]
    return (y_prompt, y_tiles.reshape(ns, 1, D_MODEL), new_pool_p, new_pool_s,
            kv_out(k_rows), kv_out(v_rows), vn_s.reshape(1, ns, 1, BRANCH_W))
```

```python
import jax
import jax.numpy as jnp
from jax import lax
from jax.experimental import pallas as pl
from jax.experimental.pallas import tpu as pltpu

D_MODEL = 1024
BRANCH_W = 1024
N_BRANCH = 3
N_PROJ = 7 + N_BRANCH
POOL_WINDOWS = (2, 4, 8, 16)
POOL_GROUPS = len(POOL_WINDOWS)
POOL_GW = BRANCH_W // POOL_GROUPS
POOL_HIST = max(POOL_WINDOWS) - 1
HIST_ROWS = POOL_HIST + 1
SGU_CHUNK = 128
SGU_GROUPS = 4
SGU_GW = BRANCH_W // SGU_GROUPS
N_MEM = 256
XA_HEADS = 4
XA_HD = BRANCH_W // XA_HEADS
EPS = 1e-6
PAST_LEN = 16384

SEQ_TILE = 256
MEM_REQ_BLOCK = 2
SAMPLE_ATTN_CHUNK = 32
LANES = 128
XA_LANE_TILES = XA_HD // LANES
KV_ROWS = XA_HEADS * XA_LANE_TILES
LOG2E = 1.4426950408889634
W_STAGE_SLOTS = 8
W_STAGE_ROWS = 128
W_STAGE_COLS = 1024
V7X_VMEM_LIMIT_BYTES = 62 * 1024 * 1024

F32 = jnp.float32
BF16 = jnp.bfloat16

_sigmoid = jax.nn.sigmoid


def _rmsnorm(x, g):
    return x * lax.rsqrt(jnp.mean(x * x, axis=-1, keepdims=True) + EPS) * g


def _silu(z):
    return z * _sigmoid(z)


def _dot(a, b):
    return jnp.dot(a, b, preferred_element_type=F32)


def _resident(shape):
    zeros = (0,) * len(shape)
    return pl.BlockSpec(shape, lambda *_: zeros, pipeline_mode=pl.Buffered(1))


def _to_kv_rows(a):
    lead = a.shape[:-1]
    a = a.reshape(*lead, XA_HEADS, XA_LANE_TILES, LANES)
    return jnp.swapaxes(a, -3, -2).reshape(*lead, KV_ROWS, LANES)


ROW_TILES = D_MODEL // LANES
NATURAL_ORDER = tuple(range(ROW_TILES))
KV_ROW_ORDER = tuple((j % XA_HEADS) * XA_LANE_TILES + j // XA_HEADS for j in range(KV_ROWS))


def _load_row_tiles(ref, order=NATURAL_ORDER):
    n = ref.shape[0] // len(order)
    tiles = [None] * len(order)
    for j, t in enumerate(order):
        tiles[t] = ref[pl.ds(j, n, stride=len(order)), :]
    return jnp.concatenate(tiles, axis=1)


def _store_row_tiles(ref, val, order=NATURAL_ORDER):
    n = val.shape[0]
    for j, t in enumerate(order):
        ref[pl.ds(j, n, stride=len(order)), :] = val[:, t * LANES:(t + 1) * LANES]


def _from_kv_rows(a):
    lead = a.shape[:-2]
    a = a.reshape(*lead, XA_LANE_TILES, XA_HEADS, LANES)
    return jnp.swapaxes(a, -3, -2).reshape(*lead, XA_HEADS, XA_HD)


def _mem_kv_body(mem_ref, g_ref, wkv_s, k_ref, v_ref, kvb_ref):
    n_req = mem_ref.shape[0]
    mem = jnp.concatenate([mem_ref[r] for r in range(n_req)], axis=0)
    kv = _dot(_rmsnorm(mem, g_ref[...]).astype(BF16), wkv_s[...])
    assert N_MEM == XA_HD
    for r in range(n_req):
        k = kv[r * N_MEM:(r + 1) * N_MEM, :BRANCH_W]
        v = kv[r * N_MEM:(r + 1) * N_MEM, BRANCH_W:]
        k_t = [k[:, h * XA_HD:(h + 1) * XA_HD].T for h in range(XA_HEADS)]
        kvb_ref[r] = jnp.concatenate(k_t + [v], axis=1).astype(BF16)
        for h in range(XA_HEADS):
            for lt in range(XA_LANE_TILES):
                cols = slice(h * XA_HD + lt * LANES, h * XA_HD + (lt + 1) * LANES)
                rows = pl.ds(lt * XA_HEADS + h, N_MEM, stride=KV_ROWS)
                k_ref[r, rows, :] = k[:, cols]
                v_ref[r, rows, :] = v[:, cols]


def _stage_weights(copies, stage_ref, sem_ref, after_chunk):
    slots = stage_ref.shape[0]
    ahead = slots - 1

    def chunk_copy(k):
        return pltpu.make_async_copy(copies[k][0], stage_ref.at[k % slots], sem_ref.at[k % slots])

    for k in range(min(ahead, len(copies))):
        chunk_copy(k).start()
    for k, (_, dst) in enumerate(copies):
        if k + ahead < len(copies):
            chunk_copy(k + ahead).start()
        chunk_copy(k).wait()
        dst[...] = stage_ref[k % slots].astype(BF16)
        after_chunk(k)


class _HbmWriter:
    def __init__(self, stage_ref, sem_ref):
        self.stage_ref, self.sem_ref, self.pending, self.count = stage_ref, sem_ref, [], 0

    def write(self, value, dst_hbm):
        slots = self.stage_ref.shape[0]
        slot = self.count % slots
        if len(self.pending) == slots:
            self.pending.pop(0).wait()
        self.stage_ref[slot] = value
        copy = pltpu.make_async_copy(self.stage_ref.at[slot], dst_hbm, self.sem_ref.at[slot])
        copy.start()
        self.pending.append(copy)
        self.count += 1

    def finish(self):
        for copy in self.pending:
            copy.wait()
        self.pending = []


def _first_step(xs_ref, hwin_ref, gin_ref, poolw_ref, pscale_ref, sgug_ref, sguw_ref, sgub_ref,
                win_hbm, wdown_hbm, wout_hbm, win_ref, wdown_ref, wout_ref, stage_ref, stage_sem,
                qs_ref, tiles_ref, tiles_sem, ostage_ref, ostage_sem,
                pin_hbm, vn_hbm, opool_hbm, osgu_hbm, az_hbm, gate_hbms):
    _, sr, sc = stage_ref.shape
    assert sc == BRANCH_W and D_MODEL % sr == 0
    k_slabs = D_MODEL // sr
    tiles = lambda ref: [(pl.ds(r * sr, sr), pl.ds(c * sc, sc))
                         for c in range(ref.shape[-1] // sc) for r in range(ref.shape[-2] // sr)]
    copies = [(win_hbm.at[r, c], win_ref.at[r, c]) for r, c in tiles(win_ref)]
    n_win = len(copies)
    copies += [(wdown_hbm.at[n, r, c], wdown_ref.at[n, r, c]) for n in range(N_BRANCH) for r, c in tiles(wdown_ref)]
    copies += [(wout_hbm.at[r, c], wout_ref.at[r, c]) for r, c in tiles(wout_ref)]

    hbs = _rmsnorm(_load_row_tiles(xs_ref), gin_ref[...]).astype(BF16)
    writer = _HbmWriter(ostage_ref, ostage_sem)
    kept = {}

    def finish_chunk(c, val):
        if c == 0:
            kept["p_in"] = val
            writer.write(val, pin_hbm)
        elif c == 1:
            p_in = kept["p_in"]
            mixed = []
            for g, w in enumerate(POOL_WINDOWS):
                sl = slice(g * POOL_GW, (g + 1) * POOL_GW)
                d = (p_in[:, sl] + hwin_ref[:, sl]) / float(min(w, PAST_LEN + 1)) - p_in[:, sl]
                mixed.append(_dot(d.astype(BF16), poolw_ref[g].astype(BF16)))
            writer.write(jnp.concatenate(mixed, axis=1) * pscale_ref[...] * _silu(val), opool_hbm)
        elif c == 2:
            kept["u"] = val
        elif c == 3:
            vn = _rmsnorm(val, sgug_ref[...])
            _store_row_tiles(tiles_ref, vn)
            kept["vn_copy"] = pltpu.make_async_copy(tiles_ref, vn_hbm, tiles_sem.at[0])
            kept["vn_copy"].start()
            gated = [vn[:, g * SGU_GW:(g + 1) * SGU_GW] * sguw_ref[g, 0:1, 0:1] + sgub_ref[0:1, g:g + 1]
                     for g in range(SGU_GROUPS)]
            kept["u_gated"] = kept["u"] * jnp.concatenate(gated, axis=1)
        elif c == 4:
            writer.write(kept["u_gated"] * _silu(val), osgu_hbm)
        elif c == 5:
            _store_row_tiles(qs_ref, val, KV_ROW_ORDER)
        elif c == 6:
            writer.write(val, az_hbm)
        else:
            writer.write(val, gate_hbms[c - 7])

    def after_chunk(k):
        if k >= n_win:
            return
        c, r = divmod(k, k_slabs)
        part = _dot(hbs[:, r * sr:(r + 1) * sr], win_ref[r * sr:(r + 1) * sr, c * sc:(c + 1) * sc])
        kept["acc"] = part if r == 0 else kept["acc"] + part
        if r == k_slabs - 1:
            finish_chunk(c, kept["acc"])

    _stage_weights(copies, stage_ref, stage_sem, after_chunk)
    writer.finish()
    kept["vn_copy"].wait()


def _sample_branch_copies(srcs, stage_ref, last_sem):
    ns = srcs[0].shape[0]
    per_slot = stage_ref.shape[1] // ns
    bufs = [stage_ref.at[k // per_slot, pl.ds((k % per_slot) * ns, ns), :] for k in range(len(srcs))]
    return [pltpu.make_async_copy(src, buf, last_sem.at[k]) for k, (src, buf) in enumerate(zip(srcs, bufs))], bufs


def _last_step(xs_ref, attn_ref, gf_ref, wdown_ref, wout_ref, branch_hbms, stage_ref, last_sem, tiles_ref, ys_hbm):
    copies, bufs = _sample_branch_copies(branch_hbms, stage_ref, last_sem)
    for copy in copies:
        copy.wait()
    o_pool, o_sgu, a_z, *gates = [buf[...] for buf in bufs]
    o_xa = _load_row_tiles(attn_ref, KV_ROW_ORDER) * _silu(a_z)
    merged = None
    for n, o in enumerate((o_pool, o_sgu, o_xa)):
        t = _sigmoid(gates[n]) * _dot(o.astype(BF16), wdown_ref[n])
        merged = t if merged is None else merged + t
    xn = _load_row_tiles(xs_ref) + _dot(merged.astype(BF16), wout_ref[...])
    _store_row_tiles(tiles_ref, _rmsnorm(xn, gf_ref[...]))
    out = pltpu.make_async_copy(tiles_ref, ys_hbm, last_sem.at[0])
    out.start()
    out.wait()


def _sample_attention(get_q, k_ref, v_ref, put_o):
    for r in range(k_ref.shape[0]):
        q = get_q(r) * (XA_HD ** -0.5 * LOG2E)
        m = l = acc = None
        for c in range(N_MEM // SAMPLE_ATTN_CHUNK):
            rows = slice(c * SAMPLE_ATTN_CHUNK, (c + 1) * SAMPLE_ATTN_CHUNK)
            part = k_ref[r, rows] * q
            part = part + pltpu.roll(part, XA_HEADS, axis=1)
            s = jnp.sum(part, axis=-1, keepdims=True)
            m_chunk = jnp.max(s, axis=0, keepdims=True)
            m_new = m_chunk if m is None else jnp.maximum(m, m_chunk)
            e = jnp.exp2(s - m_new)
            l_chunk = jnp.sum(e, axis=0)
            acc_chunk = jnp.sum(e * v_ref[r, rows], axis=0)
            if m is None:
                l, acc = l_chunk, acc_chunk
            else:
                rescale = jnp.exp2(m - m_new)[0]
                l, acc = l * rescale + l_chunk, acc * rescale + acc_chunk
            m = m_new
        put_o(r, acc / l)


def _prompt_kernel(x_ref, kv_ref, gin_ref, win_hbm, poolw_ref, pscale_ref, sgug_ref,
                   sguw_ref, sgub_ref, wdown_hbm, wout_hbm, gf_ref, xs_ref, hwin_ref, sk_ref, sv_ref,
                   y_ref, hist_ref, pin_hbm, vn_hbm, opool_hbm, osgu_hbm, az_hbm, g0_hbm, g1_hbm, g2_hbm, ys_hbm,
                   ext_ref, win_ref, wdown_ref, wout_ref, stage_ref, stage_sem, qs_ref, attn_ref,
                   ostage_ref, ostage_sem, last_sem):
    i = pl.program_id(1)
    ts = x_ref.shape[1]
    step = pl.program_id(0) * pl.num_programs(1) + i

    @pl.when(step == 0)
    def _():
        _first_step(xs_ref, hwin_ref, gin_ref, poolw_ref, pscale_ref, sgug_ref, sguw_ref, sgub_ref,
                    win_hbm, wdown_hbm, wout_hbm, win_ref, wdown_ref, wout_ref, stage_ref, stage_sem,
                    qs_ref, attn_ref, last_sem, ostage_ref, ostage_sem,
                    pin_hbm, vn_hbm, opool_hbm, osgu_hbm, az_hbm, (g0_hbm, g1_hbm, g2_hbm))

    @pl.when(i == 0)
    def _():
        ext_ref[0:HIST_ROWS, :] = jnp.zeros((HIST_ROWS, BRANCH_W), F32)

    is_last = step == pl.num_programs(0) * pl.num_programs(1) - 1
    branch_hbms = (opool_hbm, osgu_hbm, az_hbm, g0_hbm, g1_hbm, g2_hbm)

    @pl.when(is_last)
    def _():
        for copy in _sample_branch_copies(branch_hbms, stage_ref, last_sem)[0]:
            copy.start()

    q_rows = sk_ref.shape[0] * KV_ROWS
    q_blk = qs_ref[pl.ds(pl.multiple_of(step * q_rows, q_rows), q_rows), :]

    def put_attention(r, out):
        attn_ref[pl.ds(pl.multiple_of(step * q_rows + r * KV_ROWS, KV_ROWS), KV_ROWS), :] = out

    _sample_attention(lambda r: q_blk[r * KV_ROWS:(r + 1) * KV_ROWS, :], sk_ref, sv_ref, put_attention)

    x = x_ref[0]
    hb = _rmsnorm(x, gin_ref[...]).astype(BF16)

    def proj(c):
        return _dot(hb, win_ref[:, c * BRANCH_W:(c + 1) * BRANCH_W])


    p_in = proj(0)
    p_z = proj(1)
    ext_ref[HIST_ROWS:HIST_ROWS + ts, :] = p_in
    pos = i * ts + lax.broadcasted_iota(jnp.int32, (ts, 1), 0)
    pooled = []
    for g, w in enumerate(POOL_WINDOWS):
        sl = slice(g * POOL_GW, (g + 1) * POOL_GW)
        win = ext_ref[:, sl]
        span = 1
        while span < w:
            win = win + pltpu.roll(win, span, axis=0)
            span *= 2
        inv_cnt = 1.0 / jnp.minimum(w, pos + 1).astype(F32)
        pooled.append((win[HIST_ROWS:, :] * inv_cnt - p_in[:, sl]).astype(BF16))
    ext_ref[0:HIST_ROWS, :] = p_in[ts - HIST_ROWS:, :]
    hist_ref[0] = p_in[ts - HIST_ROWS:, :]

    v = proj(3)
    u = proj(2)
    s_z = proj(4)
    mixed = [_dot(pooled[g], poolw_ref[g].astype(BF16)) for g in range(POOL_GROUPS)]
    o_pool = (jnp.concatenate(mixed, axis=1) * pscale_ref[...] * _silu(p_z)).astype(BF16)

    vnb = _rmsnorm(v, sgug_ref[...]).astype(BF16)
    tril = (lax.broadcasted_iota(jnp.int32, (SGU_CHUNK, SGU_CHUNK), 0)
            >= lax.broadcasted_iota(jnp.int32, (SGU_CHUNK, SGU_CHUNK), 1))
    ws = [jnp.where(tril, sguw_ref[g], 0.0).astype(BF16) for g in range(SGU_GROUPS)]
    q = proj(5)
    a_z = proj(6)
    rows = []
    for c in range(ts // SGU_CHUNK):
        rs = slice(c * SGU_CHUNK, (c + 1) * SGU_CHUNK)
        cols = [_dot(ws[g], vnb[rs, g * SGU_GW:(g + 1) * SGU_GW]) + sgub_ref[:, g:g + 1]
                for g in range(SGU_GROUPS)]
        rows.append(jnp.concatenate(cols, axis=1))
    o_sgu = (u * jnp.concatenate(rows, axis=0) * _silu(s_z)).astype(BF16)

    qb = q.astype(BF16)
    scores = [_dot(qb[:, hd * XA_HD:(hd + 1) * XA_HD], kv_ref[0, :, hd * N_MEM:(hd + 1) * N_MEM])
              * (XA_HD ** -0.5 * LOG2E) for hd in range(XA_HEADS)]
    gates = [proj(7 + n) for n in range(N_BRANCH)]
    heads = []
    for hd in range(XA_HEADS):
        s = scores[hd]
        e = jnp.exp2(s - jnp.max(s, axis=-1, keepdims=True))
        pr = e * (1.0 / jnp.sum(e, axis=-1, keepdims=True))
        heads.append(_dot(pr.astype(BF16), kv_ref[0, :, BRANCH_W + hd * XA_HD:BRANCH_W + (hd + 1) * XA_HD]))
    o_xa = (jnp.concatenate(heads, axis=1) * _silu(a_z)).astype(BF16)

    merged = None
    for n, o in enumerate((o_pool, o_sgu, o_xa)):
        t = _sigmoid(gates[n]) * _dot(o, wdown_ref[n])
        merged = t if merged is None else merged + t
    xn = x + _dot(merged.astype(BF16), wout_ref[...])
    y_ref[0] = _rmsnorm(xn, gf_ref[...])

    @pl.when(is_last)
    def _():
        _last_step(xs_ref, attn_ref, gf_ref, wdown_ref, wout_ref, branch_hbms, stage_ref, last_sem, qs_ref, ys_hbm)


def _prompt_layer(x, kvb, gin, win, poolw, pscale, sgug, sguw, sgub_t, wdown, wout, gf,
                  xs_tiles, hwin, sk, sv):
    nb, seq, _ = x.shape
    ts = SEQ_TILE
    n_tiles = seq // ts
    ns = sk.shape[0]
    rb = ns // (nb * n_tiles)
    assert rb * nb * n_tiles == ns
    assert nb * n_tiles > 1
    sample_row = jax.ShapeDtypeStruct((ns, BRANCH_W), F32)
    sample_tiles = jax.ShapeDtypeStruct((ns * ROW_TILES, LANES), F32)
    in_cols = win.shape[1]
    assert in_cols % W_STAGE_COLS == 0 and D_MODEL % W_STAGE_COLS == 0 and D_MODEL % W_STAGE_ROWS == 0
    step = lambda b, i: b * n_tiles + i
    hbm = lambda: pl.BlockSpec(memory_space=pl.ANY)
    return pl.pallas_call(
        _prompt_kernel,
        grid=(nb, n_tiles),
        in_specs=[pl.BlockSpec((1, ts, D_MODEL), lambda b, i: (b, i, 0)),
                  pl.BlockSpec((1, N_MEM, 2 * BRANCH_W), lambda b, i: (b, 0, 0)),
                  _resident((1, D_MODEL)),
                  hbm(),
                  _resident((POOL_GROUPS, POOL_GW, POOL_GW)),
                  _resident((1, BRANCH_W)),
                  _resident((1, BRANCH_W)),
                  _resident((SGU_GROUPS, SGU_CHUNK, SGU_CHUNK)),
                  _resident((SGU_CHUNK, SGU_GROUPS)),
                  hbm(),
                  hbm(),
                  _resident((1, D_MODEL)),
                  _resident((ns * ROW_TILES, LANES)),
                  _resident((ns, BRANCH_W)),
                  pl.BlockSpec((rb, N_MEM, KV_ROWS, LANES), lambda b, i: (step(b, i), 0, 0, 0)),
                  pl.BlockSpec((rb, N_MEM, KV_ROWS, LANES), lambda b, i: (step(b, i), 0, 0, 0))],
        out_specs=[pl.BlockSpec((1, ts, D_MODEL), lambda b, i: (b, i, 0)),
                   pl.BlockSpec((1, HIST_ROWS, BRANCH_W), lambda b, i: (b, 0, 0))] + [hbm()] * 9,
        out_shape=[jax.ShapeDtypeStruct((nb, seq, D_MODEL), F32),
                   jax.ShapeDtypeStruct((nb, HIST_ROWS, BRANCH_W), F32),
                   sample_row, sample_tiles] + [sample_row] * 6 + [sample_tiles],
        scratch_shapes=[pltpu.VMEM((HIST_ROWS + ts, BRANCH_W), F32),
                        pltpu.VMEM((D_MODEL, in_cols), BF16),
                        pltpu.VMEM((N_BRANCH, BRANCH_W, D_MODEL), BF16),
                        pltpu.VMEM((D_MODEL, D_MODEL), BF16),
                        pltpu.VMEM((W_STAGE_SLOTS, W_STAGE_ROWS, W_STAGE_COLS), F32),
                        pltpu.SemaphoreType.DMA((W_STAGE_SLOTS,)),
                        pltpu.VMEM((ns * KV_ROWS, LANES), F32),
                        pltpu.VMEM((ns * KV_ROWS, LANES), F32),
                        pltpu.VMEM((2, ns, BRANCH_W), F32),
                        pltpu.SemaphoreType.DMA((2,)),
                        pltpu.SemaphoreType.DMA((2 * N_BRANCH,))],
        compiler_params=pltpu.CompilerParams(dimension_semantics=("arbitrary", "arbitrary"),
                                             vmem_limit_bytes=V7X_VMEM_LIMIT_BYTES),
        name="prompt_layer",
    )(x, kvb, gin, win, poolw, pscale, sgug, sguw, sgub_t, wdown, wout, gf, xs_tiles, hwin, sk, sv)


def _prep_kernel(hist_hbm, mem_ref, gmem_ref, wkv_ref,
                 newhist_hbm, hwin_ref, k_ref, v_ref, kvb_ref, wkv_s, hist_s, hist_sem):
    c = pl.program_id(0)
    load = pltpu.make_async_copy(hist_hbm, hist_s, hist_sem.at[0])
    roll = pltpu.make_async_copy(hist_s.at[pl.ds(1, POOL_HIST - 1)], newhist_hbm.at[pl.ds(0, POOL_HIST - 1)],
                                 hist_sem.at[1])
    fill = pltpu.make_async_copy(hist_s.at[0], newhist_hbm.at[POOL_HIST - 1], hist_sem.at[2])

    @pl.when(c == 0)
    def _():
        load.start()
        wkv_s[...] = wkv_ref[...].astype(BF16)

    @pl.when(c == 1)
    def _():
        load.wait()
        for g, w in enumerate(POOL_WINDOWS):
            sl = slice(g * POOL_GW, (g + 1) * POOL_GW)
            win = hist_s[POOL_HIST - 1, :, sl]
            for j in range(2, w):
                win = win + hist_s[POOL_HIST - j, :, sl]
            hwin_ref[:, sl] = win
        roll.start()
        hist_s[0] = jnp.zeros(hist_s.shape[1:], F32)
        fill.start()

    _mem_kv_body(mem_ref, gmem_ref, wkv_s, k_ref, v_ref, kvb_ref)

    @pl.when(c == pl.num_programs(0) - 1)
    def _():
        roll.wait()
        fill.wait()


def _prep(hist, mem, gmem, wkv):
    n = hist.shape[1]
    nb = mem.shape[0]
    rq = MEM_REQ_BLOCK
    assert nb % rq == 0 and nb // rq >= 3
    rows_blk = lambda: pl.BlockSpec((rq, N_MEM * KV_ROWS, LANES), lambda b: (b, 0, 0))
    flat_blk = lambda: pl.BlockSpec((rq, N_MEM, 2 * BRANCH_W), lambda b: (b, 0, 0))
    hbm = lambda: pl.BlockSpec(memory_space=pl.ANY)
    return pl.pallas_call(
        _prep_kernel,
        grid=(nb // rq,),
        in_specs=[hbm(),
                  pl.BlockSpec((rq, N_MEM, D_MODEL), lambda b: (b, 0, 0)),
                  _resident((1, D_MODEL)),
                  _resident((D_MODEL, 2 * BRANCH_W))],
        out_specs=[hbm(), _resident((n, BRANCH_W)),
                   rows_blk(), rows_blk(), flat_blk()],
        out_shape=[jax.ShapeDtypeStruct(hist.shape, F32),
                   jax.ShapeDtypeStruct((n, BRANCH_W), F32),
                   jax.ShapeDtypeStruct((nb, N_MEM * KV_ROWS, LANES), F32),
                   jax.ShapeDtypeStruct((nb, N_MEM * KV_ROWS, LANES), F32),
                   jax.ShapeDtypeStruct((nb, N_MEM, 2 * BRANCH_W), BF16)],
        scratch_shapes=[pltpu.VMEM((D_MODEL, 2 * BRANCH_W), BF16),
                        pltpu.VMEM(hist.shape, F32),
                        pltpu.SemaphoreType.DMA((3,))],
        compiler_params=pltpu.CompilerParams(dimension_semantics=("arbitrary",),
                                             vmem_limit_bytes=V7X_VMEM_LIMIT_BYTES),
        name="prep",
    )(hist, mem, gmem, wkv)


def kernel(x_prompt, x_sample, state_pool, cache_mem_k, cache_mem_v, mem_prompt, norm_in_g, w_in,
           pool_w, pool_scale, sgu_norm_g, sgu_w, sgu_b, mem_norm_g, w_kv, w_down, w_out, norm_f_g):
    depth = w_in.shape[0]
    assert depth == 1, "single-layer step"
    nb, seq, _ = x_prompt.shape
    ns, dec_seq, _ = x_sample.shape
    assert dec_seq == 1 and seq % SEQ_TILE == 0 and seq >= HIST_ROWS

    row = lambda a: a.reshape(1, -1)
    gin, pscale, sgug, gmem, gf = (row(norm_in_g[0]), row(pool_scale[0]), row(sgu_norm_g[0]),
                                   row(mem_norm_g[0]), row(norm_f_g))
    win, poolw, wkv, wdown, wout, sguw, sgub = (w_in[0], pool_w[0], w_kv[0], w_down[0], w_out[0],
                                                sgu_w[0], sgu_b[0])
    sgub_t = sgub.T

    x_tiles = x_sample.reshape(ns * ROW_TILES, LANES)
    hist_s = jnp.transpose(state_pool[0], (1, 0, 2))
    new_hist_s, hwin_s, k_rows, v_rows, kvb = _prep(hist_s, mem_prompt, gmem, wkv)

    y_prompt, hist_p, p_in_s, vn_s, _, _, _, _, _, _, y_tiles = _prompt_layer(
        x_prompt, kvb, gin, win, poolw, pscale, sgug, sguw, sgub_t, wdown, wout, gf, x_tiles, hwin_s,
        _to_kv_rows(cache_mem_k[0].reshape(ns, N_MEM, BRANCH_W)),
        _to_kv_rows(cache_mem_v[0].reshape(ns, N_MEM, BRANCH_W)))

    new_pool_p = hist_p[None, :, HIST_ROWS - POOL_HIST:, :]
    new_hist_s = lax.dynamic_update_slice(new_hist_s, p_in_s[None], (POOL_HIST - 1, 0, 0))
    new_pool_s = jnp.transpose(new_hist_s, (1, 0, 2))[None]
    kv_out = lambda a: _from_kv_rows(a.reshape(nb, N_MEM, KV_ROWS, LANES))[None]
    return (y_prompt, y_tiles.reshape(ns, 1, D_MODEL), new_pool_p, new_pool_s,
            kv_out(k_rows), kv_out(v_rows), vn_s.reshape(1, ns, 1, BRANCH_W))
```

```python
import jax
import jax.numpy as jnp
from jax import lax
from jax.experimental import pallas as pl
from jax.experimental.pallas import tpu as pltpu

D_MODEL = 1024
BRANCH_W = 1024
N_BRANCH = 3
N_PROJ = 7 + N_BRANCH
POOL_WINDOWS = (2, 4, 8, 16)
POOL_GROUPS = len(POOL_WINDOWS)
POOL_GW = BRANCH_W // POOL_GROUPS
POOL_HIST = max(POOL_WINDOWS) - 1
HIST_ROWS = POOL_HIST + 1
SGU_CHUNK = 128
SGU_GROUPS = 4
SGU_GW = BRANCH_W // SGU_GROUPS
N_MEM = 256
XA_HEADS = 4
XA_HD = BRANCH_W // XA_HEADS
EPS = 1e-6
PAST_LEN = 16384

SEQ_TILE = 256
MEM_REQ_BLOCK = 2
SAMPLE_ATTN_CHUNK = 64
LANES = 128
XA_LANE_TILES = XA_HD // LANES
KV_ROWS = XA_HEADS * XA_LANE_TILES
LOG2E = 1.4426950408889634
W_STAGE_SLOTS = 8
W_STAGE_ROWS = 128
W_STAGE_COLS = 1024
V7X_VMEM_LIMIT_BYTES = 62 * 1024 * 1024

F32 = jnp.float32
BF16 = jnp.bfloat16

_sigmoid = jax.nn.sigmoid


def _rmsnorm(x, g):
    return x * lax.rsqrt(jnp.mean(x * x, axis=-1, keepdims=True) + EPS) * g


def _silu(z):
    return z * _sigmoid(z)


def _dot(a, b):
    return jnp.dot(a, b, preferred_element_type=F32)


def _resident(shape):
    zeros = (0,) * len(shape)
    return pl.BlockSpec(shape, lambda *_: zeros, pipeline_mode=pl.Buffered(1))


def _to_kv_rows(a):
    lead = a.shape[:-1]
    a = a.reshape(*lead, XA_HEADS, XA_LANE_TILES, LANES)
    return jnp.swapaxes(a, -3, -2).reshape(*lead, KV_ROWS, LANES)


ROW_TILES = D_MODEL // LANES
NATURAL_ORDER = tuple(range(ROW_TILES))
KV_ROW_ORDER = tuple((j % XA_HEADS) * XA_LANE_TILES + j // XA_HEADS for j in range(KV_ROWS))


def _load_row_tiles(ref, order=NATURAL_ORDER):
    n = ref.shape[0] // len(order)
    tiles = [None] * len(order)
    for j, t in enumerate(order):
        tiles[t] = ref[pl.ds(j, n, stride=len(order)), :]
    return jnp.concatenate(tiles, axis=1)


def _store_row_tiles(ref, val, order=NATURAL_ORDER):
    n = val.shape[0]
    for j, t in enumerate(order):
        ref[pl.ds(j, n, stride=len(order)), :] = val[:, t * LANES:(t + 1) * LANES]


def _from_kv_rows(a):
    lead = a.shape[:-2]
    a = a.reshape(*lead, XA_LANE_TILES, XA_HEADS, LANES)
    return jnp.swapaxes(a, -3, -2).reshape(*lead, XA_HEADS, XA_HD)


def _mem_kv_body(mem_ref, g_ref, wkv_s, k_ref, v_ref, kvb_ref):
    n_req = mem_ref.shape[0]
    mem = jnp.concatenate([mem_ref[r] for r in range(n_req)], axis=0)
    kv = _dot(_rmsnorm(mem, g_ref[...]).astype(BF16), wkv_s[...])
    assert N_MEM == XA_HD
    for r in range(n_req):
        k = kv[r * N_MEM:(r + 1) * N_MEM, :BRANCH_W]
        v = kv[r * N_MEM:(r + 1) * N_MEM, BRANCH_W:]
        k_t = [k[:, h * XA_HD:(h + 1) * XA_HD].T for h in range(XA_HEADS)]
        kvb_ref[r] = jnp.concatenate(k_t + [v], axis=1).astype(BF16)
        for h in range(XA_HEADS):
            for lt in range(XA_LANE_TILES):
                cols = slice(h * XA_HD + lt * LANES, h * XA_HD + (lt + 1) * LANES)
                rows = pl.ds(lt * XA_HEADS + h, N_MEM, stride=KV_ROWS)
                k_ref[r, rows, :] = k[:, cols]
                v_ref[r, rows, :] = v[:, cols]


def _stage_weights(copies, stage_ref, sem_ref, after_chunk):
    slots = stage_ref.shape[0]
    ahead = slots - 1

    def chunk_copy(k):
        return pltpu.make_async_copy(copies[k][0], stage_ref.at[k % slots], sem_ref.at[k % slots])

    for k in range(min(ahead, len(copies))):
        chunk_copy(k).start()
    for k, (_, dst) in enumerate(copies):
        if k + ahead < len(copies):
            chunk_copy(k + ahead).start()
        chunk_copy(k).wait()
        dst[...] = stage_ref[k % slots].astype(BF16)
        after_chunk(k)


class _HbmWriter:
    def __init__(self, stage_ref, sem_ref):
        self.stage_ref, self.sem_ref, self.pending, self.count = stage_ref, sem_ref, [], 0

    def write(self, value, dst_hbm):
        slots = self.stage_ref.shape[0]
        slot = self.count % slots
        if len(self.pending) == slots:
            self.pending.pop(0).wait()
        self.stage_ref[slot] = value
        copy = pltpu.make_async_copy(self.stage_ref.at[slot], dst_hbm, self.sem_ref.at[slot])
        copy.start()
        self.pending.append(copy)
        self.count += 1

    def finish(self):
        for copy in self.pending:
            copy.wait()
        self.pending = []


def _first_step(xs_ref, hwin_ref, gin_ref, poolw_ref, pscale_ref, sgug_ref, sguw_ref, sgub_ref,
                win_hbm, wdown_hbm, wout_hbm, win_ref, wdown_ref, wout_ref, stage_ref, stage_sem,
                qs_ref, tiles_ref, tiles_sem, ostage_ref, ostage_sem,
                pin_hbm, vn_hbm, opool_hbm, osgu_hbm, az_hbm, gate_hbms):
    _, sr, sc = stage_ref.shape
    assert sc == BRANCH_W and D_MODEL % sr == 0
    k_slabs = D_MODEL // sr
    tiles = lambda ref: [(pl.ds(r * sr, sr), pl.ds(c * sc, sc))
                         for c in range(ref.shape[-1] // sc) for r in range(ref.shape[-2] // sr)]
    copies = [(win_hbm.at[r, c], win_ref.at[r, c]) for r, c in tiles(win_ref)]
    n_win = len(copies)
    copies += [(wdown_hbm.at[n, r, c], wdown_ref.at[n, r, c]) for n in range(N_BRANCH) for r, c in tiles(wdown_ref)]
    copies += [(wout_hbm.at[r, c], wout_ref.at[r, c]) for r, c in tiles(wout_ref)]

    hbs = _rmsnorm(_load_row_tiles(xs_ref), gin_ref[...]).astype(BF16)
    writer = _HbmWriter(ostage_ref, ostage_sem)
    kept = {}

    def finish_chunk(c, val):
        if c == 0:
            kept["p_in"] = val
            writer.write(val, pin_hbm)
        elif c == 1:
            p_in = kept["p_in"]
            mixed = []
            for g, w in enumerate(POOL_WINDOWS):
                sl = slice(g * POOL_GW, (g + 1) * POOL_GW)
                d = (p_in[:, sl] + hwin_ref[:, sl]) / float(min(w, PAST_LEN + 1)) - p_in[:, sl]
                mixed.append(_dot(d.astype(BF16), poolw_ref[g].astype(BF16)))
            writer.write(jnp.concatenate(mixed, axis=1) * pscale_ref[...] * _silu(val), opool_hbm)
        elif c == 2:
            kept["u"] = val
        elif c == 3:
            vn = _rmsnorm(val, sgug_ref[...])
            _store_row_tiles(tiles_ref, vn)
            kept["vn_copy"] = pltpu.make_async_copy(tiles_ref, vn_hbm, tiles_sem.at[0])
            kept["vn_copy"].start()
            gated = [vn[:, g * SGU_GW:(g + 1) * SGU_GW] * sguw_ref[g, 0:1, 0:1] + sgub_ref[0:1, g:g + 1]
                     for g in range(SGU_GROUPS)]
            kept["u_gated"] = kept["u"] * jnp.concatenate(gated, axis=1)
        elif c == 4:
            writer.write(kept["u_gated"] * _silu(val), osgu_hbm)
        elif c == 5:
            _store_row_tiles(qs_ref, val, KV_ROW_ORDER)
        elif c == 6:
            writer.write(val, az_hbm)
        else:
            writer.write(val, gate_hbms[c - 7])

    def after_chunk(k):
        if k >= n_win:
            return
        c, r = divmod(k, k_slabs)
        part = _dot(hbs[:, r * sr:(r + 1) * sr], win_ref[r * sr:(r + 1) * sr, c * sc:(c + 1) * sc])
        kept["acc"] = part if r == 0 else kept["acc"] + part
        if r == k_slabs - 1:
            finish_chunk(c, kept["acc"])

    _stage_weights(copies, stage_ref, stage_sem, after_chunk)
    writer.finish()
    kept["vn_copy"].wait()


def _sample_branch_copies(srcs, stage_ref, last_sem):
    ns = srcs[0].shape[0]
    per_slot = stage_ref.shape[1] // ns
    bufs = [stage_ref.at[k // per_slot, pl.ds((k % per_slot) * ns, ns), :] for k in range(len(srcs))]
    return [pltpu.make_async_copy(src, buf, last_sem.at[k]) for k, (src, buf) in enumerate(zip(srcs, bufs))], bufs


def _last_step(xs_ref, attn_ref, gf_ref, wdown_ref, wout_ref, branch_hbms, stage_ref, last_sem, tiles_ref, ys_hbm):
    copies, bufs = _sample_branch_copies(branch_hbms, stage_ref, last_sem)
    for copy in copies:
        copy.wait()
    o_pool, o_sgu, a_z, *gates = [buf[...] for buf in bufs]
    o_xa = _load_row_tiles(attn_ref, KV_ROW_ORDER) * _silu(a_z)
    merged = None
    for n, o in enumerate((o_pool, o_sgu, o_xa)):
        t = _sigmoid(gates[n]) * _dot(o.astype(BF16), wdown_ref[n])
        merged = t if merged is None else merged + t
    xn = _load_row_tiles(xs_ref) + _dot(merged.astype(BF16), wout_ref[...])
    _store_row_tiles(tiles_ref, _rmsnorm(xn, gf_ref[...]))
    out = pltpu.make_async_copy(tiles_ref, ys_hbm, last_sem.at[0])
    out.start()
    out.wait()


def _sample_attention(get_q, k_ref, v_ref, put_o):
    for r in range(k_ref.shape[0]):
        q = get_q(r) * (XA_HD ** -0.5 * LOG2E)
        m = l = acc = None
        for c in range(N_MEM // SAMPLE_ATTN_CHUNK):
            rows = slice(c * SAMPLE_ATTN_CHUNK, (c + 1) * SAMPLE_ATTN_CHUNK)
            part = k_ref[r, rows] * q
            part = part + pltpu.roll(part, XA_HEADS, axis=1)
            s = jnp.sum(part, axis=-1, keepdims=True)
            m_chunk = jnp.max(s, axis=0, keepdims=True)
            m_new = m_chunk if m is None else jnp.maximum(m, m_chunk)
            e = jnp.exp2(s - m_new)
            l_chunk = jnp.sum(e, axis=0)
            acc_chunk = jnp.sum(e * v_ref[r, rows], axis=0)
            if m is None:
                l, acc = l_chunk, acc_chunk
            else:
                rescale = jnp.exp2(m - m_new)[0]
                l, acc = l * rescale + l_chunk, acc * rescale + acc_chunk
            m = m_new
        put_o(r, acc / l)


def _prompt_kernel(x_ref, kv_ref, gin_ref, win_hbm, poolw_ref, pscale_ref, sgug_ref,
                   sguw_ref, sgub_ref, wdown_hbm, wout_hbm, gf_ref, xs_ref, hwin_ref, sk_ref, sv_ref,
                   y_ref, hist_ref, pin_hbm, vn_hbm, opool_hbm, osgu_hbm, az_hbm, g0_hbm, g1_hbm, g2_hbm, ys_hbm,
                   ext_ref, win_ref, wdown_ref, wout_ref, stage_ref, stage_sem, qs_ref, attn_ref,
                   ostage_ref, ostage_sem, last_sem):
    i = pl.program_id(1)
    ts = x_ref.shape[1]
    step = pl.program_id(0) * pl.num_programs(1) + i

    @pl.when(step == 0)
    def _():
        _first_step(xs_ref, hwin_ref, gin_ref, poolw_ref, pscale_ref, sgug_ref, sguw_ref, sgub_ref,
                    win_hbm, wdown_hbm, wout_hbm, win_ref, wdown_ref, wout_ref, stage_ref, stage_sem,
                    qs_ref, attn_ref, last_sem, ostage_ref, ostage_sem,
                    pin_hbm, vn_hbm, opool_hbm, osgu_hbm, az_hbm, (g0_hbm, g1_hbm, g2_hbm))

    @pl.when(i == 0)
    def _():
        ext_ref[0:HIST_ROWS, :] = jnp.zeros((HIST_ROWS, BRANCH_W), F32)

    is_last = step == pl.num_programs(0) * pl.num_programs(1) - 1
    branch_hbms = (opool_hbm, osgu_hbm, az_hbm, g0_hbm, g1_hbm, g2_hbm)

    @pl.when(is_last)
    def _():
        for copy in _sample_branch_copies(branch_hbms, stage_ref, last_sem)[0]:
            copy.start()

    q_rows = sk_ref.shape[0] * KV_ROWS
    q_blk = qs_ref[pl.ds(pl.multiple_of(step * q_rows, q_rows), q_rows), :]

    def put_attention(r, out):
        attn_ref[pl.ds(pl.multiple_of(step * q_rows + r * KV_ROWS, KV_ROWS), KV_ROWS), :] = out

    _sample_attention(lambda r: q_blk[r * KV_ROWS:(r + 1) * KV_ROWS, :], sk_ref, sv_ref, put_attention)

    x = x_ref[0]
    hb = _rmsnorm(x, gin_ref[...]).astype(BF16)

    def proj(c):
        return _dot(hb, win_ref[:, c * BRANCH_W:(c + 1) * BRANCH_W])


    p_in = proj(0)
    p_z = proj(1)
    ext_ref[HIST_ROWS:HIST_ROWS + ts, :] = p_in
    pos = i * ts + lax.broadcasted_iota(jnp.int32, (ts, 1), 0)
    pooled = []
    for g, w in enumerate(POOL_WINDOWS):
        sl = slice(g * POOL_GW, (g + 1) * POOL_GW)
        win = ext_ref[:, sl]
        span = 1
        while span < w:
            win = win + pltpu.roll(win, span, axis=0)
            span *= 2
        inv_cnt = 1.0 / jnp.minimum(w, pos + 1).astype(F32)
        pooled.append((win[HIST_ROWS:, :] * inv_cnt - p_in[:, sl]).astype(BF16))
    ext_ref[0:HIST_ROWS, :] = p_in[ts - HIST_ROWS:, :]
    hist_ref[0] = p_in[ts - HIST_ROWS:, :]

    v = proj(3)
    u = proj(2)
    s_z = proj(4)
    mixed = [_dot(pooled[g], poolw_ref[g].astype(BF16)) for g in range(POOL_GROUPS)]
    o_pool = (jnp.concatenate(mixed, axis=1) * pscale_ref[...] * _silu(p_z)).astype(BF16)

    vnb = _rmsnorm(v, sgug_ref[...]).astype(BF16)
    tril = (lax.broadcasted_iota(jnp.int32, (SGU_CHUNK, SGU_CHUNK), 0)
            >= lax.broadcasted_iota(jnp.int32, (SGU_CHUNK, SGU_CHUNK), 1))
    ws = [jnp.where(tril, sguw_ref[g], 0.0).astype(BF16) for g in range(SGU_GROUPS)]
    q = proj(5)
    a_z = proj(6)
    rows = []
    for c in range(ts // SGU_CHUNK):
        rs = slice(c * SGU_CHUNK, (c + 1) * SGU_CHUNK)
        cols = [_dot(ws[g], vnb[rs, g * SGU_GW:(g + 1) * SGU_GW]) + sgub_ref[:, g:g + 1]
                for g in range(SGU_GROUPS)]
        rows.append(jnp.concatenate(cols, axis=1))
    o_sgu = (u * jnp.concatenate(rows, axis=0) * _silu(s_z)).astype(BF16)

    qb = q.astype(BF16)
    scores = [_dot(qb[:, hd * XA_HD:(hd + 1) * XA_HD], kv_ref[0, :, hd * N_MEM:(hd + 1) * N_MEM])
              * (XA_HD ** -0.5 * LOG2E) for hd in range(XA_HEADS)]
    gates = [proj(7 + n) for n in range(N_BRANCH)]
    heads = []
    for hd in range(XA_HEADS):
        s = scores[hd]
        e = jnp.exp2(s - jnp.max(s, axis=-1, keepdims=True))
        pr = e * (1.0 / jnp.sum(e, axis=-1, keepdims=True))
        heads.append(_dot(pr.astype(BF16), kv_ref[0, :, BRANCH_W + hd * XA_HD:BRANCH_W + (hd + 1) * XA_HD]))
    o_xa = (jnp.concatenate(heads, axis=1) * _silu(a_z)).astype(BF16)

    merged = None
    for n, o in enumerate((o_pool, o_sgu, o_xa)):
        t = _sigmoid(gates[n]) * _dot(o, wdown_ref[n])
        merged = t if merged is None else merged + t
    xn = x + _dot(merged.astype(BF16), wout_ref[...])
    y_ref[0] = _rmsnorm(xn, gf_ref[...])

    @pl.when(is_last)
    def _():
        _last_step(xs_ref, attn_ref, gf_ref, wdown_ref, wout_ref, branch_hbms, stage_ref, last_sem, qs_ref, ys_hbm)


def _prompt_layer(x, kvb, gin, win, poolw, pscale, sgug, sguw, sgub_t, wdown, wout, gf,
                  xs_tiles, hwin, sk, sv):
    nb, seq, _ = x.shape
    ts = SEQ_TILE
    n_tiles = seq // ts
    ns = sk.shape[0]
    rb = ns // (nb * n_tiles)
    assert rb * nb * n_tiles == ns
    assert nb * n_tiles > 1
    sample_row = jax.ShapeDtypeStruct((ns, BRANCH_W), F32)
    sample_tiles = jax.ShapeDtypeStruct((ns * ROW_TILES, LANES), F32)
    in_cols = win.shape[1]
    assert in_cols % W_STAGE_COLS == 0 and D_MODEL % W_STAGE_COLS == 0 and D_MODEL % W_STAGE_ROWS == 0
    step = lambda b, i: b * n_tiles + i
    hbm = lambda: pl.BlockSpec(memory_space=pl.ANY)
    return pl.pallas_call(
        _prompt_kernel,
        grid=(nb, n_tiles),
        in_specs=[pl.BlockSpec((1, ts, D_MODEL), lambda b, i: (b, i, 0)),
                  pl.BlockSpec((1, N_MEM, 2 * BRANCH_W), lambda b, i: (b, 0, 0)),
                  _resident((1, D_MODEL)),
                  hbm(),
                  _resident((POOL_GROUPS, POOL_GW, POOL_GW)),
                  _resident((1, BRANCH_W)),
                  _resident((1, BRANCH_W)),
                  _resident((SGU_GROUPS, SGU_CHUNK, SGU_CHUNK)),
                  _resident((SGU_CHUNK, SGU_GROUPS)),
                  hbm(),
                  hbm(),
                  _resident((1, D_MODEL)),
                  _resident((ns * ROW_TILES, LANES)),
                  _resident((ns, BRANCH_W)),
                  pl.BlockSpec((rb, N_MEM, KV_ROWS, LANES), lambda b, i: (step(b, i), 0, 0, 0)),
                  pl.BlockSpec((rb, N_MEM, KV_ROWS, LANES), lambda b, i: (step(b, i), 0, 0, 0))],
        out_specs=[pl.BlockSpec((1, ts, D_MODEL), lambda b, i: (b, i, 0)),
                   pl.BlockSpec((1, HIST_ROWS, BRANCH_W), lambda b, i: (b, 0, 0))] + [hbm()] * 9,
        out_shape=[jax.ShapeDtypeStruct((nb, seq, D_MODEL), F32),
                   jax.ShapeDtypeStruct((nb, HIST_ROWS, BRANCH_W), F32),
                   sample_row, sample_tiles] + [sample_row] * 6 + [sample_tiles],
        scratch_shapes=[pltpu.VMEM((HIST_ROWS + ts, BRANCH_W), F32),
                        pltpu.VMEM((D_MODEL, in_cols), BF16),
                        pltpu.VMEM((N_BRANCH, BRANCH_W, D_MODEL), BF16),
                        pltpu.VMEM((D_MODEL, D_MODEL), BF16),
                        pltpu.VMEM((W_STAGE_SLOTS, W_STAGE_ROWS, W_STAGE_COLS), F32),
                        pltpu.SemaphoreType.DMA((W_STAGE_SLOTS,)),
                        pltpu.VMEM((ns * KV_ROWS, LANES), F32),
                        pltpu.VMEM((ns * KV_ROWS, LANES), F32),
                        pltpu.VMEM((2, ns, BRANCH_W), F32),
                        pltpu.SemaphoreType.DMA((2,)),
                        pltpu.SemaphoreType.DMA((2 * N_BRANCH,))],
        compiler_params=pltpu.CompilerParams(dimension_semantics=("arbitrary", "arbitrary"),
                                             vmem_limit_bytes=V7X_VMEM_LIMIT_BYTES),
        name="prompt_layer",
    )(x, kvb, gin, win, poolw, pscale, sgug, sguw, sgub_t, wdown, wout, gf, xs_tiles, hwin, sk, sv)


def _prep_kernel(hist_hbm, mem_ref, gmem_ref, wkv_ref,
                 newhist_hbm, hwin_ref, k_ref, v_ref, kvb_ref, wkv_s, hist_s, hist_sem):
    c = pl.program_id(0)
    load = pltpu.make_async_copy(hist_hbm, hist_s, hist_sem.at[0])
    roll = pltpu.make_async_copy(hist_s.at[pl.ds(1, POOL_HIST - 1)], newhist_hbm.at[pl.ds(0, POOL_HIST - 1)],
                                 hist_sem.at[1])
    fill = pltpu.make_async_copy(hist_s.at[0], newhist_hbm.at[POOL_HIST - 1], hist_sem.at[2])

    @pl.when(c == 0)
    def _():
        load.start()
        wkv_s[...] = wkv_ref[...].astype(BF16)

    @pl.when(c == 1)
    def _():
        load.wait()
        for g, w in enumerate(POOL_WINDOWS):
            sl = slice(g * POOL_GW, (g + 1) * POOL_GW)
            win = hist_s[POOL_HIST - 1, :, sl]
            for j in range(2, w):
                win = win + hist_s[POOL_HIST - j, :, sl]
            hwin_ref[:, sl] = win
        roll.start()
        hist_s[0] = jnp.zeros(hist_s.shape[1:], F32)
        fill.start()

    _mem_kv_body(mem_ref, gmem_ref, wkv_s, k_ref, v_ref, kvb_ref)

    @pl.when(c == pl.num_programs(0) - 1)
    def _():
        roll.wait()
        fill.wait()


def _prep(hist, mem, gmem, wkv):
    n = hist.shape[1]
    nb = mem.shape[0]
    rq = MEM_REQ_BLOCK
    assert nb % rq == 0 and nb // rq >= 3
    rows_blk = lambda: pl.BlockSpec((rq, N_MEM * KV_ROWS, LANES), lambda b: (b, 0, 0))
    flat_blk = lambda: pl.BlockSpec((rq, N_MEM, 2 * BRANCH_W), lambda b: (b, 0, 0))
    hbm = lambda: pl.BlockSpec(memory_space=pl.ANY)
    return pl.pallas_call(
        _prep_kernel,
        grid=(nb // rq,),
        in_specs=[hbm(),
                  pl.BlockSpec((rq, N_MEM, D_MODEL), lambda b: (b, 0, 0)),
                  _resident((1, D_MODEL)),
                  _resident((D_MODEL, 2 * BRANCH_W))],
        out_specs=[hbm(), _resident((n, BRANCH_W)),
                   rows_blk(), rows_blk(), flat_blk()],
        out_shape=[jax.ShapeDtypeStruct(hist.shape, F32),
                   jax.ShapeDtypeStruct((n, BRANCH_W), F32),
                   jax.ShapeDtypeStruct((nb, N_MEM * KV_ROWS, LANES), F32),
                   jax.ShapeDtypeStruct((nb, N_MEM * KV_ROWS, LANES), F32),
                   jax.ShapeDtypeStruct((nb, N_MEM, 2 * BRANCH_W), BF16)],
        scratch_shapes=[pltpu.VMEM((D_MODEL, 2 * BRANCH_W), BF16),
                        pltpu.VMEM(hist.shape, F32),
                        pltpu.SemaphoreType.DMA((3,))],
        compiler_params=pltpu.CompilerParams(dimension_semantics=("arbitrary",),
                                             vmem_limit_bytes=V7X_VMEM_LIMIT_BYTES),
        name="prep",
    )(hist, mem, gmem, wkv)


def kernel(x_prompt, x_sample, state_pool, cache_mem_k, cache_mem_v, mem_prompt, norm_in_g, w_in,
           pool_w, pool_scale, sgu_norm_g, sgu_w, sgu_b, mem_norm_g, w_kv, w_down, w_out, norm_f_g):
    depth = w_in.shape[0]
    assert depth == 1, "single-layer step"
    nb, seq, _ = x_prompt.shape
    ns, dec_seq, _ = x_sample.shape
    assert dec_seq == 1 and seq % SEQ_TILE == 0 and seq >= HIST_ROWS

    row = lambda a: a.reshape(1, -1)
    gin, pscale, sgug, gmem, gf = (row(norm_in_g[0]), row(pool_scale[0]), row(sgu_norm_g[0]),
                                   row(mem_norm_g[0]), row(norm_f_g))
    win, poolw, wkv, wdown, wout, sguw, sgub = (w_in[0], pool_w[0], w_kv[0], w_down[0], w_out[0],
                                                sgu_w[0], sgu_b[0])
    sgub_t = sgub.T

    x_tiles = x_sample.reshape(ns * ROW_TILES, LANES)
    hist_s = jnp.transpose(state_pool[0], (1, 0, 2))
    new_hist_s, hwin_s, k_rows, v_rows, kvb = _prep(hist_s, mem_prompt, gmem, wkv)

    y_prompt, hist_p, p_in_s, vn_s, _, _, _, _, _, _, y_tiles = _prompt_layer(
        x_prompt, kvb, gin, win, poolw, pscale, sgug, sguw, sgub_t, wdown, wout, gf, x_tiles, hwin_s,
        _to_kv_rows(cache_mem_k[0].reshape(ns, N_MEM, BRANCH_W)),
        _to_kv_rows(cache_mem_v[0].reshape(ns, N_MEM, BRANCH_W)))

    new_pool_p = hist_p[None, :, HIST_ROWS - POOL_HIST:, :]
    new_hist_s = lax.dynamic_update_slice(new_hist_s, p_in_s[None], (POOL_HIST - 1, 0, 0))
    new_pool_s = jnp.transpose(new_hist_s, (1, 0, 2))[None]
    kv_out = lambda a: _from_kv_rows(a.reshape(nb, N_MEM, KV_ROWS, LANES))[None]
    return (y_prompt, y_tiles.reshape(ns, 1, D_MODEL), new_pool_p, new_pool_s,
            kv_out(k_rows), kv_out(v_rows), vn_s.reshape(1, ns, 1, BRANCH_W))
```

```python
import jax
import jax.numpy as jnp
from jax import lax
from jax.experimental import pallas as pl
from jax.experimental.pallas import tpu as pltpu

D_MODEL = 1024
BRANCH_W = 1024
N_BRANCH = 3
N_PROJ = 7 + N_BRANCH
POOL_WINDOWS = (2, 4, 8, 16)
POOL_GROUPS = len(POOL_WINDOWS)
POOL_GW = BRANCH_W // POOL_GROUPS
POOL_HIST = max(POOL_WINDOWS) - 1
HIST_ROWS = POOL_HIST + 1
SGU_CHUNK = 128
SGU_GROUPS = 4
SGU_GW = BRANCH_W // SGU_GROUPS
N_MEM = 256
XA_HEADS = 4
XA_HD = BRANCH_W // XA_HEADS
EPS = 1e-6
PAST_LEN = 16384

SEQ_TILE = 256
MEM_REQ_BLOCK = 2
SAMPLE_ATTN_CHUNK = 32
LANES = 128
XA_LANE_TILES = XA_HD // LANES
KV_ROWS = XA_HEADS * XA_LANE_TILES
LOG2E = 1.4426950408889634
W_STAGE_SLOTS = 8
W_STAGE_ROWS = 128
W_STAGE_COLS = 1024
V7X_VMEM_LIMIT_BYTES = 62 * 1024 * 1024

F32 = jnp.float32
BF16 = jnp.bfloat16

_sigmoid = jax.nn.sigmoid


def _rmsnorm(x, g):
    return x * lax.rsqrt(jnp.mean(x * x, axis=-1, keepdims=True) + EPS) * g


def _silu(z):
    return z * _sigmoid(z)


def _dot(a, b):
    return jnp.dot(a, b, preferred_element_type=F32)


def _resident(shape):
    zeros = (0,) * len(shape)
    return pl.BlockSpec(shape, lambda *_: zeros, pipeline_mode=pl.Buffered(1))


def _to_kv_rows(a):
    lead = a.shape[:-1]
    a = a.reshape(*lead, XA_HEADS, XA_LANE_TILES, LANES)
    return jnp.swapaxes(a, -3, -2).reshape(*lead, KV_ROWS, LANES)


ROW_TILES = D_MODEL // LANES
NATURAL_ORDER = tuple(range(ROW_TILES))
KV_ROW_ORDER = tuple((j % XA_HEADS) * XA_LANE_TILES + j // XA_HEADS for j in range(KV_ROWS))


def _load_row_tiles(ref, order=NATURAL_ORDER):
    n = ref.shape[0] // len(order)
    tiles = [None] * len(order)
    for j, t in enumerate(order):
        tiles[t] = ref[pl.ds(j, n, stride=len(order)), :]
    return jnp.concatenate(tiles, axis=1)


def _store_row_tiles(ref, val, order=NATURAL_ORDER):
    n = val.shape[0]
    for j, t in enumerate(order):
        ref[pl.ds(j, n, stride=len(order)), :] = val[:, t * LANES:(t + 1) * LANES]


def _from_kv_rows(a):
    lead = a.shape[:-2]
    a = a.reshape(*lead, XA_LANE_TILES, XA_HEADS, LANES)
    return jnp.swapaxes(a, -3, -2).reshape(*lead, XA_HEADS, XA_HD)


def _mem_kv_body(mem_ref, g_ref, wkv_s, k_ref, v_ref, kvb_ref):
    n_req = mem_ref.shape[0]
    mem = jnp.concatenate([mem_ref[r] for r in range(n_req)], axis=0)
    kv = _dot(_rmsnorm(mem, g_ref[...]).astype(BF16), wkv_s[...])
    assert N_MEM == XA_HD
    for r in range(n_req):
        k = kv[r * N_MEM:(r + 1) * N_MEM, :BRANCH_W]
        v = kv[r * N_MEM:(r + 1) * N_MEM, BRANCH_W:]
        k_t = [k[:, h * XA_HD:(h + 1) * XA_HD].T for h in range(XA_HEADS)]
        kvb_ref[r] = jnp.concatenate(k_t + [v], axis=1).astype(BF16)
        for h in range(XA_HEADS):
            for lt in range(XA_LANE_TILES):
                cols = slice(h * XA_HD + lt * LANES, h * XA_HD + (lt + 1) * LANES)
                rows = pl.ds(lt * XA_HEADS + h, N_MEM, stride=KV_ROWS)
                k_ref[r, rows, :] = k[:, cols]
                v_ref[r, rows, :] = v[:, cols]


def _stage_weights(copies, stage_ref, sem_ref, after_chunk):
    slots = stage_ref.shape[0]
    ahead = slots - 1

    def chunk_copy(k):
        return pltpu.make_async_copy(copies[k][0], stage_ref.at[k % slots], sem_ref.at[k % slots])

    for k in range(min(ahead, len(copies))):
        chunk_copy(k).start()
    for k, (_, dst) in enumerate(copies):
        if k + ahead < len(copies):
            chunk_copy(k + ahead).start()
        chunk_copy(k).wait()
        dst[...] = stage_ref[k % slots].astype(BF16)
        after_chunk(k)


class _HbmWriter:
    def __init__(self, stage_ref, sem_ref):
        self.stage_ref, self.sem_ref, self.pending, self.count = stage_ref, sem_ref, [], 0

    def write(self, value, dst_hbm):
        slots = self.stage_ref.shape[0]
        slot = self.count % slots
        if len(self.pending) == slots:
            self.pending.pop(0).wait()
        self.stage_ref[slot] = value
        copy = pltpu.make_async_copy(self.stage_ref.at[slot], dst_hbm, self.sem_ref.at[slot])
        copy.start()
        self.pending.append(copy)
        self.count += 1

    def finish(self):
        for copy in self.pending:
            copy.wait()
        self.pending = []


def _first_step(xs_ref, hwin_ref, gin_ref, poolw_ref, pscale_ref, sgug_ref, sguw_ref, sgub_ref,
                win_hbm, wdown_hbm, wout_hbm, win_ref, wdown_ref, wout_ref, stage_ref, stage_sem,
                qs_ref, tiles_ref, tiles_sem, ostage_ref, ostage_sem,
                pin_hbm, vn_hbm, opool_hbm, osgu_hbm, az_hbm, gate_hbms):
    _, sr, sc = stage_ref.shape
    assert sc == BRANCH_W and D_MODEL % sr == 0
    k_slabs = D_MODEL // sr
    tiles = lambda ref: [(pl.ds(r * sr, sr), pl.ds(c * sc, sc))
                         for c in range(ref.shape[-1] // sc) for r in range(ref.shape[-2] // sr)]
    copies = [(win_hbm.at[r, c], win_ref.at[r, c]) for r, c in tiles(win_ref)]
    n_win = len(copies)
    copies += [(wdown_hbm.at[n, r, c], wdown_ref.at[n, r, c]) for n in range(N_BRANCH) for r, c in tiles(wdown_ref)]
    copies += [(wout_hbm.at[r, c], wout_ref.at[r, c]) for r, c in tiles(wout_ref)]

    hbs = _rmsnorm(_load_row_tiles(xs_ref), gin_ref[...]).astype(BF16)
    writer = _HbmWriter(ostage_ref, ostage_sem)
    kept = {}

    def finish_chunk(c, val):
        if c == 0:
            kept["p_in"] = val
            writer.write(val, pin_hbm)
        elif c == 1:
            p_in = kept["p_in"]
            mixed = []
            for g, w in enumerate(POOL_WINDOWS):
                sl = slice(g * POOL_GW, (g + 1) * POOL_GW)
                d = (p_in[:, sl] + hwin_ref[:, sl]) / float(min(w, PAST_LEN + 1)) - p_in[:, sl]
                mixed.append(_dot(d.astype(BF16), poolw_ref[g].astype(BF16)))
            writer.write(jnp.concatenate(mixed, axis=1) * pscale_ref[...] * _silu(val), opool_hbm)
        elif c == 2:
            kept["u"] = val
        elif c == 3:
            vn = _rmsnorm(val, sgug_ref[...])
            _store_row_tiles(tiles_ref, vn)
            kept["vn_copy"] = pltpu.make_async_copy(tiles_ref, vn_hbm, tiles_sem.at[0])
            kept["vn_copy"].start()
            gated = [vn[:, g * SGU_GW:(g + 1) * SGU_GW] * sguw_ref[g, 0:1, 0:1] + sgub_ref[0:1, g:g + 1]
                     for g in range(SGU_GROUPS)]
            kept["u_gated"] = kept["u"] * jnp.concatenate(gated, axis=1)
        elif c == 4:
            writer.write(kept["u_gated"] * _silu(val), osgu_hbm)
        elif c == 5:
            _store_row_tiles(qs_ref, val, KV_ROW_ORDER)
        elif c == 6:
            writer.write(val, az_hbm)
        else:
            writer.write(val, gate_hbms[c - 7])

    def after_chunk(k):
        if k >= n_win:
            return
        c, r = divmod(k, k_slabs)
        part = _dot(hbs[:, r * sr:(r + 1) * sr], win_ref[r * sr:(r + 1) * sr, c * sc:(c + 1) * sc])
        kept["acc"] = part if r == 0 else kept["acc"] + part
        if r == k_slabs - 1:
            finish_chunk(c, kept["acc"])

    _stage_weights(copies, stage_ref, stage_sem, after_chunk)
    writer.finish()
    kept["vn_copy"].wait()


def _sample_branch_copies(srcs, stage_ref, last_sem):
    ns = srcs[0].shape[0]
    per_slot = stage_ref.shape[1] // ns
    bufs = [stage_ref.at[k // per_slot, pl.ds((k % per_slot) * ns, ns), :] for k in range(len(srcs))]
    return [pltpu.make_async_copy(src, buf, last_sem.at[k]) for k, (src, buf) in enumerate(zip(srcs, bufs))], bufs


def _last_step(xs_ref, attn_ref, gf_ref, wdown_ref, wout_ref, branch_hbms, stage_ref, last_sem, tiles_ref, ys_hbm):
    copies, bufs = _sample_branch_copies(branch_hbms, stage_ref, last_sem)
    for copy in copies:
        copy.wait()
    o_pool, o_sgu, a_z, *gates = [buf[...] for buf in bufs]
    o_xa = _load_row_tiles(attn_ref, KV_ROW_ORDER) * _silu(a_z)
    merged = None
    for n, o in enumerate((o_pool, o_sgu, o_xa)):
        t = _sigmoid(gates[n]) * _dot(o.astype(BF16), wdown_ref[n])
        merged = t if merged is None else merged + t
    xn = _load_row_tiles(xs_ref) + _dot(merged.astype(BF16), wout_ref[...])
    _store_row_tiles(tiles_ref, _rmsnorm(xn, gf_ref[...]))
    out = pltpu.make_async_copy(tiles_ref, ys_hbm, last_sem.at[0])
    out.start()
    out.wait()


def _sample_attention(get_q, k_ref, v_ref, put_o):
    for r in range(k_ref.shape[0]):
        q = get_q(r) * (XA_HD ** -0.5 * LOG2E)
        m = l = acc = None
        for c in range(N_MEM // SAMPLE_ATTN_CHUNK):
            rows = slice(c * SAMPLE_ATTN_CHUNK, (c + 1) * SAMPLE_ATTN_CHUNK)
            part = k_ref[r, rows] * q
            part = part + pltpu.roll(part, XA_HEADS, axis=1)
            s = jnp.sum(part, axis=-1, keepdims=True)
            m_chunk = jnp.max(s, axis=0, keepdims=True)
            m_new = m_chunk if m is None else jnp.maximum(m, m_chunk)
            e = jnp.exp2(s - m_new)
            l_chunk = jnp.sum(e, axis=0)
            acc_chunk = jnp.sum(e * v_ref[r, rows], axis=0)
            if m is None:
                l, acc = l_chunk, acc_chunk
            else:
                rescale = jnp.exp2(m - m_new)[0]
                l, acc = l * rescale + l_chunk, acc * rescale + acc_chunk
            m = m_new
        put_o(r, acc / l)


def _prompt_kernel(x_ref, kv_ref, gin_ref, win_hbm, poolw_ref, pscale_ref, sgug_ref,
                   sguw_ref, sgub_ref, wdown_hbm, wout_hbm, gf_ref, xs_ref, hwin_ref, sk_hbm, sv_hbm,
                   y_ref, hist_ref, pin_hbm, vn_hbm, opool_hbm, osgu_hbm, az_hbm, g0_hbm, g1_hbm, g2_hbm, ys_hbm,
                   ext_ref, win_ref, wdown_ref, wout_ref, stage_ref, stage_sem, qs_ref, attn_ref,
                   ostage_ref, ostage_sem, last_sem, skv_ref, skv_sem):
    i = pl.program_id(1)
    ts = x_ref.shape[1]
    n_steps = pl.num_programs(0) * pl.num_programs(1)
    step = pl.program_id(0) * pl.num_programs(1) + i
    rb = skv_ref.shape[2]

    def sample_kv_copies(s):
        slot = s % 2
        rows = pl.ds(pl.multiple_of(s * rb, rb), rb)
        return [pltpu.make_async_copy(src.at[rows], skv_ref.at[slot, n], skv_sem.at[slot, n])
                for n, src in enumerate((sk_hbm, sv_hbm))]

    @pl.when(step == 0)
    def _():
        _first_step(xs_ref, hwin_ref, gin_ref, poolw_ref, pscale_ref, sgug_ref, sguw_ref, sgub_ref,
                    win_hbm, wdown_hbm, wout_hbm, win_ref, wdown_ref, wout_ref, stage_ref, stage_sem,
                    qs_ref, attn_ref, last_sem, ostage_ref, ostage_sem,
                    pin_hbm, vn_hbm, opool_hbm, osgu_hbm, az_hbm, (g0_hbm, g1_hbm, g2_hbm))
        for copy in sample_kv_copies(step):
            copy.start()

    @pl.when(step + 1 < n_steps)
    def _():
        for copy in sample_kv_copies(step + 1):
            copy.start()

    @pl.when(i == 0)
    def _():
        ext_ref[0:HIST_ROWS, :] = jnp.zeros((HIST_ROWS, BRANCH_W), F32)

    is_last = step == n_steps - 1
    branch_hbms = (opool_hbm, osgu_hbm, az_hbm, g0_hbm, g1_hbm, g2_hbm)

    @pl.when(is_last)
    def _():
        for copy in _sample_branch_copies(branch_hbms, stage_ref, last_sem)[0]:
            copy.start()

    q_rows = rb * KV_ROWS
    q_blk = qs_ref[pl.ds(pl.multiple_of(step * q_rows, q_rows), q_rows), :]

    def put_attention(r, out):
        attn_ref[pl.ds(pl.multiple_of(step * q_rows + r * KV_ROWS, KV_ROWS), KV_ROWS), :] = out

    for copy in sample_kv_copies(step):
        copy.wait()
    _sample_attention(lambda r: q_blk[r * KV_ROWS:(r + 1) * KV_ROWS, :],
                      skv_ref.at[step % 2, 0], skv_ref.at[step % 2, 1], put_attention)

    x = x_ref[0]
    hb = _rmsnorm(x, gin_ref[...]).astype(BF16)

    def proj(c):
        return _dot(hb, win_ref[:, c * BRANCH_W:(c + 1) * BRANCH_W])


    p_in = proj(0)
    p_z = proj(1)
    ext_ref[HIST_ROWS:HIST_ROWS + ts, :] = p_in
    pos = i * ts + lax.broadcasted_iota(jnp.int32, (ts, 1), 0)
    pooled = []
    for g, w in enumerate(POOL_WINDOWS):
        sl = slice(g * POOL_GW, (g + 1) * POOL_GW)
        win = ext_ref[:, sl]
        span = 1
        while span < w:
            win = win + pltpu.roll(win, span, axis=0)
            span *= 2
        inv_cnt = 1.0 / jnp.minimum(w, pos + 1).astype(F32)
        pooled.append((win[HIST_ROWS:, :] * inv_cnt - p_in[:, sl]).astype(BF16))
    ext_ref[0:HIST_ROWS, :] = p_in[ts - HIST_ROWS:, :]
    hist_ref[0] = p_in[ts - HIST_ROWS:, :]

    v = proj(3)
    u = proj(2)
    s_z = proj(4)
    mixed = [_dot(pooled[g], poolw_ref[g].astype(BF16)) for g in range(POOL_GROUPS)]
    o_pool = (jnp.concatenate(mixed, axis=1) * pscale_ref[...] * _silu(p_z)).astype(BF16)

    vnb = _rmsnorm(v, sgug_ref[...]).astype(BF16)
    tril = (lax.broadcasted_iota(jnp.int32, (SGU_CHUNK, SGU_CHUNK), 0)
            >= lax.broadcasted_iota(jnp.int32, (SGU_CHUNK, SGU_CHUNK), 1))
    ws = [jnp.where(tril, sguw_ref[g], 0.0).astype(BF16) for g in range(SGU_GROUPS)]
    q = proj(5)
    a_z = proj(6)
    rows = []
    for c in range(ts // SGU_CHUNK):
        rs = slice(c * SGU_CHUNK, (c + 1) * SGU_CHUNK)
        cols = [_dot(ws[g], vnb[rs, g * SGU_GW:(g + 1) * SGU_GW]) + sgub_ref[:, g:g + 1]
                for g in range(SGU_GROUPS)]
        rows.append(jnp.concatenate(cols, axis=1))
    o_sgu = (u * jnp.concatenate(rows, axis=0) * _silu(s_z)).astype(BF16)

    qb = q.astype(BF16)
    scores = [_dot(qb[:, hd * XA_HD:(hd + 1) * XA_HD], kv_ref[0, :, hd * N_MEM:(hd + 1) * N_MEM])
              * (XA_HD ** -0.5 * LOG2E) for hd in range(XA_HEADS)]
    gates = [proj(7 + n) for n in range(N_BRANCH)]
    heads = []
    for hd in range(XA_HEADS):
        s = scores[hd]
        e = jnp.exp2(s - jnp.max(s, axis=-1, keepdims=True))
        pr = e * (1.0 / jnp.sum(e, axis=-1, keepdims=True))
        heads.append(_dot(pr.astype(BF16), kv_ref[0, :, BRANCH_W + hd * XA_HD:BRANCH_W + (hd + 1) * XA_HD]))
    o_xa = (jnp.concatenate(heads, axis=1) * _silu(a_z)).astype(BF16)

    merged = None
    for n, o in enumerate((o_pool, o_sgu, o_xa)):
        t = _sigmoid(gates[n]) * _dot(o, wdown_ref[n])
        merged = t if merged is None else merged + t
    xn = x + _dot(merged.astype(BF16), wout_ref[...])
    y_ref[0] = _rmsnorm(xn, gf_ref[...])

    @pl.when(is_last)
    def _():
        _last_step(xs_ref, attn_ref, gf_ref, wdown_ref, wout_ref, branch_hbms, stage_ref, last_sem, qs_ref, ys_hbm)


def _prompt_layer(x, kvb, gin, win, poolw, pscale, sgug, sguw, sgub_t, wdown, wout, gf,
                  xs_tiles, hwin, sk, sv):
    nb, seq, _ = x.shape
    ts = SEQ_TILE
    n_tiles = seq // ts
    ns = sk.shape[0]
    rb = ns // (nb * n_tiles)
    assert rb * nb * n_tiles == ns
    assert nb * n_tiles > 1
    sample_row = jax.ShapeDtypeStruct((ns, BRANCH_W), F32)
    sample_tiles = jax.ShapeDtypeStruct((ns * ROW_TILES, LANES), F32)
    in_cols = win.shape[1]
    assert in_cols % W_STAGE_COLS == 0 and D_MODEL % W_STAGE_COLS == 0 and D_MODEL % W_STAGE_ROWS == 0
    step = lambda b, i: b * n_tiles + i
    hbm = lambda: pl.BlockSpec(memory_space=pl.ANY)
    return pl.pallas_call(
        _prompt_kernel,
        grid=(nb, n_tiles),
        in_specs=[pl.BlockSpec((1, ts, D_MODEL), lambda b, i: (b, i, 0)),
                  pl.BlockSpec((1, N_MEM, 2 * BRANCH_W), lambda b, i: (b, 0, 0)),
                  _resident((1, D_MODEL)),
                  hbm(),
                  _resident((POOL_GROUPS, POOL_GW, POOL_GW)),
                  _resident((1, BRANCH_W)),
                  _resident((1, BRANCH_W)),
                  _resident((SGU_GROUPS, SGU_CHUNK, SGU_CHUNK)),
                  _resident((SGU_CHUNK, SGU_GROUPS)),
                  hbm(),
                  hbm(),
                  _resident((1, D_MODEL)),
                  _resident((ns * ROW_TILES, LANES)),
                  _resident((ns, BRANCH_W)),
                  hbm(),
                  hbm()],
        out_specs=[pl.BlockSpec((1, ts, D_MODEL), lambda b, i: (b, i, 0)),
                   pl.BlockSpec((1, HIST_ROWS, BRANCH_W), lambda b, i: (b, 0, 0))] + [hbm()] * 9,
        out_shape=[jax.ShapeDtypeStruct((nb, seq, D_MODEL), F32),
                   jax.ShapeDtypeStruct((nb, HIST_ROWS, BRANCH_W), F32),
                   sample_row, sample_tiles] + [sample_row] * 6 + [sample_tiles],
        scratch_shapes=[pltpu.VMEM((HIST_ROWS + ts, BRANCH_W), F32),
                        pltpu.VMEM((D_MODEL, in_cols), BF16),
                        pltpu.VMEM((N_BRANCH, BRANCH_W, D_MODEL), BF16),
                        pltpu.VMEM((D_MODEL, D_MODEL), BF16),
                        pltpu.VMEM((W_STAGE_SLOTS, W_STAGE_ROWS, W_STAGE_COLS), F32),
                        pltpu.SemaphoreType.DMA((W_STAGE_SLOTS,)),
                        pltpu.VMEM((ns * KV_ROWS, LANES), F32),
                        pltpu.VMEM((ns * KV_ROWS, LANES), F32),
                        pltpu.VMEM((2, ns, BRANCH_W), F32),
                        pltpu.SemaphoreType.DMA((2,)),
                        pltpu.SemaphoreType.DMA((2 * N_BRANCH,)),
                        pltpu.VMEM((2, 2, rb, N_MEM, KV_ROWS, LANES), F32),
                        pltpu.SemaphoreType.DMA((2, 2))],
        compiler_params=pltpu.CompilerParams(dimension_semantics=("arbitrary", "arbitrary"),
                                             vmem_limit_bytes=V7X_VMEM_LIMIT_BYTES),
        name="prompt_layer",
    )(x, kvb, gin, win, poolw, pscale, sgug, sguw, sgub_t, wdown, wout, gf, xs_tiles, hwin, sk, sv)


def _prep_kernel(hist_hbm, mem_ref, gmem_ref, wkv_ref,
                 newhist_hbm, hwin_ref, k_ref, v_ref, kvb_ref, wkv_s, hist_s, hist_sem):
    c = pl.program_id(0)
    load = pltpu.make_async_copy(hist_hbm, hist_s, hist_sem.at[0])
    roll = pltpu.make_async_copy(hist_s.at[pl.ds(1, POOL_HIST - 1)], newhist_hbm.at[pl.ds(0, POOL_HIST - 1)],
                                 hist_sem.at[1])
    fill = pltpu.make_async_copy(hist_s.at[0], newhist_hbm.at[POOL_HIST - 1], hist_sem.at[2])

    @pl.when(c == 0)
    def _():
        load.start()
        wkv_s[...] = wkv_ref[...].astype(BF16)

    @pl.when(c == 1)
    def _():
        load.wait()
        for g, w in enumerate(POOL_WINDOWS):
            sl = slice(g * POOL_GW, (g + 1) * POOL_GW)
            win = hist_s[POOL_HIST - 1, :, sl]
            for j in range(2, w):
                win = win + hist_s[POOL_HIST - j, :, sl]
            hwin_ref[:, sl] = win
        roll.start()
        hist_s[0] = jnp.zeros(hist_s.shape[1:], F32)
        fill.start()

    _mem_kv_body(mem_ref, gmem_ref, wkv_s, k_ref, v_ref, kvb_ref)

    @pl.when(c == pl.num_programs(0) - 1)
    def _():
        roll.wait()
        fill.wait()


def _prep(hist, mem, gmem, wkv):
    n = hist.shape[1]
    nb = mem.shape[0]
    rq = MEM_REQ_BLOCK
    assert nb % rq == 0 and nb // rq >= 3
    rows_blk = lambda: pl.BlockSpec((rq, N_MEM * KV_ROWS, LANES), lambda b: (b, 0, 0))
    flat_blk = lambda: pl.BlockSpec((rq, N_MEM, 2 * BRANCH_W), lambda b: (b, 0, 0))
    hbm = lambda: pl.BlockSpec(memory_space=pl.ANY)
    return pl.pallas_call(
        _prep_kernel,
        grid=(nb // rq,),
        in_specs=[hbm(),
                  pl.BlockSpec((rq, N_MEM, D_MODEL), lambda b: (b, 0, 0)),
                  _resident((1, D_MODEL)),
                  _resident((D_MODEL, 2 * BRANCH_W))],
        out_specs=[hbm(), _resident((n, BRANCH_W)),
                   rows_blk(), rows_blk(), flat_blk()],
        out_shape=[jax.ShapeDtypeStruct(hist.shape, F32),
                   jax.ShapeDtypeStruct((n, BRANCH_W), F32),
                   jax.ShapeDtypeStruct((nb, N_MEM * KV_ROWS, LANES), F32),
                   jax.ShapeDtypeStruct((nb, N_MEM * KV_ROWS, LANES), F32),
                   jax.ShapeDtypeStruct((nb, N_MEM, 2 * BRANCH_W), BF16)],
        scratch_shapes=[pltpu.VMEM((D_MODEL, 2 * BRANCH_W), BF16),
                        pltpu.VMEM(hist.shape, F32),
                        pltpu.SemaphoreType.DMA((3,))],
        compiler_params=pltpu.CompilerParams(dimension_semantics=("arbitrary",),
                                             vmem_limit_bytes=V7X_VMEM_LIMIT_BYTES),
        name="prep",
    )(hist, mem, gmem, wkv)


def kernel(x_prompt, x_sample, state_pool, cache_mem_k, cache_mem_v, mem_prompt, norm_in_g, w_in,
           pool_w, pool_scale, sgu_norm_g, sgu_w, sgu_b, mem_norm_g, w_kv, w_down, w_out, norm_f_g):
    depth = w_in.shape[0]
    assert depth == 1, "single-layer step"
    nb, seq, _ = x_prompt.shape
    ns, dec_seq, _ = x_sample.shape
    assert dec_seq == 1 and seq % SEQ_TILE == 0 and seq >= HIST_ROWS

    row = lambda a: a.reshape(1, -1)
    gin, pscale, sgug, gmem, gf = (row(norm_in_g[0]), row(pool_scale[0]), row(sgu_norm_g[0]),
                                   row(mem_norm_g[0]), row(norm_f_g))
    win, poolw, wkv, wdown, wout, sguw, sgub = (w_in[0], pool_w[0], w_kv[0], w_down[0], w_out[0],
                                                sgu_w[0], sgu_b[0])
    sgub_t = sgub.T

    x_tiles = x_sample.reshape(ns * ROW_TILES, LANES)
    hist_s = jnp.transpose(state_pool[0], (1, 0, 2))
    new_hist_s, hwin_s, k_rows, v_rows, kvb = _prep(hist_s, mem_prompt, gmem, wkv)

    y_prompt, hist_p, p_in_s, vn_s, _, _, _, _, _, _, y_tiles = _prompt_layer(
        x_prompt, kvb, gin, win, poolw, pscale, sgug, sguw, sgub_t, wdown, wout, gf, x_tiles, hwin_s,
        _to_kv_rows(cache_mem_k[0].reshape(ns, N_MEM, BRANCH_W)),
        _to_kv_rows(cache_mem_v[0].reshape(ns, N_MEM, BRANCH_W)))

    new_pool_p = hist_p[None, :, HIST_ROWS - POOL_HIST:, :]
    new_hist_s = lax.dynamic_update_slice(new_hist_s, p_in_s[None], (POOL_HIST - 1, 0, 0))
    new_pool_s = jnp.transpose(new_hist_s, (1, 0, 2))[None]
    kv_out = lambda a: _from_kv_rows(a.reshape(nb, N_MEM, KV_ROWS, LANES))[None]
    return (y_prompt, y_tiles.reshape(ns, 1, D_MODEL), new_pool_p, new_pool_s,
            kv_out(k_rows), kv_out(v_rows), vn_s.reshape(1, ns, 1, BRANCH_W))
```

```python
import jax
import jax.numpy as jnp
from jax import lax
from jax.experimental import pallas as pl
from jax.experimental.pallas import tpu as pltpu

D_MODEL = 1024
BRANCH_W = 1024
N_BRANCH = 3
N_PROJ = 7 + N_BRANCH
POOL_WINDOWS = (2, 4, 8, 16)
POOL_GROUPS = len(POOL_WINDOWS)
POOL_GW = BRANCH_W // POOL_GROUPS
POOL_HIST = max(POOL_WINDOWS) - 1
HIST_ROWS = POOL_HIST + 1
SGU_CHUNK = 128
SGU_GROUPS = 4
SGU_GW = BRANCH_W // SGU_GROUPS
N_MEM = 256
XA_HEADS = 4
XA_HD = BRANCH_W // XA_HEADS
EPS = 1e-6
PAST_LEN = 16384

SEQ_TILE = 256
MEM_REQ_BLOCK = 2
SAMPLE_ATTN_CHUNK = 32
LANES = 128
XA_LANE_TILES = XA_HD // LANES
KV_ROWS = XA_HEADS * XA_LANE_TILES
LOG2E = 1.4426950408889634
W_STAGE_SLOTS = 8
W_STAGE_ROWS = 128
W_STAGE_COLS = 1024
V7X_VMEM_LIMIT_BYTES = 62 * 1024 * 1024

F32 = jnp.float32
BF16 = jnp.bfloat16

_sigmoid = jax.nn.sigmoid


def _rmsnorm(x, g):
    return x * lax.rsqrt(jnp.mean(x * x, axis=-1, keepdims=True) + EPS) * g


def _silu(z):
    return z * _sigmoid(z)


def _dot(a, b):
    return jnp.dot(a, b, preferred_element_type=F32)


def _resident(shape):
    zeros = (0,) * len(shape)
    return pl.BlockSpec(shape, lambda *_: zeros, pipeline_mode=pl.Buffered(1))


def _to_kv_rows(a):
    lead = a.shape[:-1]
    a = a.reshape(*lead, XA_HEADS, XA_LANE_TILES, LANES)
    return jnp.swapaxes(a, -3, -2).reshape(*lead, KV_ROWS, LANES)


ROW_TILES = D_MODEL // LANES
NATURAL_ORDER = tuple(range(ROW_TILES))
KV_ROW_ORDER = tuple((j % XA_HEADS) * XA_LANE_TILES + j // XA_HEADS for j in range(KV_ROWS))


def _load_row_tiles(ref, order=NATURAL_ORDER):
    n = ref.shape[0] // len(order)
    tiles = [None] * len(order)
    for j, t in enumerate(order):
        tiles[t] = ref[pl.ds(j, n, stride=len(order)), :]
    return jnp.concatenate(tiles, axis=1)


def _store_row_tiles(ref, val, order=NATURAL_ORDER):
    n = val.shape[0]
    for j, t in enumerate(order):
        ref[pl.ds(j, n, stride=len(order)), :] = val[:, t * LANES:(t + 1) * LANES]


def _from_kv_rows(a):
    lead = a.shape[:-2]
    a = a.reshape(*lead, XA_LANE_TILES, XA_HEADS, LANES)
    return jnp.swapaxes(a, -3, -2).reshape(*lead, XA_HEADS, XA_HD)


def _mem_kv_body(mem_ref, g_ref, wkv_s, k_ref, v_ref, kvb_ref):
    n_req = mem_ref.shape[0]
    mem = jnp.concatenate([mem_ref[r] for r in range(n_req)], axis=0)
    kv = _dot(_rmsnorm(mem, g_ref[...]).astype(BF16), wkv_s[...])
    assert N_MEM == XA_HD
    for r in range(n_req):
        k = kv[r * N_MEM:(r + 1) * N_MEM, :BRANCH_W]
        v = kv[r * N_MEM:(r + 1) * N_MEM, BRANCH_W:]
        k_t = [k[:, h * XA_HD:(h + 1) * XA_HD].T for h in range(XA_HEADS)]
        kvb_ref[r] = jnp.concatenate(k_t + [v], axis=1).astype(BF16)
        for h in range(XA_HEADS):
            for lt in range(XA_LANE_TILES):
                cols = slice(h * XA_HD + lt * LANES, h * XA_HD + (lt + 1) * LANES)
                rows = pl.ds(lt * XA_HEADS + h, N_MEM, stride=KV_ROWS)
                k_ref[r, rows, :] = k[:, cols]
                v_ref[r, rows, :] = v[:, cols]


def _stage_weights(copies, stage_ref, sem_ref, after_chunk):
    slots = stage_ref.shape[0]
    ahead = slots - 1

    def chunk_copy(k):
        return pltpu.make_async_copy(copies[k][0], stage_ref.at[k % slots], sem_ref.at[k % slots])

    for k in range(min(ahead, len(copies))):
        chunk_copy(k).start()
    for k, (_, dst) in enumerate(copies):
        if k + ahead < len(copies):
            chunk_copy(k + ahead).start()
        chunk_copy(k).wait()
        dst[...] = stage_ref[k % slots].astype(BF16)
        after_chunk(k)


class _HbmWriter:
    def __init__(self, stage_ref, sem_ref):
        self.stage_ref, self.sem_ref, self.pending, self.count = stage_ref, sem_ref, [], 0

    def write(self, value, dst_hbm):
        slots = self.stage_ref.shape[0]
        slot = self.count % slots
        if len(self.pending) == slots:
            self.pending.pop(0).wait()
        self.stage_ref[slot] = value
        copy = pltpu.make_async_copy(self.stage_ref.at[slot], dst_hbm, self.sem_ref.at[slot])
        copy.start()
        self.pending.append(copy)
        self.count += 1

    def finish(self):
        for copy in self.pending:
            copy.wait()
        self.pending = []


def _first_step(xs_ref, hwin_ref, gin_ref, poolw_ref, pscale_ref, sgug_ref, sguw_ref, sgub_ref,
                win_hbm, wdown_hbm, wout_hbm, win_ref, wdown_ref, wout_ref, stage_ref, stage_sem,
                qs_ref, tiles_ref, tiles_sem, ostage_ref, ostage_sem,
                pin_hbm, vn_hbm, opool_hbm, osgu_hbm, az_hbm, gate_hbms):
    _, sr, sc = stage_ref.shape
    assert sc == BRANCH_W and D_MODEL % sr == 0
    k_slabs = D_MODEL // sr
    tiles = lambda ref: [(pl.ds(r * sr, sr), pl.ds(c * sc, sc))
                         for c in range(ref.shape[-1] // sc) for r in range(ref.shape[-2] // sr)]
    copies = [(win_hbm.at[r, c], win_ref.at[r, c]) for r, c in tiles(win_ref)]
    n_win = len(copies)
    copies += [(wdown_hbm.at[n, r, c], wdown_ref.at[n, r, c]) for n in range(N_BRANCH) for r, c in tiles(wdown_ref)]
    copies += [(wout_hbm.at[r, c], wout_ref.at[r, c]) for r, c in tiles(wout_ref)]

    hbs = _rmsnorm(_load_row_tiles(xs_ref), gin_ref[...]).astype(BF16)
    writer = _HbmWriter(ostage_ref, ostage_sem)
    kept = {}

    def finish_chunk(c, val):
        if c == 0:
            kept["p_in"] = val
            writer.write(val, pin_hbm)
        elif c == 1:
            p_in = kept["p_in"]
            mixed = []
            for g, w in enumerate(POOL_WINDOWS):
                sl = slice(g * POOL_GW, (g + 1) * POOL_GW)
                d = (p_in[:, sl] + hwin_ref[:, sl]) / float(min(w, PAST_LEN + 1)) - p_in[:, sl]
                mixed.append(_dot(d.astype(BF16), poolw_ref[g].astype(BF16)))
            writer.write(jnp.concatenate(mixed, axis=1) * pscale_ref[...] * _silu(val), opool_hbm)
        elif c == 2:
            kept["u"] = val
        elif c == 3:
            vn = _rmsnorm(val, sgug_ref[...])
            _store_row_tiles(tiles_ref, vn)
            kept["vn_copy"] = pltpu.make_async_copy(tiles_ref, vn_hbm, tiles_sem.at[0])
            kept["vn_copy"].start()
            gated = [vn[:, g * SGU_GW:(g + 1) * SGU_GW] * sguw_ref[g, 0:1, 0:1] + sgub_ref[0:1, g:g + 1]
                     for g in range(SGU_GROUPS)]
            kept["u_gated"] = kept["u"] * jnp.concatenate(gated, axis=1)
        elif c == 4:
            writer.write(kept["u_gated"] * _silu(val), osgu_hbm)
        elif c == 5:
            _store_row_tiles(qs_ref, val, KV_ROW_ORDER)
        elif c == 6:
            writer.write(val, az_hbm)
        else:
            writer.write(val, gate_hbms[c - 7])

    def after_chunk(k):
        if k >= n_win:
            return
        c, r = divmod(k, k_slabs)
        part = _dot(hbs[:, r * sr:(r + 1) * sr], win_ref[r * sr:(r + 1) * sr, c * sc:(c + 1) * sc])
        kept["acc"] = part if r == 0 else kept["acc"] + part
        if r == k_slabs - 1:
            finish_chunk(c, kept["acc"])

    _stage_weights(copies, stage_ref, stage_sem, after_chunk)
    writer.finish()
    kept["vn_copy"].wait()


def _sample_branch_copies(srcs, stage_ref, last_sem):
    ns = srcs[0].shape[0]
    per_slot = stage_ref.shape[1] // ns
    bufs = [stage_ref.at[k // per_slot, pl.ds((k % per_slot) * ns, ns), :] for k in range(len(srcs))]
    return [pltpu.make_async_copy(src, buf, last_sem.at[k]) for k, (src, buf) in enumerate(zip(srcs, bufs))], bufs


def _last_step(xs_ref, attn_ref, gf_ref, wdown_ref, wout_ref, branch_hbms, stage_ref, last_sem, tiles_ref, ys_hbm):
    copies, bufs = _sample_branch_copies(branch_hbms, stage_ref, last_sem)
    for copy in copies:
        copy.wait()
    o_pool, o_sgu, a_z, *gates = [buf[...] for buf in bufs]
    o_xa = _load_row_tiles(attn_ref, KV_ROW_ORDER) * _silu(a_z)
    merged = None
    for n, o in enumerate((o_pool, o_sgu, o_xa)):
        t = _sigmoid(gates[n]) * _dot(o.astype(BF16), wdown_ref[n])
        merged = t if merged is None else merged + t
    xn = _load_row_tiles(xs_ref) + _dot(merged.astype(BF16), wout_ref[...])
    _store_row_tiles(tiles_ref, _rmsnorm(xn, gf_ref[...]))
    out = pltpu.make_async_copy(tiles_ref, ys_hbm, last_sem.at[0])
    out.start()
    out.wait()


def _sample_attention(get_q, k_ref, v_ref, put_o):
    for r in range(k_ref.shape[0]):
        q = get_q(r) * (XA_HD ** -0.5 * LOG2E)
        m = l = acc = None
        for c in range(N_MEM // SAMPLE_ATTN_CHUNK):
            rows = slice(c * SAMPLE_ATTN_CHUNK, (c + 1) * SAMPLE_ATTN_CHUNK)
            part = k_ref[r, rows] * q
            part = part + pltpu.roll(part, XA_HEADS, axis=1)
            s = jnp.sum(part, axis=-1, keepdims=True)
            m_chunk = jnp.max(s, axis=0, keepdims=True)
            m_new = m_chunk if m is None else jnp.maximum(m, m_chunk)
            e = jnp.exp2(s - m_new)
            l_chunk = jnp.sum(e, axis=0)
            acc_chunk = jnp.sum(e * v_ref[r, rows], axis=0)
            if m is None:
                l, acc = l_chunk, acc_chunk
            else:
                rescale = jnp.exp2(m - m_new)[0]
                l, acc = l * rescale + l_chunk, acc * rescale + acc_chunk
            m = m_new
        put_o(r, acc / l)


def _prompt_kernel(x_ref, kv_ref, gin_ref, win_hbm, poolw_ref, pscale_ref, sgug_ref,
                   sguw_ref, sgub_ref, wdown_hbm, wout_hbm, gf_ref, xs_ref, hwin_ref, sk_hbm, sv_hbm,
                   y_ref, hist_ref, pin_hbm, vn_hbm, opool_hbm, osgu_hbm, az_hbm, g0_hbm, g1_hbm, g2_hbm, ys_hbm,
                   ext_ref, win_ref, wdown_ref, wout_ref, stage_ref, stage_sem, qs_ref, attn_ref,
                   ostage_ref, ostage_sem, last_sem, skv_ref, skv_sem):
    i = pl.program_id(1)
    ts = x_ref.shape[1]
    n_steps = pl.num_programs(0) * pl.num_programs(1)
    step = pl.program_id(0) * pl.num_programs(1) + i
    rb = skv_ref.shape[2]

    def sample_kv_copies(s):
        slot = s % 2
        rows = pl.ds(pl.multiple_of(s * rb, rb), rb)
        return [pltpu.make_async_copy(src.at[rows], skv_ref.at[slot, n], skv_sem.at[slot, n])
                for n, src in enumerate((sk_hbm, sv_hbm))]

    @pl.when(step == 0)
    def _():
        _first_step(xs_ref, hwin_ref, gin_ref, poolw_ref, pscale_ref, sgug_ref, sguw_ref, sgub_ref,
                    win_hbm, wdown_hbm, wout_hbm, win_ref, wdown_ref, wout_ref, stage_ref, stage_sem,
                    qs_ref, attn_ref, last_sem, ostage_ref, ostage_sem,
                    pin_hbm, vn_hbm, opool_hbm, osgu_hbm, az_hbm, (g0_hbm, g1_hbm, g2_hbm))
        for copy in sample_kv_copies(step):
            copy.start()


    @pl.when(i == 0)
    def _():
        ext_ref[0:HIST_ROWS, :] = jnp.zeros((HIST_ROWS, BRANCH_W), F32)

    is_last = step == n_steps - 1
    branch_hbms = (opool_hbm, osgu_hbm, az_hbm, g0_hbm, g1_hbm, g2_hbm)

    @pl.when(is_last)
    def _():
        for copy in _sample_branch_copies(branch_hbms, stage_ref, last_sem)[0]:
            copy.start()

    q_rows = rb * KV_ROWS
    q_blk = qs_ref[pl.ds(pl.multiple_of(step * q_rows, q_rows), q_rows), :]

    def put_attention(r, out):
        attn_ref[pl.ds(pl.multiple_of(step * q_rows + r * KV_ROWS, KV_ROWS), KV_ROWS), :] = out

    for copy in sample_kv_copies(step):
        copy.wait()

    @pl.when(step + 1 < n_steps)
    def _():
        for copy in sample_kv_copies(step + 1):
            copy.start()

    _sample_attention(lambda r: q_blk[r * KV_ROWS:(r + 1) * KV_ROWS, :],
                      skv_ref.at[step % 2, 0], skv_ref.at[step % 2, 1], put_attention)

    x = x_ref[0]
    hb = _rmsnorm(x, gin_ref[...]).astype(BF16)

    def proj(c):
        return _dot(hb, win_ref[:, c * BRANCH_W:(c + 1) * BRANCH_W])


    p_in = proj(0)
    p_z = proj(1)
    ext_ref[HIST_ROWS:HIST_ROWS + ts, :] = p_in
    pos = i * ts + lax.broadcasted_iota(jnp.int32, (ts, 1), 0)
    pooled = []
    for g, w in enumerate(POOL_WINDOWS):
        sl = slice(g * POOL_GW, (g + 1) * POOL_GW)
        win = ext_ref[:, sl]
        span = 1
        while span < w:
            win = win + pltpu.roll(win, span, axis=0)
            span *= 2
        inv_cnt = 1.0 / jnp.minimum(w, pos + 1).astype(F32)
        pooled.append((win[HIST_ROWS:, :] * inv_cnt - p_in[:, sl]).astype(BF16))
    ext_ref[0:HIST_ROWS, :] = p_in[ts - HIST_ROWS:, :]
    hist_ref[0] = p_in[ts - HIST_ROWS:, :]

    v = proj(3)
    u = proj(2)
    s_z = proj(4)
    mixed = [_dot(pooled[g], poolw_ref[g].astype(BF16)) for g in range(POOL_GROUPS)]
    o_pool = (jnp.concatenate(mixed, axis=1) * pscale_ref[...] * _silu(p_z)).astype(BF16)

    vnb = _rmsnorm(v, sgug_ref[...]).astype(BF16)
    tril = (lax.broadcasted_iota(jnp.int32, (SGU_CHUNK, SGU_CHUNK), 0)
            >= lax.broadcasted_iota(jnp.int32, (SGU_CHUNK, SGU_CHUNK), 1))
    ws = [jnp.where(tril, sguw_ref[g], 0.0).astype(BF16) for g in range(SGU_GROUPS)]
    q = proj(5)
    a_z = proj(6)
    rows = []
    for c in range(ts // SGU_CHUNK):
        rs = slice(c * SGU_CHUNK, (c + 1) * SGU_CHUNK)
        cols = [_dot(ws[g], vnb[rs, g * SGU_GW:(g + 1) * SGU_GW]) + sgub_ref[:, g:g + 1]
                for g in range(SGU_GROUPS)]
        rows.append(jnp.concatenate(cols, axis=1))
    o_sgu = (u * jnp.concatenate(rows, axis=0) * _silu(s_z)).astype(BF16)

    qb = q.astype(BF16)
    scores = [_dot(qb[:, hd * XA_HD:(hd + 1) * XA_HD], kv_ref[0, :, hd * N_MEM:(hd + 1) * N_MEM])
              * (XA_HD ** -0.5 * LOG2E) for hd in range(XA_HEADS)]
    gates = [proj(7 + n) for n in range(N_BRANCH)]
    heads = []
    for hd in range(XA_HEADS):
        s = scores[hd]
        e = jnp.exp2(s - jnp.max(s, axis=-1, keepdims=True))
        pr = e * (1.0 / jnp.sum(e, axis=-1, keepdims=True))
        heads.append(_dot(pr.astype(BF16), kv_ref[0, :, BRANCH_W + hd * XA_HD:BRANCH_W + (hd + 1) * XA_HD]))
    o_xa = (jnp.concatenate(heads, axis=1) * _silu(a_z)).astype(BF16)

    merged = None
    for n, o in enumerate((o_pool, o_sgu, o_xa)):
        t = _sigmoid(gates[n]) * _dot(o, wdown_ref[n])
        merged = t if merged is None else merged + t
    xn = x + _dot(merged.astype(BF16), wout_ref[...])
    y_ref[0] = _rmsnorm(xn, gf_ref[...])

    @pl.when(is_last)
    def _():
        _last_step(xs_ref, attn_ref, gf_ref, wdown_ref, wout_ref, branch_hbms, stage_ref, last_sem, qs_ref, ys_hbm)


def _prompt_layer(x, kvb, gin, win, poolw, pscale, sgug, sguw, sgub_t, wdown, wout, gf,
                  xs_tiles, hwin, sk, sv):
    nb, seq, _ = x.shape
    ts = SEQ_TILE
    n_tiles = seq // ts
    ns = sk.shape[0]
    rb = ns // (nb * n_tiles)
    assert rb * nb * n_tiles == ns
    assert nb * n_tiles > 1
    sample_row = jax.ShapeDtypeStruct((ns, BRANCH_W), F32)
    sample_tiles = jax.ShapeDtypeStruct((ns * ROW_TILES, LANES), F32)
    in_cols = win.shape[1]
    assert in_cols % W_STAGE_COLS == 0 and D_MODEL % W_STAGE_COLS == 0 and D_MODEL % W_STAGE_ROWS == 0
    step = lambda b, i: b * n_tiles + i
    hbm = lambda: pl.BlockSpec(memory_space=pl.ANY)
    return pl.pallas_call(
        _prompt_kernel,
        grid=(nb, n_tiles),
        in_specs=[pl.BlockSpec((1, ts, D_MODEL), lambda b, i: (b, i, 0)),
                  pl.BlockSpec((1, N_MEM, 2 * BRANCH_W), lambda b, i: (b, 0, 0)),
                  _resident((1, D_MODEL)),
                  hbm(),
                  _resident((POOL_GROUPS, POOL_GW, POOL_GW)),
                  _resident((1, BRANCH_W)),
                  _resident((1, BRANCH_W)),
                  _resident((SGU_GROUPS, SGU_CHUNK, SGU_CHUNK)),
                  _resident((SGU_CHUNK, SGU_GROUPS)),
                  hbm(),
                  hbm(),
                  _resident((1, D_MODEL)),
                  _resident((ns * ROW_TILES, LANES)),
                  _resident((ns, BRANCH_W)),
                  hbm(),
                  hbm()],
        out_specs=[pl.BlockSpec((1, ts, D_MODEL), lambda b, i: (b, i, 0)),
                   pl.BlockSpec((1, HIST_ROWS, BRANCH_W), lambda b, i: (b, 0, 0))] + [hbm()] * 9,
        out_shape=[jax.ShapeDtypeStruct((nb, seq, D_MODEL), F32),
                   jax.ShapeDtypeStruct((nb, HIST_ROWS, BRANCH_W), F32),
                   sample_row, sample_tiles] + [sample_row] * 6 + [sample_tiles],
        scratch_shapes=[pltpu.VMEM((HIST_ROWS + ts, BRANCH_W), F32),
                        pltpu.VMEM((D_MODEL, in_cols), BF16),
                        pltpu.VMEM((N_BRANCH, BRANCH_W, D_MODEL), BF16),
                        pltpu.VMEM((D_MODEL, D_MODEL), BF16),
                        pltpu.VMEM((W_STAGE_SLOTS, W_STAGE_ROWS, W_STAGE_COLS), F32),
                        pltpu.SemaphoreType.DMA((W_STAGE_SLOTS,)),
                        pltpu.VMEM((ns * KV_ROWS, LANES), F32),
                        pltpu.VMEM((ns * KV_ROWS, LANES), F32),
                        pltpu.VMEM((2, ns, BRANCH_W), F32),
                        pltpu.SemaphoreType.DMA((2,)),
                        pltpu.SemaphoreType.DMA((2 * N_BRANCH,)),
                        pltpu.VMEM((2, 2, rb, N_MEM, KV_ROWS, LANES), F32),
                        pltpu.SemaphoreType.DMA((2, 2))],
        compiler_params=pltpu.CompilerParams(dimension_semantics=("arbitrary", "arbitrary"),
                                             vmem_limit_bytes=V7X_VMEM_LIMIT_BYTES),
        name="prompt_layer",
    )(x, kvb, gin, win, poolw, pscale, sgug, sguw, sgub_t, wdown, wout, gf, xs_tiles, hwin, sk, sv)


def _prep_kernel(hist_hbm, mem_ref, gmem_ref, wkv_ref,
                 newhist_hbm, hwin_ref, k_ref, v_ref, kvb_ref, wkv_s, hist_s, hist_sem):
    c = pl.program_id(0)
    load = pltpu.make_async_copy(hist_hbm, hist_s, hist_sem.at[0])
    roll = pltpu.make_async_copy(hist_s.at[pl.ds(1, POOL_HIST - 1)], newhist_hbm.at[pl.ds(0, POOL_HIST - 1)],
                                 hist_sem.at[1])
    fill = pltpu.make_async_copy(hist_s.at[0], newhist_hbm.at[POOL_HIST - 1], hist_sem.at[2])

    @pl.when(c == 0)
    def _():
        load.start()
        wkv_s[...] = wkv_ref[...].astype(BF16)

    @pl.when(c == 1)
    def _():
        load.wait()
        for g, w in enumerate(POOL_WINDOWS):
            sl = slice(g * POOL_GW, (g + 1) * POOL_GW)
            win = hist_s[POOL_HIST - 1, :, sl]
            for j in range(2, w):
                win = win + hist_s[POOL_HIST - j, :, sl]
            hwin_ref[:, sl] = win
        roll.start()
        hist_s[0] = jnp.zeros(hist_s.shape[1:], F32)
        fill.start()

    _mem_kv_body(mem_ref, gmem_ref, wkv_s, k_ref, v_ref, kvb_ref)

    @pl.when(c == pl.num_programs(0) - 1)
    def _():
        roll.wait()
        fill.wait()


def _prep(hist, mem, gmem, wkv):
    n = hist.shape[1]
    nb = mem.shape[0]
    rq = MEM_REQ_BLOCK
    assert nb % rq == 0 and nb // rq >= 3
    rows_blk = lambda: pl.BlockSpec((rq, N_MEM * KV_ROWS, LANES), lambda b: (b, 0, 0))
    flat_blk = lambda: pl.BlockSpec((rq, N_MEM, 2 * BRANCH_W), lambda b: (b, 0, 0))
    hbm = lambda: pl.BlockSpec(memory_space=pl.ANY)
    return pl.pallas_call(
        _prep_kernel,
        grid=(nb // rq,),
        in_specs=[hbm(),
                  pl.BlockSpec((rq, N_MEM, D_MODEL), lambda b: (b, 0, 0)),
                  _resident((1, D_MODEL)),
                  _resident((D_MODEL, 2 * BRANCH_W))],
        out_specs=[hbm(), _resident((n, BRANCH_W)),
                   rows_blk(), rows_blk(), flat_blk()],
        out_shape=[jax.ShapeDtypeStruct(hist.shape, F32),
                   jax.ShapeDtypeStruct((n, BRANCH_W), F32),
                   jax.ShapeDtypeStruct((nb, N_MEM * KV_ROWS, LANES), F32),
                   jax.ShapeDtypeStruct((nb, N_MEM * KV_ROWS, LANES), F32),
                   jax.ShapeDtypeStruct((nb, N_MEM, 2 * BRANCH_W), BF16)],
        scratch_shapes=[pltpu.VMEM((D_MODEL, 2 * BRANCH_W), BF16),
                        pltpu.VMEM(hist.shape, F32),
                        pltpu.SemaphoreType.DMA((3,))],
        compiler_params=pltpu.CompilerParams(dimension_semantics=("arbitrary",),
                                             vmem_limit_bytes=V7X_VMEM_LIMIT_BYTES),
        name="prep",
    )(hist, mem, gmem, wkv)


def kernel(x_prompt, x_sample, state_pool, cache_mem_k, cache_mem_v, mem_prompt, norm_in_g, w_in,
           pool_w, pool_scale, sgu_norm_g, sgu_w, sgu_b, mem_norm_g, w_kv, w_down, w_out, norm_f_g):
    depth = w_in.shape[0]
    assert depth == 1, "single-layer step"
    nb, seq, _ = x_prompt.shape
    ns, dec_seq, _ = x_sample.shape
    assert dec_seq == 1 and seq % SEQ_TILE == 0 and seq >= HIST_ROWS

    row = lambda a: a.reshape(1, -1)
    gin, pscale, sgug, gmem, gf = (row(norm_in_g[0]), row(pool_scale[0]), row(sgu_norm_g[0]),
                                   row(mem_norm_g[0]), row(norm_f_g))
    win, poolw, wkv, wdown, wout, sguw, sgub = (w_in[0], pool_w[0], w_kv[0], w_down[0], w_out[0],
                                                sgu_w[0], sgu_b[0])
    sgub_t = sgub.T

    x_tiles = x_sample.reshape(ns * ROW_TILES, LANES)
    hist_s = jnp.transpose(state_pool[0], (1, 0, 2))
    new_hist_s, hwin_s, k_rows, v_rows, kvb = _prep(hist_s, mem_prompt, gmem, wkv)

    y_prompt, hist_p, p_in_s, vn_s, _, _, _, _, _, _, y_tiles = _prompt_layer(
        x_prompt, kvb, gin, win, poolw, pscale, sgug, sguw, sgub_t, wdown, wout, gf, x_tiles, hwin_s,
        _to_kv_rows(cache_mem_k[0].reshape(ns, N_MEM, BRANCH_W)),
        _to_kv_rows(cache_mem_v[0].reshape(ns, N_MEM, BRANCH_W)))

    new_pool_p = hist_p[None, :, HIST_ROWS - POOL_HIST:, :]
    new_hist_s = lax.dynamic_update_slice(new_hist_s, p_in_s[None], (POOL_HIST - 1, 0, 0))
    new_pool_s = jnp.transpose(new_hist_s, (1, 0, 2))[None]
    kv_out = lambda a: _from_kv_rows(a.reshape(nb, N_MEM, KV_ROWS, LANES))[None]
    return (y_prompt, y_tiles.reshape(ns, 1, D_MODEL), new_pool_p, new_pool_s,
            kv_out(k_rows), kv_out(v_rows), vn_s.reshape(1, ns, 1, BRANCH_W))
```
